```python
import math
import jax
import jax.numpy as jnp
from jax import lax
import numpy as np

D_MODEL = 2048
BATCH = 8
SEQ = 2048
DEPTH = 2
DEC_BATCH = 32
DEC_SEQ = 8
PAST_LEN = 8192
PAGE_SIZE = 128

GDN_HEADS = 8
GDN_DK = 128
GDN_DV = 128
GDN_CONV = 4
GDN_CHUNK = 64
DA_HEADS = 8
DA_DK = 64
DA_DV = 2 * DA_DK
ROPE_DIM = DA_DK // 4
ROPE_THETA = 500000.0
Q_BLOCK = 128
DA_LAMBDA_INIT = 0.8 - 0.6 * math.exp(-0.3 * 0)
RW_HEAD = 64
RW_HEADS = D_MODEL // RW_HEAD
RW_DECAY_LORA = 96
RW_AAA_LORA = 96
RW_GATE_LORA = 256
RW_GN_EPS = 1e-5 * RW_HEAD
D_FF = 7168
N_EXPERTS = 8
TOP_K = 2
MOE_BLOCK = 128
NORM_EPS = 1e-6

GDN_QK = GDN_HEADS * GDN_DK
GDN_VW = GDN_HEADS * GDN_DV
GDN_QKV = 2 * GDN_QK + GDN_VW
DA_QK = DA_HEADS * 2 * DA_DK
DA_VW = DA_HEADS * DA_DV
IN0_SPLITS = (GDN_QKV, GDN_QKV + GDN_VW, GDN_QKV + GDN_VW + GDN_HEADS, GDN_QKV + GDN_VW + 2 * GDN_HEADS,
              GDN_QKV + GDN_VW + 2 * GDN_HEADS + DA_QK, GDN_QKV + GDN_VW + 2 * GDN_HEADS + 2 * DA_QK)
IN0_WIDTH = GDN_QKV + GDN_VW + 2 * GDN_HEADS + 2 * DA_QK + DA_VW
MIX0_WIDTH = GDN_VW + DA_VW

kernel_name = 'hybrid_gdn_diffattn_rwkv7_moe_adaln_step'


def rms_norm(x, w, eps=NORM_EPS):
    xf = x.astype(jnp.float32)
    y = xf * lax.rsqrt(jnp.mean(xf * xf, axis=-1, keepdims=True) + eps)
    return (y * w.astype(jnp.float32)).astype(x.dtype)


def l2_normalize(x, eps=1e-12):
    xf = x.astype(jnp.float32)
    return xf * lax.rsqrt(jnp.sum(xf * xf, axis=-1, keepdims=True) + eps)


def ada_modulation(c, w, b):
    mod = jnp.dot(jax.nn.silu(c), w) + b
    return jnp.split(mod[:, None, :], 6, axis=-1)


def modulate(x, g, shift, scale):
    return rms_norm(x, g) * (1.0 + scale) + shift


def swiglu(x, w_gate, w_up, w_down):
    return jnp.dot(jax.nn.silu(jnp.dot(x, w_gate)) * jnp.dot(x, w_up), w_down)


def causal_depthwise_conv(x_ext, w):
    return lax.conv_general_dilated(
        x_ext, w.astype(x_ext.dtype)[:, None, :], window_strides=(1,), padding='VALID',
        dimension_numbers=('NWC', 'WIO', 'NWC'), feature_group_count=x_ext.shape[-1])


def partial_rotary(x, pos):
    half = ROPE_DIM // 2
    inv_freq = ROPE_THETA ** (-jnp.arange(half, dtype=jnp.float32) * (2.0 / ROPE_DIM))
    ang = pos.astype(jnp.float32)[:, None] * inv_freq
    cos = jnp.cos(ang)[:, None, None, :]
    sin = jnp.sin(ang)[:, None, None, :]
    xr = x[..., :ROPE_DIM].astype(jnp.float32)
    x1, x2 = xr[..., :half], xr[..., half:]
    rot = jnp.concatenate([x1 * cos - x2 * sin, x2 * cos + x1 * sin], axis=-1)
    return jnp.concatenate([rot.astype(x.dtype), x[..., ROPE_DIM:]], axis=-1)


def gated_delta_chunked(q, k, v, g, beta, s0):
    B, L, H, DK = q.shape
    DV = v.shape[-1]
    C = GDN_CHUNK
    n = -(-L // C)
    pad = n * C - L

    def blocks(t):
        t = jnp.pad(t, [(0, 0), (0, pad)] + [(0, 0)] * (t.ndim - 2))
        t = t.reshape((B, n, C) + t.shape[2:])
        return jnp.moveaxis(t, [1, 3], [0, 2])

    qc, kc, vc, g_raw, bc = (blocks(t) for t in (q, k, v, g, beta))
    gc = jnp.cumsum(g_raw, axis=-1)
    causal = jnp.tril(jnp.ones((C, C), bool))
    strict = jnp.tril(jnp.ones((C, C), bool), -1)
    decay = jnp.exp(jnp.where(causal, gc[..., :, None] - gc[..., None, :], -jnp.inf))
    kb = kc * bc[..., None]
    a_mat = jnp.where(strict, jnp.einsum('nbhid,nbhjd->nbhij', kb, kc) * decay, 0.0) + jnp.eye(C, dtype=jnp.float32)
    u = lax.linalg.triangular_solve(a_mat, vc * bc[..., None], left_side=True, lower=True, unit_diagonal=True)
    w = lax.linalg.triangular_solve(a_mat, kb * jnp.exp(gc)[..., None], left_side=True, lower=True, unit_diagonal=True)
    qk = jnp.einsum('nbhid,nbhjd->nbhij', qc, kc) * decay

    def chunk_step(S, inp):
        q_c, k_c, u_c, w_c, g_c, qk_c = inp
        v_new = u_c - jnp.einsum('bhcd,bhde->bhce', w_c, S)
        o_c = (jnp.einsum('bhcd,bhde->bhce', q_c * jnp.exp(g_c)[..., None], S)
               + jnp.einsum('bhij,bhje->bhie', qk_c, v_new))
        g_last = g_c[..., -1:]
        S = S * jnp.exp(g_last)[..., None] + jnp.einsum(
            'bhcd,bhce->bhde', k_c * jnp.exp(g_last - g_c)[..., None], v_new)
        return S, o_c

    S, o = lax.scan(chunk_step, s0, (qc, kc, u, w, gc, qk))
    o = jnp.moveaxis(o, [0, 2], [1, 3]).reshape(B, n * C, H, DV)[:, :L]
    return o, S


def gated_deltanet(qkv_raw, z, b_raw, a_raw, conv_state, s0, conv_w, a_log, dt_bias, norm_w):
    B, L, _ = qkv_raw.shape
    ext = jnp.concatenate([conv_state.astype(qkv_raw.dtype), qkv_raw], axis=1)
    new_conv = ext[:, ext.shape[1] - (GDN_CONV - 1):]
    qkv = jax.nn.silu(causal_depthwise_conv(ext, conv_w))
    q, k, v = jnp.split(qkv, [GDN_QK, 2 * GDN_QK], axis=-1)
    q = l2_normalize(q.reshape(B, L, GDN_HEADS, GDN_DK)) * GDN_DK ** -0.5
    k = l2_normalize(k.reshape(B, L, GDN_HEADS, GDN_DK))
    v = v.reshape(B, L, GDN_HEADS, GDN_DV).astype(jnp.float32)
    beta = jax.nn.sigmoid(b_raw.astype(jnp.float32))
    g = -jnp.exp(a_log.astype(jnp.float32)) * jax.nn.softplus(a_raw.astype(jnp.float32) + dt_bias.astype(jnp.float32))
    o, s_new = gated_delta_chunked(q, k, v, g, beta, s0.astype(jnp.float32))
    o = rms_norm(o, norm_w) * jax.nn.silu(z.astype(jnp.float32).reshape(B, L, GDN_HEADS, GDN_DV))
    return o.reshape(B, L, GDN_VW).astype(qkv_raw.dtype), new_conv, s_new.astype(s0.dtype)


def diff_attend_block(q, segments, lam):
    scores = []
    for k, _, mask in segments:
        s = jnp.einsum('bqhmd,bshmd->bhmqs', q, k).astype(jnp.float32) * DA_DK ** -0.5
        scores.append(s if mask is None else jnp.where(mask, s, -jnp.inf))
    p = jax.nn.softmax(jnp.concatenate(scores, axis=-1), axis=-1)
    a = p[:, :, 0] - lam * p[:, :, 1]
    outs = []
    start = 0
    for k, v, _ in segments:
        n = k.shape[1]
        outs.append(jnp.einsum('bhqs,bshd->bqhd', a[..., start:start + n].astype(v.dtype), v))
        start += n
    out = outs[0]
    for o in outs[1:]:
        out = out + o
    return out


def diff_attention(q, k, v, pos, past_k, past_v, lq1, lk1, lq2, lk2, subln_w):
    f32 = jnp.float32
    lam = (jnp.exp(jnp.sum(lq1.astype(f32) * lk1.astype(f32)))
           - jnp.exp(jnp.sum(lq2.astype(f32) * lk2.astype(f32))) + DA_LAMBDA_INIT)
    B, L = q.shape[:2]
    if past_k is None:
        nb = L // Q_BLOCK
        q_blocks = q.reshape(B, nb, Q_BLOCK, DA_HEADS, 2, DA_DK).swapaxes(0, 1)

        def one_block(blk):
            q_b, q_pos = blk
            return diff_attend_block(q_b, [(k, v, pos[None, :] <= q_pos[:, None])], lam)

        o = lax.map(one_block, (q_blocks, pos.reshape(nb, Q_BLOCK)))
        o = o.swapaxes(0, 1).reshape(B, L, DA_HEADS, DA_DV)
    else:
        o = diff_attend_block(q, [(past_k.astype(k.dtype), past_v.astype(v.dtype), None),
                                  (k, v, pos[None, :] <= pos[:, None])], lam)
    return rms_norm(o, subln_w) * (1.0 - DA_LAMBDA_INIT)


def wkv7_scan(r, w, k, v, a, b, s0):
    def step(S, inp):
        r_t, w_t, k_t, v_t, a_t, b_t = inp
        sa = jnp.einsum('bhvk,bhk->bhv', S, a_t)
        S = S * w_t[:, :, None, :] + sa[..., None] * b_t[:, :, None, :] + v_t[..., None] * k_t[:, :, None, :]
        return S, jnp.einsum('bhvk,bhk->bhv', S, r_t)

    S, y = lax.scan(step, s0, tuple(t.swapaxes(0, 1) for t in (r, w, k, v, a, b)))
    return y.swapaxes(0, 1), S


def rwkv7_time_mix(h, shift_state, wkv_state, mu, w0, w1, w2, a0, a1, a2, g1, g2, k_k, k_a, r_k,
                   w_r, w_k, w_v, w_o, ln_w, ln_b):
    f32 = jnp.float32
    B, L, D = h.shape
    prev = jnp.concatenate([shift_state[:, None].astype(h.dtype), h[:, :-1]], axis=1)
    xx = prev - h
    xr, xw, xk, xv, xa, xg = [h + xx * mu[i] for i in range(6)]
    r = jnp.dot(xr, w_r).astype(f32)
    w_log = -jax.nn.softplus(-(w0 + jnp.dot(jnp.tanh(jnp.dot(xw, w1)), w2)).astype(f32)) - 0.5
    decay = jnp.exp(-jnp.exp(w_log))
    k = jnp.dot(xk, w_k).astype(f32)
    v = jnp.dot(xv, w_v).astype(f32)
    a = jax.nn.sigmoid((a0 + jnp.dot(jnp.dot(xa, a1), a2)).astype(f32))
    g = jnp.dot(jax.nn.sigmoid(jnp.dot(xg, g1)), g2).astype(f32)
    kk = l2_normalize((k * k_k.astype(f32)).reshape(B, L, RW_HEADS, RW_HEAD))
    k = (k * (1.0 + (a - 1.0) * k_a.astype(f32))).reshape(B, L, RW_HEADS, RW_HEAD)
    r4 = r.reshape(B, L, RW_HEADS, RW_HEAD)
    v4 = v.reshape(B, L, RW_HEADS, RW_HEAD)
    a4 = a.reshape(B, L, RW_HEADS, RW_HEAD)
    y, s_new = wkv7_scan(r4, decay.reshape(B, L, RW_HEADS, RW_HEAD), k, v4, -kk, kk * a4,
                         wkv_state.astype(f32))
    mean = jnp.mean(y, axis=-1, keepdims=True)
    var = jnp.mean(jnp.square(y - mean), axis=-1, keepdims=True)
    y = ((y - mean) * lax.rsqrt(var + RW_GN_EPS)).reshape(B, L, D) * ln_w.astype(f32) + ln_b.astype(f32)
    y = y + (jnp.sum(r4 * k * r_k.astype(f32), axis=-1, keepdims=True) * v4).reshape(B, L, D)
    out = jnp.dot((y * g).astype(h.dtype), w_o)
    return out, h[:, -1], s_new.astype(wkv_state.dtype)


def moe_swiglu(h, router_w, router_b, w_gate, w_up, w_down):
    T, D = h.shape
    logits = jnp.dot(h, router_w).astype(jnp.float32) + router_b.astype(jnp.float32)
    top_logit, top_e = lax.top_k(logits, TOP_K)
    gates = jax.nn.softmax(top_logit, axis=-1)
    n_assign = T * TOP_K
    flat_e = top_e.reshape(-1).astype(jnp.int32)
    order = jnp.argsort(flat_e)
    sorted_e = flat_e[order]
    counts = jnp.bincount(flat_e, length=N_EXPERTS).astype(jnp.int32)
    padded = (counts + MOE_BLOCK - 1) // MOE_BLOCK * MOE_BLOCK
    pad_end = jnp.cumsum(padded)
    pad_start = pad_end - padded
    start = jnp.cumsum(counts) - counts
    dest_sorted = pad_start[sorted_e] + jnp.arange(n_assign, dtype=jnp.int32) - start[sorted_e]
    dest = jnp.zeros((n_assign,), jnp.int32).at[order].set(dest_sorted)
    n_blocks = -(-n_assign // MOE_BLOCK) + N_EXPERTS
    n_rows = n_blocks * MOE_BLOCK
    row_tok = jnp.full((n_rows,), T, jnp.int32).at[dest].set(jnp.arange(n_assign, dtype=jnp.int32) // TOP_K)
    block_e = jnp.minimum(jnp.searchsorted(pad_end, jnp.arange(n_blocks, dtype=jnp.int32) * MOE_BLOCK,
                                           side='right'), N_EXPERTS - 1)
    xb = jnp.concatenate([h, jnp.zeros((1, D), h.dtype)], axis=0)[row_tok].reshape(n_blocks, MOE_BLOCK, D)

    def expert_block(args):
        x_blk, e = args
        return swiglu(x_blk, w_gate[e], w_up[e], w_down[e])

    yb = lax.map(expert_block, (xb, block_e)).reshape(n_rows, D)
    y = yb[dest].reshape(T, TOP_K, D)
    return jnp.einsum('tkd,tk->td', y, gates.astype(y.dtype))


def layer_even(x, c, pos, past_k, past_v, conv_state, gdn_state,
               ada_w, ada_b, norm_mix, w_in, gdn_conv_w, gdn_a_log, gdn_dt_bias, gdn_norm_w,
               da_lq1, da_lk1, da_lq2, da_lk2, da_subln_w, w_out, norm_ffn, ffn_w_gate, ffn_w_up, ffn_w_down):
    B, L, _ = x.shape
    sh1, sc1, gt1, sh2, sc2, gt2 = ada_modulation(c, ada_w, ada_b)
    h = modulate(x, norm_mix, sh1, sc1)
    proj = jnp.dot(h, w_in)
    g_qkv, g_z, g_b, g_a, d_q, d_k, d_v = jnp.split(proj, IN0_SPLITS, axis=-1)
    o_gdn, new_conv, new_gdn = gated_deltanet(g_qkv, g_z, g_b, g_a, conv_state, gdn_state,
                                              gdn_conv_w, gdn_a_log, gdn_dt_bias, gdn_norm_w)
    q = partial_rotary(d_q.reshape(B, L, DA_HEADS, 2, DA_DK), pos)
    k = partial_rotary(d_k.reshape(B, L, DA_HEADS, 2, DA_DK), pos)
    v = d_v.reshape(B, L, DA_HEADS, DA_DV)
    o_da = diff_attention(q, k, v, pos, past_k, past_v, da_lq1, da_lk1, da_lq2, da_lk2, da_subln_w)
    mix = jnp.dot(jnp.concatenate([o_gdn, o_da.reshape(B, L, DA_VW).astype(o_gdn.dtype)], axis=-1), w_out)
    x = x + gt1 * mix
    x = x + gt2 * swiglu(modulate(x, norm_ffn, sh2, sc2), ffn_w_gate, ffn_w_up, ffn_w_down)
    return x, k.reshape(B, L, DA_HEADS, 2 * DA_DK), v, new_conv, new_gdn


def layer_odd(x, c, shift_state, wkv_state, ada_w, ada_b, norm_mix,
              rw_mu, rw_w0, rw_w1, rw_w2, rw_a0, rw_a1, rw_a2, rw_g1, rw_g2, rw_k_k, rw_k_a, rw_r_k,
              rw_wr, rw_wk, rw_wv, rw_wo, rw_ln_w, rw_ln_b,
              norm_ffn, router_w, router_b, moe_w_gate, moe_w_up, moe_w_down):
    B, L, D = x.shape
    sh1, sc1, gt1, sh2, sc2, gt2 = ada_modulation(c, ada_w, ada_b)
    h = modulate(x, norm_mix, sh1, sc1)
    mix, new_shift, new_wkv = rwkv7_time_mix(h, shift_state, wkv_state, rw_mu, rw_w0, rw_w1, rw_w2,
                                             rw_a0, rw_a1, rw_a2, rw_g1, rw_g2, rw_k_k, rw_k_a, rw_r_k,
                                             rw_wr, rw_wk, rw_wv, rw_wo, rw_ln_w, rw_ln_b)
    x = x + gt1 * mix
    h = modulate(x, norm_ffn, sh2, sc2).reshape(B * L, D)
    x = x + gt2 * moe_swiglu(h, router_w, router_b, moe_w_gate, moe_w_up, moe_w_down).reshape(B, L, D)
    return x, new_shift, new_wkv


def setup_inputs(seed: int = 0) -> dict:
    key = jax.random.key(seed)
    ks = iter(jax.random.split(key, 96))
    f32 = jnp.float32

    def nrm(shape, scale=1.0):
        return jax.random.normal(next(ks), shape, f32) * scale

    def unif(shape, lo, hi):
        return jax.random.uniform(next(ks), shape, f32, lo, hi)

    def gain(n):
        return 1.0 + nrm((n,), 0.05)

    n_pages = PAST_LEN // PAGE_SIZE
    n_used = DEC_BATCH * n_pages
    n_pool = n_used + n_used // 4
    page_table = jax.random.permutation(next(ks), n_pool)[:n_used].reshape(DEC_BATCH, n_pages).astype(jnp.int32)
    dt = jnp.exp(unif((GDN_HEADS,), math.log(1e-3), math.log(1e-1)))
    dt_bias = dt + jnp.log(-jnp.expm1(-dt))
    rd = D_MODEL ** -0.5
    return {
        'x_prompt': nrm((BATCH, SEQ, D_MODEL)),
        'x_sample': nrm((DEC_BATCH, DEC_SEQ, D_MODEL)),
        'cache_k': nrm((n_pool, PAGE_SIZE, DA_HEADS, 2 * DA_DK)),
        'cache_v': nrm((n_pool, PAGE_SIZE, DA_HEADS, DA_DV)),
        'state_gdn_conv': nrm((DEC_BATCH, GDN_CONV - 1, GDN_QKV)),
        'state_gdn': nrm((DEC_BATCH, GDN_HEADS, GDN_DK, GDN_DV), GDN_DK ** -0.5),
        'state_rwkv_shift': nrm((DEC_BATCH, D_MODEL)),
        'state_rwkv': nrm((DEC_BATCH, RW_HEADS, RW_HEAD, RW_HEAD), 0.5),
        'page_table': page_table,
        'c_prompt': nrm((BATCH, D_MODEL)),
        'c_sample': nrm((DEC_BATCH, D_MODEL)),
        'ada_w0': nrm((D_MODEL, 6 * D_MODEL), 0.5 * rd),
        'ada_b0': nrm((6 * D_MODEL,), 0.02),
        'norm_mix0': gain(D_MODEL),
        'w_in0': nrm((D_MODEL, IN0_WIDTH), rd),
        'gdn_conv_w': nrm((GDN_CONV, GDN_QKV), GDN_CONV ** -0.5),
        'gdn_a_log': jnp.log(unif((GDN_HEADS,), 1.0, 16.0)),
        'gdn_dt_bias': dt_bias,
        'gdn_norm_w': gain(GDN_DV),
        'da_lq1': nrm((DA_DK,), 0.1),
        'da_lk1': nrm((DA_DK,), 0.1),
        'da_lq2': nrm((DA_DK,), 0.1),
        'da_lk2': nrm((DA_DK,), 0.1),
        'da_subln_w': gain(DA_DV),
        'w_out0': nrm((MIX0_WIDTH, D_MODEL), MIX0_WIDTH ** -0.5),
        'norm_ffn0': gain(D_MODEL),
        'ffn_w_gate': nrm((D_MODEL, D_FF), rd),
        'ffn_w_up': nrm((D_MODEL, D_FF), rd),
        'ffn_w_down': nrm((D_FF, D_MODEL), D_FF ** -0.5),
        'ada_w1': nrm((D_MODEL, 6 * D_MODEL), 0.5 * rd),
        'ada_b1': nrm((6 * D_MODEL,), 0.02),
        'norm_mix1': gain(D_MODEL),
        'rw_mu': unif((6, D_MODEL), 0.0, 1.0),
        'rw_w0': unif((D_MODEL,), -5.0, 0.0),
        'rw_w1': nrm((D_MODEL, RW_DECAY_LORA), rd),
        'rw_w2': nrm((RW_DECAY_LORA, D_MODEL), 0.5 * RW_DECAY_LORA ** -0.5),
        'rw_a0': nrm((D_MODEL,), 0.1),
        'rw_a1': nrm((D_MODEL, RW_AAA_LORA), rd),
        'rw_a2': nrm((RW_AAA_LORA, D_MODEL), 0.5 * RW_AAA_LORA ** -0.5),
        'rw_g1': nrm((D_MODEL, RW_GATE_LORA), rd),
        'rw_g2': nrm((RW_GATE_LORA, D_MODEL), RW_GATE_LORA ** -0.5),
        'rw_k_k': 0.85 + nrm((D_MODEL,), 0.05),
        'rw_k_a': 1.0 + nrm((D_MODEL,), 0.05),
        'rw_r_k': nrm((RW_HEADS, RW_HEAD), 0.1),
        'rw_wr': nrm((D_MODEL, D_MODEL), rd),
        'rw_wk': nrm((D_MODEL, D_MODEL), rd),
        'rw_wv': nrm((D_MODEL, D_MODEL), rd),
        'rw_wo': nrm((D_MODEL, D_MODEL), rd),
        'rw_ln_w': gain(D_MODEL),
        'rw_ln_b': nrm((D_MODEL,), 0.02),
        'norm_ffn1': gain(D_MODEL),
        'moe_router_w': nrm((D_MODEL, N_EXPERTS), rd),
        'moe_router_b': nrm((N_EXPERTS,), 0.01),
        'moe_w_gate': nrm((N_EXPERTS, D_MODEL, D_FF), rd),
        'moe_w_up': nrm((N_EXPERTS, D_MODEL, D_FF), rd),
        'moe_w_down': nrm((N_EXPERTS, D_FF, D_MODEL), D_FF ** -0.5),
        'norm_final': gain(D_MODEL),
    }


def reference(x_prompt, x_sample, cache_k, cache_v, state_gdn_conv, state_gdn, state_rwkv_shift, state_rwkv,
              page_table, c_prompt, c_sample,
              ada_w0, ada_b0, norm_mix0, w_in0, gdn_conv_w, gdn_a_log, gdn_dt_bias, gdn_norm_w,
              da_lq1, da_lk1, da_lq2, da_lk2, da_subln_w, w_out0, norm_ffn0, ffn_w_gate, ffn_w_up, ffn_w_down,
              ada_w1, ada_b1, norm_mix1, rw_mu, rw_w0, rw_w1, rw_w2, rw_a0, rw_a1, rw_a2, rw_g1, rw_g2,
              rw_k_k, rw_k_a, rw_r_k, rw_wr, rw_wk, rw_wv, rw_wo, rw_ln_w, rw_ln_b,
              norm_ffn1, moe_router_w, moe_router_b, moe_w_gate, moe_w_up, moe_w_down, norm_final):
    bp, lp, d = x_prompt.shape
    bs, ls, _ = x_sample.shape
    n_pages = page_table.shape[1]
    past_len = n_pages * PAGE_SIZE
    pos_p = jnp.arange(lp, dtype=jnp.int32)
    pos_s = past_len + jnp.arange(ls, dtype=jnp.int32)
    past_k = cache_k[page_table].reshape(bs, past_len, DA_HEADS, 2, DA_DK)
    past_v = cache_v[page_table].reshape(bs, past_len, DA_HEADS, DA_DV)
    even_w = (ada_w0, ada_b0, norm_mix0, w_in0, gdn_conv_w, gdn_a_log, gdn_dt_bias, gdn_norm_w,
              da_lq1, da_lk1, da_lq2, da_lk2, da_subln_w, w_out0, norm_ffn0, ffn_w_gate, ffn_w_up, ffn_w_down)
    odd_w = (ada_w1, ada_b1, norm_mix1, rw_mu, rw_w0, rw_w1, rw_w2, rw_a0, rw_a1, rw_a2, rw_g1, rw_g2,
             rw_k_k, rw_k_a, rw_r_k, rw_wr, rw_wk, rw_wv, rw_wo, rw_ln_w, rw_ln_b,
             norm_ffn1, moe_router_w, moe_router_b, moe_w_gate, moe_w_up, moe_w_down)
    xp, xs = x_prompt, x_sample
    for layer in range(DEPTH):
        if layer % 2 == 0:
            xp, k_p, v_p, conv_p, gdn_p = layer_even(
                xp, c_prompt, pos_p, None, None,
                jnp.zeros((bp, GDN_CONV - 1, GDN_QKV), xp.dtype),
                jnp.zeros((bp, GDN_HEADS, GDN_DK, GDN_DV), xp.dtype), *even_w)
            xs, k_s, v_s, conv_s, gdn_s = layer_even(
                xs, c_sample, pos_s, past_k, past_v, state_gdn_conv, state_gdn, *even_w)
        else:
            xp, shift_p, rw_p = layer_odd(
                xp, c_prompt, jnp.zeros((bp, d), xp.dtype),
                jnp.zeros((bp, RW_HEADS, RW_HEAD, RW_HEAD), xp.dtype), *odd_w)
            xs, shift_s, rw_s = layer_odd(xs, c_sample, state_rwkv_shift, state_rwkv, *odd_w)
    y_prompt = rms_norm(xp, norm_final)
    y_sample = rms_norm(xs, norm_final)
    return (y_prompt, y_sample, k_p, v_p, k_s, v_s, conv_p, conv_s, gdn_p, gdn_s, shift_p, shift_s, rw_p, rw_s)
```

```python
import functools
import math

import jax
import jax.numpy as jnp
from jax import lax
from jax.experimental import pallas as pl
from jax.experimental.pallas import tpu as pltpu

F32 = jnp.float32
BF16 = jnp.bfloat16

D_MODEL = 2048
NORM_EPS = 1e-6
GDN_HEADS = 8
GDN_DK = 128
GDN_CONV = 4
GDN_CHUNK = 64
DA_HEADS = 8
DA_DK = 64
DA_DV = 128
ROPE_DIM = 16
ROPE_THETA = 500000.0
DA_LAMBDA_INIT = 0.8 - 0.6 * math.exp(-0.3 * 0)
RW_HEAD = 64
RW_CHUNK = 64
RW_GN_EPS = 1e-5 * RW_HEAD
N_EXPERTS = 8
TOP_K = 2
PAGE_SIZE = 128
LANES = 128
SUBLANES = 8
VMEM_LIMIT = 56 * 1024 * 1024
MOE_TILE = 512


def _cparams(sem):
    return pltpu.CompilerParams(dimension_semantics=sem, vmem_limit_bytes=VMEM_LIMIT)


def _tile(n, pref):
    if n <= pref:
        return n
    t = pref
    while t >= SUBLANES:
        if n % t == 0 and t % SUBLANES == 0:
            return t
        t -= SUBLANES
    return n


def _dot(a, b):
    return jnp.dot(a.astype(BF16), b.astype(BF16), preferred_element_type=F32)


def _dot_nt(a, b):
    return lax.dot_general(a.astype(BF16), b.astype(BF16), (((1,), (1,)), ((), ())),
                           preferred_element_type=F32)


def _dot_tn(a, b):
    return lax.dot_general(a.astype(BF16), b.astype(BF16), (((0,), (0,)), ((), ())),
                           preferred_element_type=F32)


def _sigmoid(x):
    return 1.0 / (1.0 + jnp.exp(-x))


def _silu(x):
    return x * _sigmoid(x)


def _softplus(x):
    return jnp.maximum(x, 0.0) + jnp.log(1.0 + jnp.exp(-jnp.abs(x)))


def _rms(x, eps=NORM_EPS):
    return x * lax.rsqrt(jnp.mean(x * x, axis=-1, keepdims=True) + eps)


def _unit_lower_inverse(a, block):
    n = a.shape[0]
    row = lax.broadcasted_iota(jnp.int32, (n, n), 0)
    col = lax.broadcasted_iota(jnp.int32, (n, n), 1)
    inv = (row == col).astype(F32) - jnp.where((row // 2) == (col // 2), a, 0.0)
    s = 4
    while s <= block:
        a_s = jnp.where(((row // s) == (col // s)) & ((row // (s // 2)) != (col // (s // 2))), a, 0.0)
        inv = inv - _dot(_dot(inv, a_s), inv)
        s *= 2
    return inv


class _Mod:
    def __init__(self, m, seq_len, tm):
        self.width = m.shape[1]
        if seq_len % tm == 0:
            self.per_batch = True
            self.tiles_per_batch = seq_len // tm
            self.arr = m.reshape(m.shape[0], 1, self.width)
        else:
            assert tm % seq_len == 0
            self.per_batch = False
            self.arr = jnp.repeat(m, seq_len, axis=0)
        self.tm = tm

    def spec(self, tn=None, col=None):
        tn = self.width if tn is None else tn
        col = (lambda i, j: 0) if col is None else col
        if self.per_batch:
            tpb = self.tiles_per_batch
            return pl.BlockSpec((None, 1, tn), lambda i, j: (i // tpb, 0, col(i, j)))
        return pl.BlockSpec((self.tm, tn), lambda i, j: (i, col(i, j)))

    def spec1(self):
        if self.per_batch:
            tpb = self.tiles_per_batch
            return pl.BlockSpec((None, 1, self.width), lambda i: (i // tpb, 0, 0))
        return pl.BlockSpec((self.tm, self.width), lambda i: (i, 0))


def _ada_kernel(c_ref, w_ref, b_ref, o_ref):
    a = _silu(c_ref[...])
    o_ref[...] = _dot(a, w_ref[...]) + b_ref[...]


def _ada_mod(c, w, b):
    r, d = c.shape
    n = w.shape[1]
    tn = _tile(n, 1024)
    return pl.pallas_call(
        _ada_kernel,
        grid=(n // tn,),
        in_specs=[pl.BlockSpec((r, d), lambda j: (0, 0)),
                  pl.BlockSpec((d, tn), lambda j: (0, j)),
                  pl.BlockSpec((1, tn), lambda j: (0, j))],
        out_specs=pl.BlockSpec((r, tn), lambda j: (0, j)),
        out_shape=jax.ShapeDtypeStruct((r, n), F32),
        compiler_params=_cparams(("parallel",)),
        name="ada_mod",
    )(c, w, b.reshape(1, n))


def _rope_tile(acc, cos, sin_lo, sin_hi):
    fwd = pltpu.roll(acc, LANES - ROPE_DIM // 2, 1)
    bwd = pltpu.roll(acc, ROPE_DIM // 2, 1)
    return acc * cos + fwd * sin_lo + bwd * sin_hi


def _norm_proj_kernel(*refs, n_w, rope, has_extra, tn):
    x_ref, sh_ref, sc_ref, g_ref = refs[:4]
    pos = 4
    w_refs = refs[pos:pos + n_w]
    pos += n_w
    if has_extra:
        we_ref = refs[pos]
        pos += 1
    if any(rope):
        cos_ref, slo_ref, shi_ref = refs[pos:pos + 3]
        pos += 3
    o_refs = refs[pos:pos + n_w]
    pos += n_w
    if has_extra:
        oe_ref = refs[pos]
        pos += 1
    h_ref = refs[pos]
    j = pl.program_id(1)

    @pl.when(j == 0)
    def _():
        h = _rms(x_ref[...]) * g_ref[...] * (1.0 + sc_ref[...]) + sh_ref[...]
        h_ref[...] = h.astype(BF16)
        if has_extra:
            oe_ref[...] = jnp.dot(h_ref[...], we_ref[...], preferred_element_type=F32)

    h = h_ref[...]
    for k in range(n_w):
        acc = jnp.dot(h, w_refs[k][...], preferred_element_type=F32)
        if rope[k]:
            cos, slo, shi = cos_ref[...], slo_ref[...], shi_ref[...]
            for c in range(tn // LANES):
                sl = slice(c * LANES, (c + 1) * LANES)
                o_refs[k][:, sl] = _rope_tile(acc[:, sl], cos, slo, shi)
        else:
            o_refs[k][...] = acc


def _norm_proj(x, sh, sc, g, ws, rope, extra_w, rope_tabs, seq_len, tm):
    m, d = x.shape
    n = ws[0].shape[1]
    tn = _tile(n, 256)
    n_w = len(ws)
    shm, scm = _Mod(sh, seq_len, tm), _Mod(sc, seq_len, tm)
    in_specs = [pl.BlockSpec((tm, d), lambda i, j: (i, 0)), shm.spec(), scm.spec(),
                pl.BlockSpec((1, d), lambda i, j: (0, 0))]
    args = [x, shm.arr, scm.arr, g.reshape(1, d)]
    for w in ws:
        in_specs.append(pl.BlockSpec((d, tn), lambda i, j: (0, j)))
        args.append(w)
    has_extra = extra_w is not None
    if has_extra:
        in_specs.append(pl.BlockSpec((d, LANES), lambda i, j: (0, 0)))
        args.append(extra_w)
    if any(rope):
        if seq_len % tm == 0:
            tpb = seq_len // tm
            tab_spec = pl.BlockSpec((tm, LANES), lambda i, j: (i % tpb, 0))
        else:
            tab_spec = pl.BlockSpec((tm, LANES), lambda i, j: (i, 0))
        for t in rope_tabs:
            in_specs.append(tab_spec)
            args.append(t)
    out_specs = [pl.BlockSpec((tm, tn), lambda i, j: (i, j)) for _ in ws]
    out_shape = [jax.ShapeDtypeStruct((m, n), F32) for _ in ws]
    if has_extra:
        out_specs.append(pl.BlockSpec((tm, LANES), lambda i, j: (i, 0)))
        out_shape.append(jax.ShapeDtypeStruct((m, LANES), F32))
    return pl.pallas_call(
        functools.partial(_norm_proj_kernel, n_w=n_w, rope=tuple(rope), has_extra=has_extra, tn=tn),
        grid=(m // tm, n // tn),
        in_specs=in_specs,
        out_specs=out_specs,
        out_shape=out_shape,
        scratch_shapes=[pltpu.VMEM((tm, d), BF16)],
        compiler_params=_cparams(("parallel", "arbitrary")),
        name="norm_proj",
    )(*args)


def _rope_tables(pos):
    half = ROPE_DIM // 2
    inv_freq = ROPE_THETA ** (-jnp.arange(half, dtype=F32) * (2.0 / ROPE_DIM))
    ang = pos.astype(F32)[:, None] * inv_freq
    cos, sin = jnp.cos(ang), jnp.sin(ang)
    n = pos.shape[0]
    ones = jnp.ones((n, DA_DK - ROPE_DIM), F32)
    zeros = jnp.zeros((n, DA_DK - ROPE_DIM), F32)
    zh = jnp.zeros((n, half), F32)
    cos_m = jnp.concatenate([cos, cos, ones], axis=1)
    slo_m = jnp.concatenate([-sin, zh, zeros], axis=1)
    shi_m = jnp.concatenate([zh, sin, zeros], axis=1)
    return tuple(jnp.concatenate([t, t], axis=1) for t in (cos_m, slo_m, shi_m))


def _gdn_kernel(q_ref, k_ref, v_ref, z_ref, ba_ref, csq_ref, csk_ref, csv_ref, cwq_ref, cwk_ref, cwv_ref,
                s0_ref, alog_ref, dtb_ref, nw_ref, o_ref, sout_ref, s_scr, hist_scr, ext_scr, *, tt, chunk):
    hp = pl.program_id(1)
    t = pl.program_id(2)
    nt = pl.num_programs(2)
    n_hist = SUBLANES

    @pl.when(t == 0)
    def _():
        s_scr[...] = s0_ref[...]
        hist_scr[0] = csq_ref[...]
        hist_scr[1] = csk_ref[...]
        hist_scr[2] = csv_ref[...]

    conv = []
    for s, (raw_ref, cw_ref) in enumerate(((q_ref, cwq_ref), (k_ref, cwk_ref), (v_ref, cwv_ref))):
        raw = raw_ref[...]
        ext_scr[s, 0:n_hist, :] = hist_scr[s]
        ext_scr[s, n_hist:n_hist + tt, :] = raw
        cw = cw_ref[...]
        y = raw * cw[GDN_CONV - 1:GDN_CONV, :]
        for dly in range(1, GDN_CONV):
            y = y + ext_scr[s, n_hist - dly:n_hist - dly + tt, :] * cw[GDN_CONV - 1 - dly:GDN_CONV - dly, :]
        hist_scr[s] = raw[tt - n_hist:tt, :]
        conv.append(_silu(y))
    q_all, k_all, v_all = conv
    z_all = z_ref[...]

    ba = ba_ref[...]
    lane = lax.broadcasted_iota(jnp.int32, ba.shape, 1)
    gates = jnp.where(lane < GDN_HEADS, _sigmoid(ba), -jnp.exp(alog_ref[...]) * _softplus(ba + dtb_ref[...]))
    beta_cols, g_cols = [], []
    for hl in range(2):
        hh = 2 * hp + hl
        beta_cols.append(jnp.sum(jnp.where(lane == hh, gates, 0.0), axis=1, keepdims=True))
        g_cols.append(jnp.sum(jnp.where(lane == GDN_HEADS + hh, gates, 0.0), axis=1, keepdims=True))

    c = chunk
    n2 = 2 * c
    row = lax.broadcasted_iota(jnp.int32, (n2, n2), 0)
    col = lax.broadcasted_iota(jnp.int32, (n2, n2), 1)
    same = (row // c) == (col // c)
    low_incl = same & (col <= row)
    low_strict = same & (col < row)
    up_incl = same & (row <= col)
    eye = row == col
    nw = nw_ref[...]

    def stack(x_all, r0, nrows):
        parts = []
        for hl in range(2):
            p = x_all[r0:r0 + nrows, hl * LANES:(hl + 1) * LANES]
            if nrows < c:
                p = jnp.concatenate([p, jnp.zeros((c - nrows, p.shape[1]), F32)], axis=0)
            parts.append(p)
        return jnp.concatenate(parts, axis=0)

    def stack_col(cols, r0, nrows):
        parts = []
        for hl in range(2):
            p = cols[hl][r0:r0 + nrows, :]
            if nrows < c:
                p = jnp.concatenate([p, jnp.zeros((c - nrows, 1), F32)], axis=0)
            parts.append(p)
        return jnp.concatenate(parts, axis=0)

    nrows = min(tt, c)
    for ci in range(max(tt // c, 1)):
        r0 = ci * c
        q = stack(q_all, r0, nrows)
        k = stack(k_all, r0, nrows)
        v = stack(v_all, r0, nrows)
        z = stack(z_all, r0, nrows)
        beta = stack_col(beta_cols, r0, nrows)
        g = stack_col(g_cols, r0, nrows)
        q = q * lax.rsqrt(jnp.sum(q * q, axis=-1, keepdims=True) + 1e-12) * (GDN_DK ** -0.5)
        k = k * lax.rsqrt(jnp.sum(k * k, axis=-1, keepdims=True) + 1e-12)
        g_row = jnp.sum(jnp.where(eye, g, 0.0), axis=0, keepdims=True)
        gc_col = jnp.sum(jnp.where(low_incl, g_row, 0.0), axis=1, keepdims=True)
        gc_row = jnp.sum(jnp.where(up_incl, g, 0.0), axis=0, keepdims=True)
        decay = jnp.where(low_incl, jnp.exp(jnp.where(low_incl, gc_col - gc_row, 0.0)), 0.0)
        kb = k * beta
        a_mat = jnp.where(low_strict, _dot_nt(kb, k) * decay, 0.0)
        t_inv = _unit_lower_inverse(a_mat, c)
        egc = jnp.exp(gc_col)
        ks = jnp.concatenate([_dot(k[hl * c:(hl + 1) * c], s_scr[hl]) for hl in range(2)], axis=0)
        rhs = beta * (v - egc * ks)
        v_new = _dot(t_inv, rhs)
        qk = jnp.where(low_incl, _dot_nt(q, k) * decay, 0.0)
        qe = q * egc
        qs = jnp.concatenate([_dot(qe[hl * c:(hl + 1) * c], s_scr[hl]) for hl in range(2)], axis=0)
        o = qs + _dot(qk, v_new)
        for hl in range(2):
            g_last = gc_col[(hl + 1) * c - 1:(hl + 1) * c, :]
            kd = k[hl * c:(hl + 1) * c] * jnp.exp(g_last - gc_col[hl * c:(hl + 1) * c])
            s_scr[hl] = s_scr[hl] * jnp.exp(g_last) + _dot_tn(kd, v_new[hl * c:(hl + 1) * c])
        o = _rms(o) * nw * _silu(z)
        for hl in range(2):
            o_ref[r0:r0 + nrows, hl * LANES:(hl + 1) * LANES] = o[hl * c:hl * c + nrows].astype(o_ref.dtype)

    @pl.when(t == nt - 1)
    def _():
        sout_ref[...] = s_scr[...]


def _gated_deltanet(qraw, kraw, vraw, z, ba, conv_state, s0, conv_w, a_log, dt_bias, norm_w, n_batch, seq_len):
    m = qraw.shape[0]
    tt = _tile(seq_len, 256)
    nt = seq_len // tt
    chunk = GDN_CHUNK
    n_hist = SUBLANES
    w2 = 2 * LANES
    qkv_w = 3 * GDN_HEADS * GDN_DK
    cs = jnp.concatenate([jnp.zeros((n_batch, n_hist - (GDN_CONV - 1), qkv_w), F32), conv_state], axis=1)
    cw = jnp.concatenate([conv_w, jnp.zeros((n_hist - GDN_CONV, qkv_w), F32)], axis=0)
    alog = jnp.zeros((1, LANES), F32).at[0, GDN_HEADS:2 * GDN_HEADS].set(a_log)
    dtb = jnp.zeros((1, LANES), F32).at[0, GDN_HEADS:2 * GDN_HEADS].set(dt_bias)
    nblk = GDN_HEADS // 2
    row_spec = pl.BlockSpec((tt, w2), lambda b, hp, t: (b * nt + t, hp))
    cs_specs = [pl.BlockSpec((None, n_hist, w2), functools.partial(lambda b, hp, t, s: (b, 0, s * nblk + hp), s=s))
                for s in range(3)]
    cw_specs = [pl.BlockSpec((n_hist, w2), functools.partial(lambda b, hp, t, s: (0, s * nblk + hp), s=s))
                for s in range(3)]
    st_spec = pl.BlockSpec((None, 2, GDN_DK, GDN_DK), lambda b, hp, t: (b, hp, 0, 0))
    vec_spec = pl.BlockSpec((1, LANES), lambda b, hp, t: (0, 0))
    return pl.pallas_call(
        functools.partial(_gdn_kernel, tt=tt, chunk=chunk),
        grid=(n_batch, nblk, nt),
        in_specs=[row_spec, row_spec, row_spec, row_spec,
                  pl.BlockSpec((tt, LANES), lambda b, hp, t: (b * nt + t, 0)),
                  *cs_specs, *cw_specs, st_spec, vec_spec, vec_spec, vec_spec],
        out_specs=[row_spec, st_spec],
        out_shape=[jax.ShapeDtypeStruct((m, GDN_HEADS * GDN_DK), BF16),
                   jax.ShapeDtypeStruct(s0.shape, F32)],
        scratch_shapes=[pltpu.VMEM((2, GDN_DK, GDN_DK), F32),
                        pltpu.VMEM((3, n_hist, w2), F32),
                        pltpu.VMEM((3, n_hist + tt, w2), F32)],
        compiler_params=_cparams(("parallel", "parallel", "arbitrary")),
        name="gated_deltanet",
    )(qraw, kraw, vraw, z, ba, cs, cs, cs, cw, cw, cw, s0, alog, dtb, norm_w.reshape(1, LANES))


def _lambda(lq1_ref, lk1_ref, lq2_ref, lk2_ref):
    s1 = jnp.sum(lq1_ref[...] * lk1_ref[...], axis=-1, keepdims=True)
    s2 = jnp.sum(lq2_ref[...] * lk2_ref[...], axis=-1, keepdims=True)
    return jnp.exp(s1) - jnp.exp(s2) + DA_LAMBDA_INIT


def _flash_kernel(q_ref, k_ref, v_ref, lq1_ref, lk1_ref, lq2_ref, lk2_ref, sw_ref, o_ref,
                  qs_scr, m_scr, l_scr, acc_scr, *, tq, tk):
    qi = pl.program_id(2)
    ki = pl.program_id(3)

    @pl.when(ki == 0)
    def _():
        q = q_ref[...] * (DA_DK ** -0.5)
        lane = lax.broadcasted_iota(jnp.int32, q.shape, 1)
        qs_scr[0] = jnp.where(lane < DA_DK, q, 0.0).astype(BF16)
        qs_scr[1] = jnp.where(lane >= DA_DK, q, 0.0).astype(BF16)
        m_scr[...] = jnp.full(m_scr.shape, -jnp.inf, F32)
        l_scr[...] = jnp.zeros(l_scr.shape, F32)
        acc_scr[...] = jnp.zeros(acc_scr.shape, F32)

    @pl.when(ki <= qi)
    def _():
        k = k_ref[...].astype(BF16)
        v = v_ref[...].astype(BF16)
        row = lax.broadcasted_iota(jnp.int32, (tq, tk), 0) + qi * tq
        col = lax.broadcasted_iota(jnp.int32, (tq, tk), 1) + ki * tk
        visible = col <= row
        for mp in range(2):
            s = lax.dot_general(qs_scr[mp], k, (((1,), (1,)), ((), ())), preferred_element_type=F32)
            s = jnp.where(visible, s, -jnp.inf)
            m_old = m_scr[mp]
            m_new = jnp.maximum(m_old, jnp.max(s, axis=-1, keepdims=True))
            alpha = jnp.exp(m_old - m_new)
            p = jnp.exp(s - m_new)
            l_scr[mp] = alpha * l_scr[mp] + jnp.sum(p, axis=-1, keepdims=True)
            acc_scr[mp] = alpha * acc_scr[mp] + jnp.dot(p.astype(BF16), v, preferred_element_type=F32)
            m_scr[mp] = m_new

    @pl.when(ki == qi)
    def _():
        lam = _lambda(lq1_ref, lk1_ref, lq2_ref, lk2_ref)
        o = acc_scr[0] / l_scr[0] - lam * (acc_scr[1] / l_scr[1])
        o = _rms(o) * sw_ref[...] * (1.0 - DA_LAMBDA_INIT)
        o_ref[...] = o.astype(o_ref.dtype)


def _diff_attention_prompt(q, k, v, lq1, lk1, lq2, lk2, subln_w, n_batch, seq_len):
    m = q.shape[0]
    tq = _tile(seq_len, 512)
    tk = tq
    nq = seq_len // tq
    vec = lambda a: a.reshape(1, -1)
    vspec = lambda n: pl.BlockSpec((1, n), lambda b, h, qi, ki: (0, 0))
    return pl.pallas_call(
        functools.partial(_flash_kernel, tq=tq, tk=tk),
        grid=(n_batch, DA_HEADS, nq, nq),
        in_specs=[pl.BlockSpec((tq, LANES), lambda b, h, qi, ki: (b * nq + qi, h)),
                  pl.BlockSpec((tk, LANES), lambda b, h, qi, ki: (b * nq + jnp.minimum(ki, qi), h)),
                  pl.BlockSpec((tk, LANES), lambda b, h, qi, ki: (b * nq + jnp.minimum(ki, qi), h)),
                  vspec(DA_DK), vspec(DA_DK), vspec(DA_DK), vspec(DA_DK), vspec(DA_DV)],
        out_specs=pl.BlockSpec((tq, LANES), lambda b, h, qi, ki: (b * nq + qi, h)),
        out_shape=jax.ShapeDtypeStruct((m, DA_HEADS * DA_DV), BF16),
        scratch_shapes=[pltpu.VMEM((2, tq, LANES), BF16), pltpu.VMEM((2, tq, 1), F32),
                        pltpu.VMEM((2, tq, 1), F32), pltpu.VMEM((2, tq, LANES), F32)],
        compiler_params=_cparams(("parallel", "parallel", "parallel", "arbitrary")),
        name="diff_attn_prompt",
    )(q, k, v, vec(lq1), vec(lk1), vec(lq2), vec(lk2), vec(subln_w))


def _paged_attn_kernel(pt_ref, q_ref, kn_ref, vn_ref, *refs, n_group, ls):
    k_refs = refs[:n_group]
    v_refs = refs[n_group:2 * n_group]
    lq1_ref, lk1_ref, lq2_ref, lk2_ref, sw_ref, o_ref, qs_scr, m_scr, l_scr, acc_scr = refs[2 * n_group:]
    p_idx = pl.program_id(1)
    n_steps = pl.num_programs(1)
    n_rows = 2 * ls

    @pl.when(p_idx == 0)
    def _():
        q = q_ref[...] * (DA_DK ** -0.5)
        q2 = jnp.concatenate([q, q], axis=0)
        r = lax.broadcasted_iota(jnp.int32, q2.shape, 0)
        lane = lax.broadcasted_iota(jnp.int32, q2.shape, 1)
        keep = (r // ls) == ((lane % LANES) // DA_DK)
        qs_scr[...] = jnp.where(keep, q2, 0.0).astype(BF16)
        m_scr[...] = jnp.full(m_scr.shape, -jnp.inf, F32)
        l_scr[...] = jnp.zeros(l_scr.shape, F32)
        acc_scr[...] = jnp.zeros(acc_scr.shape, F32)

    def update(h, s, vh):
        m_old = m_scr[h]
        m_new = jnp.maximum(m_old, jnp.max(s, axis=-1, keepdims=True))
        alpha = jnp.exp(m_old - m_new)
        p = jnp.exp(s - m_new)
        l_scr[h] = alpha * l_scr[h] + jnp.sum(p, axis=-1, keepdims=True)
        acc_scr[h] = alpha * acc_scr[h] + jnp.dot(p.astype(BF16), vh, preferred_element_type=F32)
        m_scr[h] = m_new

    for h in range(DA_HEADS):
        sl = slice(h * LANES, (h + 1) * LANES)
        qh = qs_scr[:, sl]
        kh = jnp.concatenate([kr[:, sl].astype(BF16) for kr in k_refs], axis=0)
        vh = jnp.concatenate([vr[:, sl].astype(BF16) for vr in v_refs], axis=0)
        s = lax.dot_general(qh, kh, (((1,), (1,)), ((), ())), preferred_element_type=F32)
        update(h, s, vh)

    @pl.when(p_idx == n_steps - 1)
    def _():
        lam = _lambda(lq1_ref, lk1_ref, lq2_ref, lk2_ref)
        pad = jnp.zeros((LANES - ls, DA_HEADS * LANES), F32)
        kn = jnp.concatenate([kn_ref[...], pad], axis=0).astype(BF16)
        vn = jnp.concatenate([vn_ref[...], pad], axis=0).astype(BF16)
        r = lax.broadcasted_iota(jnp.int32, (n_rows, LANES), 0)
        c = lax.broadcasted_iota(jnp.int32, (n_rows, LANES), 1)
        visible = c <= (r % ls)
        for h in range(DA_HEADS):
            sl = slice(h * LANES, (h + 1) * LANES)
            s = lax.dot_general(qs_scr[:, sl], kn[:, sl], (((1,), (1,)), ((), ())), preferred_element_type=F32)
            s = jnp.where(visible, s, -jnp.inf)
            update(h, s, vn[:, sl])
            acc = acc_scr[h] / l_scr[h]
            o = acc[0:ls] - lam * acc[ls:2 * ls]
            o = _rms(o) * sw_ref[...] * (1.0 - DA_LAMBDA_INIT)
            o_ref[:, sl] = o.astype(o_ref.dtype)


def _diff_attention_sample(q, k_new, v_new, cache_k, cache_v, page_table, lq1, lk1, lq2, lk2, subln_w, n_batch, ls):
    n_pool = cache_k.shape[0]
    width = DA_HEADS * LANES
    ck = cache_k.reshape(n_pool, PAGE_SIZE, width)
    cv = cache_v.reshape(n_pool, PAGE_SIZE, width)
    n_pages = page_table.shape[1]
    n_group = 4 if n_pages % 4 == 0 else (2 if n_pages % 2 == 0 else 1)
    n_steps = n_pages // n_group
    row_spec = pl.BlockSpec((ls, width), lambda b, p, pt: (b, 0))
    page_specs = [pl.BlockSpec((None, PAGE_SIZE, width),
                               functools.partial(lambda b, p, pt, g: (pt[b, p * n_group + g], 0, 0), g=g))
                  for g in range(n_group)]
    vec = lambda a: a.reshape(1, -1)
    vspec = lambda n: pl.BlockSpec((1, n), lambda b, p, pt: (0, 0))
    grid_spec = pltpu.PrefetchScalarGridSpec(
        num_scalar_prefetch=1,
        grid=(n_batch, n_steps),
        in_specs=[row_spec, row_spec, row_spec, *page_specs, *page_specs,
                  vspec(DA_DK), vspec(DA_DK), vspec(DA_DK), vspec(DA_DK), vspec(DA_DV)],
        out_specs=row_spec,
        scratch_shapes=[pltpu.VMEM((2 * ls, width), BF16), pltpu.VMEM((DA_HEADS, 2 * ls, 1), F32),
                        pltpu.VMEM((DA_HEADS, 2 * ls, 1), F32), pltpu.VMEM((DA_HEADS, 2 * ls, LANES), F32)],
    )
    return pl.pallas_call(
        functools.partial(_paged_attn_kernel, n_group=n_group, ls=ls),
        grid_spec=grid_spec,
        out_shape=jax.ShapeDtypeStruct((n_batch * ls, width), BF16),
        compiler_params=_cparams(("parallel", "arbitrary")),
        name="diff_attn_sample",
    )(page_table, q, k_new, v_new, *([ck] * n_group), *([cv] * n_group),
      vec(lq1), vec(lk1), vec(lq2), vec(lk2), vec(subln_w))


def _proj_res_kernel(*refs, n_parts):
    x_ref, gt_ref = refs[:2]
    a_refs = refs[2:2 + n_parts]
    w_refs = refs[2 + n_parts:2 + 2 * n_parts]
    o_ref = refs[2 + 2 * n_parts]
    acc = jnp.dot(a_refs[0][...], w_refs[0][...], preferred_element_type=F32)
    for p in range(1, n_parts):
        acc = acc + jnp.dot(a_refs[p][...], w_refs[p][...], preferred_element_type=F32)
    o_ref[...] = x_ref[...] + gt_ref[...] * acc


def _proj_residual(x, gt, parts, ws, seq_len, tm):
    m, n = x.shape
    tn = _tile(n, 512)
    gtm = _Mod(gt, seq_len, tm)
    in_specs = [pl.BlockSpec((tm, tn), lambda i, j: (i, j)), gtm.spec(tn, lambda i, j: j)]
    for a in parts:
        in_specs.append(pl.BlockSpec((tm, a.shape[1]), lambda i, j: (i, 0)))
    for w in ws:
        in_specs.append(pl.BlockSpec((w.shape[0], tn), lambda i, j: (0, j)))
    return pl.pallas_call(
        functools.partial(_proj_res_kernel, n_parts=len(parts)),
        grid=(m // tm, n // tn),
        in_specs=in_specs,
        out_specs=pl.BlockSpec((tm, tn), lambda i, j: (i, j)),
        out_shape=jax.ShapeDtypeStruct((m, n), F32),
        compiler_params=_cparams(("parallel", "arbitrary")),
        name="proj_residual",
    )(x, gtm.arr, *parts, *ws)


def _norm_ffn_kernel(x_ref, sh_ref, sc_ref, gt_ref, g_ref, wg_ref, wu_ref, wd_ref, o_ref, h_ref):
    f = pl.program_id(1)
    nf = pl.num_programs(1)

    @pl.when(f == 0)
    def _():
        h = _rms(x_ref[...]) * g_ref[...] * (1.0 + sc_ref[...]) + sh_ref[...]
        h_ref[...] = h.astype(BF16)

    h = h_ref[...]
    gate = jnp.dot(h, wg_ref[...], preferred_element_type=F32)
    up = jnp.dot(h, wu_ref[...], preferred_element_type=F32)
    act = (_silu(gate) * up).astype(BF16)
    part = jnp.dot(act, wd_ref[...], preferred_element_type=F32)

    @pl.when(f == 0)
    def _():
        o_ref[...] = part

    @pl.when(f > 0)
    def _():
        o_ref[...] += part

    @pl.when(f == nf - 1)
    def _():
        o_ref[...] = x_ref[...] + gt_ref[...] * o_ref[...]


def _norm_ffn(x, sh, sc, gt, g, wg, wu, wd, seq_len, tm):
    m, d = x.shape
    dff = wg.shape[1]
    tf = _tile(dff, 512)
    tm = _tile(tm, 512)
    shm, scm, gtm = _Mod(sh, seq_len, tm), _Mod(sc, seq_len, tm), _Mod(gt, seq_len, tm)
    return pl.pallas_call(
        _norm_ffn_kernel,
        grid=(m // tm, dff // tf),
        in_specs=[pl.BlockSpec((tm, d), lambda i, f: (i, 0)), shm.spec(), scm.spec(), gtm.spec(),
                  pl.BlockSpec((1, d), lambda i, f: (0, 0)),
                  pl.BlockSpec((d, tf), lambda i, f: (0, f)),
                  pl.BlockSpec((d, tf), lambda i, f: (0, f)),
                  pl.BlockSpec((tf, d), lambda i, f: (f, 0))],
        out_specs=pl.BlockSpec((tm, d), lambda i, f: (i, 0)),
        out_shape=jax.ShapeDtypeStruct((m, d), F32),
        scratch_shapes=[pltpu.VMEM((tm, d), BF16)],
        compiler_params=_cparams(("parallel", "arbitrary")),
        name="norm_ffn",
    )(x, shm.arr, scm.arr, gtm.arr, g.reshape(1, d), wg, wu, wd)


def _rwkv_proj_kernel(x_ref, xp_ref, sh_ref, sc_ref, shift_ref, g_ref, mu_ref, w1_ref, a1_ref, g1_ref,
                      wr_ref, wk_ref, wv_ref, w2_ref, a2_ref, g2_ref, w0_ref, a0_ref,
                      r_ref, k_ref, v_ref, lw_ref, a_ref, gg_ref,
                      mix_scr, sw_scr, sa_scr, sg_scr, *, tm, seq_len):
    i = pl.program_id(0)
    j = pl.program_id(1)

    @pl.when(j == 0)
    def _():
        g = g_ref[...]
        scale = 1.0 + sc_ref[...]
        shift = sh_ref[...]
        h = _rms(x_ref[...]) * g * scale + shift
        row = lax.broadcasted_iota(jnp.int32, h.shape, 0)
        rolled = pltpu.roll(h, 1, 0)
        if seq_len % tm == 0:
            sc_row = scale[0:1] if scale.shape[0] > 1 else scale
            sh_row = shift[0:1] if shift.shape[0] > 1 else shift
            hp = _rms(xp_ref[...]) * g * sc_row + sh_row
            first = jnp.where((i % (seq_len // tm)) == 0, shift_ref[...], hp[SUBLANES - 1:SUBLANES])
            prev = jnp.where(row == 0, first, rolled)
        else:
            prev = jnp.where((row % seq_len) == 0, shift_ref[...], rolled)
        xx = prev - h
        mu = mu_ref[...]
        for n in range(6):
            mix_scr[n] = (h + xx * mu[n:n + 1]).astype(BF16)
        sw_scr[...] = jnp.tanh(jnp.dot(mix_scr[1], w1_ref[...], preferred_element_type=F32)).astype(BF16)
        sa_scr[...] = jnp.dot(mix_scr[4], a1_ref[...], preferred_element_type=F32).astype(BF16)
        sg_scr[...] = _sigmoid(jnp.dot(mix_scr[5], g1_ref[...], preferred_element_type=F32)).astype(BF16)

    r_ref[...] = jnp.dot(mix_scr[0], wr_ref[...], preferred_element_type=F32)
    k_ref[...] = jnp.dot(mix_scr[2], wk_ref[...], preferred_element_type=F32)
    v_ref[...] = jnp.dot(mix_scr[3], wv_ref[...], preferred_element_type=F32)
    wl = w0_ref[...] + jnp.dot(sw_scr[...], w2_ref[...], preferred_element_type=F32)
    lw_ref[...] = -jnp.exp(-_softplus(-wl) - 0.5)
    a_ref[...] = _sigmoid(a0_ref[...] + jnp.dot(sa_scr[...], a2_ref[...], preferred_element_type=F32))
    gg_ref[...] = jnp.dot(sg_scr[...], g2_ref[...], preferred_element_type=F32)


def _rwkv_proj(x, sh, sc, shift_state, g, mu, w1, a1, g1, wr, wk, wv, w2, a2, g2, w0, a0, n_batch, seq_len, tm):
    m, d = x.shape
    tn = _tile(d, 256)
    shm, scm = _Mod(sh, seq_len, tm), _Mod(sc, seq_len, tm)
    stm = _Mod(shift_state, seq_len, tm)
    rows8 = tm // SUBLANES
    full = lambda a: pl.BlockSpec(a.shape, lambda i, j: (0,) * a.ndim)
    colw = lambda a: pl.BlockSpec((a.shape[0], tn), lambda i, j: (0, j))
    out_spec = pl.BlockSpec((tm, tn), lambda i, j: (i, j))
    g2d, w0r, a0r = g.reshape(1, d), w0.reshape(1, d), a0.reshape(1, d)
    return pl.pallas_call(
        functools.partial(_rwkv_proj_kernel, tm=tm, seq_len=seq_len),
        grid=(m // tm, d // tn),
        in_specs=[pl.BlockSpec((tm, d), lambda i, j: (i, 0)),
                  pl.BlockSpec((SUBLANES, d), lambda i, j: (jnp.maximum(i * rows8 - 1, 0), 0)),
                  shm.spec(), scm.spec(), stm.spec(), full(g2d), full(mu), full(w1), full(a1), full(g1),
                  colw(wr), colw(wk), colw(wv), colw(w2), colw(a2), colw(g2), colw(w0r), colw(a0r)],
        out_specs=[out_spec] * 6,
        out_shape=[jax.ShapeDtypeStruct((m, d), F32)] * 6,
        scratch_shapes=[pltpu.VMEM((6, tm, d), BF16), pltpu.VMEM((tm, w1.shape[1]), BF16),
                        pltpu.VMEM((tm, a1.shape[1]), BF16), pltpu.VMEM((tm, g1.shape[1]), BF16)],
        compiler_params=_cparams(("parallel", "arbitrary")),
        name="rwkv_proj",
    )(x, x, shm.arr, scm.arr, stm.arr, g2d, mu, w1, a1, g1, wr, wk, wv, w2, a2, g2, w0r, a0r)


def _norm_mod_kernel(x_ref, sh_ref, sc_ref, g_ref, o_ref):
    o_ref[...] = _rms(x_ref[...]) * g_ref[...] * (1.0 + sc_ref[...]) + sh_ref[...]


def _norm_mod(x, sh, sc, g):
    r, d = x.shape
    spec = pl.BlockSpec((r, d), lambda i: (0, 0))
    return pl.pallas_call(
        _norm_mod_kernel, grid=(1,),
        in_specs=[spec, spec, spec, pl.BlockSpec((1, d), lambda i: (0, 0))],
        out_specs=spec, out_shape=jax.ShapeDtypeStruct((r, d), F32),
        compiler_params=_cparams(("arbitrary",)), name="norm_mod",
    )(x, sh, sc, g.reshape(1, d))


def _wkv_kernel(r_ref, k_ref, v_ref, lw_ref, a_ref, g_ref, kk_ref, ka_ref, rk_ref, lnw_ref, lnb_ref, s0_ref,
                o_ref, sout_ref, z_scr, *, tt, chunk):
    t = pl.program_id(2)
    nt = pl.num_programs(2)

    @pl.when(t == 0)
    def _():
        z_scr[...] = s0_ref[...]

    c = chunk
    n2 = 2 * c
    lane1 = lax.broadcasted_iota(jnp.int32, (1, LANES), 1)
    head_a = lane1 < RW_HEAD
    rr = lax.broadcasted_iota(jnp.int32, (LANES, LANES), 0)
    cc = lax.broadcasted_iota(jnp.int32, (LANES, LANES), 1)
    seg_ones = ((rr // RW_HEAD) == (cc // RW_HEAD)).astype(BF16)
    row = lax.broadcasted_iota(jnp.int32, (n2, n2), 0)
    col = lax.broadcasted_iota(jnp.int32, (n2, n2), 1)
    same = (row // c) == (col // c)
    low_incl = same & (col <= row)
    low_strict = same & (col < row)
    tri = (lax.broadcasted_iota(jnp.int32, (c, c), 1) <= lax.broadcasted_iota(jnp.int32, (c, c), 0)).astype(BF16)
    kk_w, ka_w, rk_w, lnw, lnb = kk_ref[...], ka_ref[...], rk_ref[...], lnw_ref[...], lnb_ref[...]

    def seg_sum(x):
        return jnp.dot(x.astype(BF16), seg_ones, preferred_element_type=F32)

    def stack(x):
        return jnp.concatenate([jnp.where(head_a, x, 0.0), jnp.where(head_a, 0.0, x)], axis=0)

    nrows = min(tt, c)

    def load(ref, r0):
        x = ref[r0:r0 + nrows, :]
        if nrows < c:
            x = jnp.concatenate([x, jnp.zeros((c - nrows, LANES), F32)], axis=0)
        return x

    for ci in range(max(tt // c, 1)):
        r0 = ci * c
        r_, k_, v_, lw_, a_ = (load(ref, r0) for ref in (r_ref, k_ref, v_ref, lw_ref, a_ref))
        kk_raw = k_ * kk_w
        kk = kk_raw * lax.rsqrt(seg_sum(kk_raw * kk_raw) + 1e-12)
        kmod = k_ * (1.0 + (a_ - 1.0) * ka_w)
        pvec = -kk
        cvec = kk * a_
        lw_hi = lw_.astype(BF16)
        lw_lo = (lw_ - lw_hi.astype(F32)).astype(BF16)
        cum = (jnp.dot(tri, lw_hi, preferred_element_type=F32)
               + jnp.dot(tri, lw_lo, preferred_element_type=F32))
        cum_end = cum[c - 1:c, :]
        e_pos = jnp.exp(cum)
        e_neg = jnp.exp(-cum)
        e_end = jnp.exp(cum_end - cum)
        e_prev = jnp.exp(cum - lw_)
        pt_s = stack(pvec * e_prev)
        rt_s = stack(r_ * e_pos)
        ct_s = stack(cvec * e_neg)
        kt_s = stack(kmod * e_neg)
        ch_s = stack(cvec * e_end)
        kh_s = stack(kmod * e_end)
        v_s = stack(v_)
        pr = jnp.concatenate([pt_s, rt_s], axis=0)
        ck = jnp.concatenate([ct_s, kt_s], axis=0)
        gram = _dot_nt(pr, ck)
        l_pc = jnp.where(low_strict, gram[0:n2, 0:n2], 0.0)
        l_pk = jnp.where(low_strict, gram[0:n2, n2:2 * n2], 0.0)
        a_rc = jnp.where(low_incl, gram[n2:2 * n2, 0:n2], 0.0)
        a_rk = jnp.where(low_incl, gram[n2:2 * n2, n2:2 * n2], 0.0)
        t_inv = _unit_lower_inverse(-l_pc, c)
        z = z_scr[...]
        prz = _dot_nt(pr, z)
        u = _dot(t_inv, prz[0:n2] + _dot(l_pk, v_s))
        uv = jnp.concatenate([u, v_s], axis=0)
        y_s = prz[n2:2 * n2] + _dot(jnp.concatenate([a_rc, a_rk], axis=1), uv)
        y = y_s[0:c] + y_s[c:n2]
        z_scr[...] = z * jnp.exp(cum_end) + _dot_tn(uv, jnp.concatenate([ch_s, kh_s], axis=0))
        mean = seg_sum(y) * (1.0 / RW_HEAD)
        dev = y - mean
        var = seg_sum(dev * dev) * (1.0 / RW_HEAD)
        yn = dev * lax.rsqrt(var + RW_GN_EPS) * lnw + lnb
        bonus = seg_sum(r_ * kmod * rk_w) * v_
        out = (yn + bonus) * load(g_ref, r0)
        o_ref[r0:r0 + nrows, :] = out[0:nrows].astype(o_ref.dtype)

    @pl.when(t == nt - 1)
    def _():
        sout_ref[...] = z_scr[...]


def _wkv7(r, k, v, lw, a, g, k_k, k_a, r_k, ln_w, ln_b, s0_pairs, n_batch, seq_len):
    m, d = r.shape
    n_pairs = d // LANES
    tt = _tile(seq_len, 256)
    nt = seq_len // tt
    row_spec = pl.BlockSpec((tt, LANES), lambda b, hp, t: (b * nt + t, hp))
    vec_spec = pl.BlockSpec((1, LANES), lambda b, hp, t: (0, hp))
    st_spec = pl.BlockSpec((None, None, LANES, LANES), lambda b, hp, t: (b, hp, 0, 0))
    vec = lambda x: x.reshape(1, d)
    return pl.pallas_call(
        functools.partial(_wkv_kernel, tt=tt, chunk=RW_CHUNK),
        grid=(n_batch, n_pairs, nt),
        in_specs=[row_spec] * 6 + [vec_spec] * 5 + [st_spec],
        out_specs=[row_spec, st_spec],
        out_shape=[jax.ShapeDtypeStruct((m, d), BF16), jax.ShapeDtypeStruct(s0_pairs.shape, F32)],
        scratch_shapes=[pltpu.VMEM((LANES, LANES), F32)],
        compiler_params=_cparams(("parallel", "parallel", "arbitrary")),
        name="wkv7",
    )(r, k, v, lw, a, g, vec(k_k), vec(k_a), vec(r_k), vec(ln_w), vec(ln_b), s0_pairs)


def _pair_states(s):
    b, h, n, _ = s.shape
    s2 = s.reshape(b, h // 2, 2, n, n)
    zero = jnp.zeros((b, h // 2, n, n), s.dtype)
    top = jnp.concatenate([s2[:, :, 0], zero], axis=-1)
    bot = jnp.concatenate([zero, s2[:, :, 1]], axis=-1)
    return jnp.concatenate([top, bot], axis=-2)


def _unpair_states(sp):
    n = RW_HEAD
    b, hp = sp.shape[:2]
    return jnp.stack([sp[:, :, :n, :n], sp[:, :, n:, n:]], axis=2).reshape(b, 2 * hp, n, n)


def _router_kernel(x_ref, sh_ref, sc_ref, g_ref, rw_ref, rb_ref, h_ref, idx_ref, gate_ref):
    h = _rms(x_ref[...]) * g_ref[...] * (1.0 + sc_ref[...]) + sh_ref[...]
    h_ref[...] = h
    logits = jnp.dot(h, rw_ref[...], preferred_element_type=F32, precision=lax.Precision.HIGHEST) + rb_ref[...]
    lane = lax.broadcasted_iota(jnp.int32, logits.shape, 1)
    lane_f = lane.astype(F32)
    neg = -jnp.inf
    l1 = jnp.where(lane < N_EXPERTS, logits, neg)
    m1 = jnp.max(l1, axis=-1, keepdims=True)
    i1 = jnp.min(jnp.where(l1 == m1, lane_f, float(LANES)), axis=-1, keepdims=True)
    l2 = jnp.where(lane_f == i1, neg, l1)
    m2 = jnp.max(l2, axis=-1, keepdims=True)
    i2 = jnp.min(jnp.where(l2 == m2, lane_f, float(LANES)), axis=-1, keepdims=True)
    e = jnp.exp(m2 - m1)
    g0 = 1.0 / (1.0 + e)
    g1 = e / (1.0 + e)
    idx_ref[...] = jnp.where(lane == 0, i1, jnp.where(lane == 1, i2, 0.0)).astype(jnp.int32)
    gate_ref[...] = jnp.where(lane == 0, g0, jnp.where(lane == 1, g1, 0.0))


def _router(x, sh, sc, g, rw_pad, rb_pad, seq_len, tm):
    m, d = x.shape
    shm, scm = _Mod(sh, seq_len, tm), _Mod(sc, seq_len, tm)
    return pl.pallas_call(
        _router_kernel,
        grid=(m // tm,),
        in_specs=[pl.BlockSpec((tm, d), lambda i: (i, 0)), shm.spec1(), scm.spec1(),
                  pl.BlockSpec((1, d), lambda i: (0, 0)),
                  pl.BlockSpec((d, LANES), lambda i: (0, 0)), pl.BlockSpec((1, LANES), lambda i: (0, 0))],
        out_specs=[pl.BlockSpec((tm, d), lambda i: (i, 0)), pl.BlockSpec((tm, LANES), lambda i: (i, 0)),
                   pl.BlockSpec((tm, LANES), lambda i: (i, 0))],
        out_shape=[jax.ShapeDtypeStruct((m, d), F32), jax.ShapeDtypeStruct((m, LANES), jnp.int32),
                   jax.ShapeDtypeStruct((m, LANES), F32)],
        compiler_params=_cparams(("parallel",)),
        name="moe_router",
    )(x, shm.arr, scm.arr, g.reshape(1, d), rw_pad, rb_pad)


def _gather_rows_kernel(idx_ref, src_ref, o_ref, sem, *, tm):
    i = pl.program_id(0)

    def row_copy(r):
        tok = idx_ref[i * tm + r]
        return pltpu.make_async_copy(src_ref.at[pl.ds(tok, 1), :], o_ref.at[pl.ds(r, 1), :], sem)

    def start(r, carry):
        row_copy(r).start()
        return carry

    def wait(r, carry):
        row_copy(r).wait()
        return carry

    lax.fori_loop(0, tm, start, 0)
    lax.fori_loop(0, tm, wait, 0)


def _gather_rows(src, row_idx, tm):
    n_rows = row_idx.shape[0]
    d = src.shape[1]
    grid_spec = pltpu.PrefetchScalarGridSpec(
        num_scalar_prefetch=1,
        grid=(n_rows // tm,),
        in_specs=[pl.BlockSpec(memory_space=pl.ANY)],
        out_specs=pl.BlockSpec((tm, d), lambda i, idx: (i, 0)),
        scratch_shapes=[pltpu.SemaphoreType.DMA(())],
    )
    return pl.pallas_call(
        functools.partial(_gather_rows_kernel, tm=tm),
        grid_spec=grid_spec,
        out_shape=jax.ShapeDtypeStruct((n_rows, d), src.dtype),
        compiler_params=_cparams(("arbitrary",)),
        name="moe_gather",
    )(row_idx, src)


def _moe_ffn_kernel(te_ref, na_ref, x_ref, wg_ref, wu_ref, wd_ref, o_ref, h_ref):
    i = pl.program_id(0)
    f = pl.program_id(1)
    active = i < na_ref[0]

    @pl.when(jnp.logical_and(active, f == 0))
    def _():
        h_ref[...] = x_ref[...].astype(BF16)

    @pl.when(jnp.logical_not(active))
    def _():
        o_ref[...] = jnp.zeros(o_ref.shape, F32)

    @pl.when(active)
    def _():
        h = h_ref[...]
        gate = jnp.dot(h, wg_ref[...], preferred_element_type=F32)
        up = jnp.dot(h, wu_ref[...], preferred_element_type=F32)
        act = (_silu(gate) * up).astype(BF16)
        part = jnp.dot(act, wd_ref[...], preferred_element_type=F32)

        @pl.when(f == 0)
        def _():
            o_ref[...] = part

        @pl.when(f > 0)
        def _():
            o_ref[...] += part


def _moe_ffn(xs, tile_expert, n_active, wg, wu, wd, tm):
    n_rows, d = xs.shape
    dff = wg.shape[2]
    tf = _tile(dff, 512)
    nf = dff // tf

    def fcol(i, f, te, na):
        return jnp.where(i < na[0], f, nf - 1)

    grid_spec = pltpu.PrefetchScalarGridSpec(
        num_scalar_prefetch=2,
        grid=(n_rows // tm, nf),
        in_specs=[pl.BlockSpec((tm, d), lambda i, f, te, na: (i, 0)),
                  pl.BlockSpec((None, d, tf), lambda i, f, te, na: (te[i], 0, fcol(i, f, te, na))),
                  pl.BlockSpec((None, d, tf), lambda i, f, te, na: (te[i], 0, fcol(i, f, te, na))),
                  pl.BlockSpec((None, tf, d), lambda i, f, te, na: (te[i], fcol(i, f, te, na), 0))],
        out_specs=pl.BlockSpec((tm, d), lambda i, f, te, na: (i, 0)),
        scratch_shapes=[pltpu.VMEM((tm, d), BF16)],
    )
    return pl.pallas_call(
        _moe_ffn_kernel,
        grid_spec=grid_spec,
        out_shape=jax.ShapeDtypeStruct((n_rows, d), F32),
        compiler_params=_cparams(("parallel", "arbitrary")),
        name="moe_ffn",
    )(tile_expert, n_active, xs, wg, wu, wd)


def _combine_kernel(dest_ref, x_ref, gt_ref, gate_ref, g_ref, yb_ref, o_ref, ybuf, sem, *, tm, tok0):
    i = pl.program_id(0)

    def row_copy(n):
        r = n // TOP_K
        k = n % TOP_K
        src = dest_ref[(tok0 + i * tm) * TOP_K + n]
        return pltpu.make_async_copy(yb_ref.at[pl.ds(src, 1), :], ybuf.at[k, pl.ds(r, 1), :], sem)

    def start(n, carry):
        row_copy(n).start()
        return carry

    def wait(n, carry):
        row_copy(n).wait()
        return carry

    lax.fori_loop(0, tm * TOP_K, start, 0)
    lax.fori_loop(0, tm * TOP_K, wait, 0)
    gates = gate_ref[...]
    y = gates[:, 0:1] * ybuf[0] + gates[:, 1:2] * ybuf[1]
    x = x_ref[...] + gt_ref[...] * y
    o_ref[...] = _rms(x) * g_ref[...]


def _combine_final(x, gt, gates, dest, yb, g, tok0, seq_len, tm):
    m, d = x.shape
    gtm = _Mod(gt, seq_len, tm)
    if gtm.per_batch:
        tpb = gtm.tiles_per_batch
        gt_spec = pl.BlockSpec((None, 1, d), lambda i, dst: (i // tpb, 0, 0))
    else:
        gt_spec = pl.BlockSpec((tm, d), lambda i, dst: (i, 0))
    grid_spec = pltpu.PrefetchScalarGridSpec(
        num_scalar_prefetch=1,
        grid=(m // tm,),
        in_specs=[pl.BlockSpec((tm, d), lambda i, dst: (i, 0)), gt_spec,
                  pl.BlockSpec((tm, LANES), lambda i, dst: (i, 0)),
                  pl.BlockSpec((1, d), lambda i, dst: (0, 0)),
                  pl.BlockSpec(memory_space=pl.ANY)],
        out_specs=pl.BlockSpec((tm, d), lambda i, dst: (i, 0)),
        scratch_shapes=[pltpu.VMEM((TOP_K, tm, d), F32), pltpu.SemaphoreType.DMA(())],
    )
    return pl.pallas_call(
        functools.partial(_combine_kernel, tm=tm, tok0=tok0),
        grid_spec=grid_spec,
        out_shape=jax.ShapeDtypeStruct((m, d), F32),
        compiler_params=_cparams(("arbitrary",)),
        name="moe_combine_final",
    )(dest, x, gtm.arr, gates, g.reshape(1, d), yb)


def _split_mod(mod):
    return [mod[:, n * D_MODEL:(n + 1) * D_MODEL] for n in range(6)]


def _layer_even(x, mod, pos, conv_state, gdn_state, paged, p, n_batch, seq_len, tm):
    sh1, sc1, gt1, sh2, sc2, gt2 = mod
    m = x.shape[0]
    if seq_len % tm == 0:
        tabs = _rope_tables(pos)
    else:
        tabs = tuple(jnp.tile(t, (n_batch, 1)) for t in _rope_tables(pos))
    gq, gk, gv, gz, ba = _norm_proj(x, sh1, sc1, p["norm_mix0"], p["w_gdn"], (False,) * 4, p["w_ba"], None,
                                    seq_len, tm)
    dq, dk, dv = _norm_proj(x, sh1, sc1, p["norm_mix0"], p["w_da"], (True, True, False), None, tabs, seq_len, tm)
    o_gdn, new_gdn = _gated_deltanet(gq, gk, gv, gz, ba, conv_state, gdn_state, p["gdn_conv_w"], p["gdn_a_log"],
                                     p["gdn_dt_bias"], p["gdn_norm_w"], n_batch, seq_len)
    lam_args = (p["da_lq1"], p["da_lk1"], p["da_lq2"], p["da_lk2"], p["da_subln_w"])
    if paged is None:
        o_da = _diff_attention_prompt(dq, dk, dv, *lam_args, n_batch, seq_len)
    else:
        o_da = _diff_attention_sample(dq, dk, dv, paged[0], paged[1], paged[2], *lam_args, n_batch, seq_len)
    x = _proj_residual(x, gt1, [o_gdn, o_da], p["w_out"], seq_len, tm)
    x = _norm_ffn(x, sh2, sc2, gt2, p["norm_ffn0"], p["ffn_wg"], p["ffn_wu"], p["ffn_wd"], seq_len, tm)
    raw = jnp.concatenate([gq.reshape(n_batch, seq_len, -1), gk.reshape(n_batch, seq_len, -1),
                           gv.reshape(n_batch, seq_len, -1)], axis=-1)
    ext = jnp.concatenate([conv_state, raw[:, max(seq_len - (GDN_CONV - 1), 0):]], axis=1)
    new_conv = ext[:, ext.shape[1] - (GDN_CONV - 1):]
    k_out = dk.reshape(n_batch, seq_len, DA_HEADS, 2 * DA_DK)
    v_out = dv.reshape(n_batch, seq_len, DA_HEADS, DA_DV)
    return x, k_out, v_out, new_conv, new_gdn


def _layer_odd_mix(x, mod, shift_state, wkv_state, p, n_batch, seq_len, tm):
    sh1, sc1, gt1 = mod[:3]
    d = x.shape[1]
    tm_rw = _tile(tm, 512)
    r, k, v, lw, a, g = _rwkv_proj(x, sh1, sc1, shift_state, p["norm_mix1"], p["rw_mu"], p["rw_w1"], p["rw_a1"],
                                   p["rw_g1"], p["rw_wr"], p["rw_wk"], p["rw_wv"], p["rw_w2"], p["rw_a2"],
                                   p["rw_g2"], p["rw_w0"], p["rw_a0"], n_batch, seq_len, tm_rw)
    yg, s_pairs = _wkv7(r, k, v, lw, a, g, p["rw_k_k"], p["rw_k_a"], p["rw_r_k"], p["rw_ln_w"], p["rw_ln_b"],
                        _pair_states(wkv_state), n_batch, seq_len)
    last = x.reshape(n_batch, seq_len, d)[:, seq_len - 1]
    new_shift = _norm_mod(last, sh1, sc1, p["norm_mix1"])
    x = _proj_residual(x, gt1, [yg], [p["rw_wo"]], seq_len, tm)
    return x, new_shift, _unpair_states(s_pairs)


def kernel(x_prompt, x_sample, cache_k, cache_v, state_gdn_conv, state_gdn, state_rwkv_shift, state_rwkv, page_table, c_prompt, c_sample, ada_w0, ada_b0, norm_mix0, w_in0, gdn_conv_w, gdn_a_log, gdn_dt_bias, gdn_norm_w, da_lq1, da_lk1, da_lq2, da_lk2, da_subln_w, w_out0, norm_ffn0, ffn_w_gate, ffn_w_up, ffn_w_down, ada_w1, ada_b1, norm_mix1, rw_mu, rw_w0, rw_w1, rw_w2, rw_a0, rw_a1, rw_a2, rw_g1, rw_g2, rw_k_k, rw_k_a, rw_r_k, rw_wr, rw_wk, rw_wv, rw_wo, rw_ln_w, rw_ln_b, norm_ffn1, moe_router_w, moe_router_b, moe_w_gate, moe_w_up, moe_w_down, norm_final):
    bp, lp, d = x_prompt.shape
    bs, ls, _ = x_sample.shape
    n_pages = page_table.shape[1]
    past_len = n_pages * PAGE_SIZE
    mp, ms = bp * lp, bs * ls
    tm_p = _tile(lp, 1024)
    tm_s = _tile(ms, 256)

    qk_w = GDN_HEADS * GDN_DK
    c0 = 3 * qk_w
    c1 = c0 + qk_w
    c2 = c1 + 2 * GDN_HEADS
    da_w = DA_HEADS * 2 * DA_DK
    bf = lambda w: w.astype(BF16)
    w_ba = jnp.concatenate([w_in0[:, c1:c2], jnp.zeros((d, LANES - 2 * GDN_HEADS), F32)], axis=1)
    lora_pad = lambda w, axis: jnp.pad(w, [(0, (-w.shape[a]) % LANES if a == axis else 0) for a in range(2)])
    p = {
        "norm_mix0": norm_mix0,
        "w_gdn": [bf(w_in0[:, n * qk_w:(n + 1) * qk_w]) for n in range(4)],
        "w_ba": bf(w_ba),
        "w_da": [bf(w_in0[:, c2 + n * da_w:c2 + (n + 1) * da_w]) for n in range(3)],
        "gdn_conv_w": gdn_conv_w, "gdn_a_log": gdn_a_log, "gdn_dt_bias": gdn_dt_bias, "gdn_norm_w": gdn_norm_w,
        "da_lq1": da_lq1, "da_lk1": da_lk1, "da_lq2": da_lq2, "da_lk2": da_lk2, "da_subln_w": da_subln_w,
        "w_out": [bf(w_out0[:qk_w]), bf(w_out0[qk_w:])],
        "norm_ffn0": norm_ffn0, "ffn_wg": bf(ffn_w_gate), "ffn_wu": bf(ffn_w_up), "ffn_wd": bf(ffn_w_down),
        "norm_mix1": norm_mix1, "rw_mu": rw_mu, "rw_w0": rw_w0, "rw_a0": rw_a0,
        "rw_w1": bf(lora_pad(rw_w1, 1)), "rw_w2": bf(lora_pad(rw_w2, 0)),
        "rw_a1": bf(lora_pad(rw_a1, 1)), "rw_a2": bf(lora_pad(rw_a2, 0)),
        "rw_g1": bf(rw_g1), "rw_g2": bf(rw_g2),
        "rw_wr": bf(rw_wr), "rw_wk": bf(rw_wk), "rw_wv": bf(rw_wv), "rw_wo": bf(rw_wo),
        "rw_k_k": rw_k_k, "rw_k_a": rw_k_a, "rw_r_k": rw_r_k.reshape(-1), "rw_ln_w": rw_ln_w, "rw_ln_b": rw_ln_b,
    }

    c_all = jnp.concatenate([c_prompt, c_sample], axis=0)
    mod0 = _ada_mod(c_all, ada_w0, ada_b0)
    mod1 = _ada_mod(c_all, ada_w1, ada_b1)
    mod0_p, mod0_s = _split_mod(mod0[:bp]), _split_mod(mod0[bp:])
    mod1_p, mod1_s = _split_mod(mod1[:bp]), _split_mod(mod1[bp:])

    xp = x_prompt.reshape(mp, d)
    xs = x_sample.reshape(ms, d)
    pos_p = jnp.arange(lp, dtype=jnp.int32)
    pos_s = past_len + jnp.arange(ls, dtype=jnp.int32)

    xp, k_p, v_p, conv_p, gdn_p = _layer_even(
        xp, mod0_p, pos_p, jnp.zeros((bp, GDN_CONV - 1, 3 * qk_w), F32),
        jnp.zeros((bp, GDN_HEADS, GDN_DK, GDN_DK), F32), None, p, bp, lp, tm_p)
    xs, k_s, v_s, conv_s, gdn_s = _layer_even(
        xs, mod0_s, pos_s, state_gdn_conv, state_gdn, (cache_k, cache_v, page_table), p, bs, ls, tm_s)

    n_rw_heads = d // RW_HEAD
    xp, shift_p, rw_p = _layer_odd_mix(xp, mod1_p, jnp.zeros((bp, d), F32),
                                       jnp.zeros((bp, n_rw_heads, RW_HEAD, RW_HEAD), F32), p, bp, lp, tm_p)
    xs, shift_s, rw_s = _layer_odd_mix(xs, mod1_s, state_rwkv_shift, state_rwkv, p, bs, ls, tm_s)

    rw_pad = jnp.concatenate([moe_router_w, jnp.zeros((d, LANES - N_EXPERTS), F32)], axis=1)
    rb_pad = jnp.concatenate([moe_router_b, jnp.zeros((LANES - N_EXPERTS,), F32)]).reshape(1, LANES)
    tm_r = _tile(lp, 512)
    h_p, idx_p, gate_p = _router(xp, mod1_p[3], mod1_p[4], norm_ffn1, rw_pad, rb_pad, lp, tm_r)
    h_s, idx_s, gate_s = _router(xs, mod1_s[3], mod1_s[4], norm_ffn1, rw_pad, rb_pad, ls, tm_s)
    h_all = jnp.concatenate([h_p, h_s], axis=0)
    flat_e = jnp.concatenate([idx_p[:, :TOP_K], idx_s[:, :TOP_K]], axis=0).reshape(-1)
    n_assign = flat_e.shape[0]
    tmoe = MOE_TILE
    onehot = (flat_e[:, None] == jnp.arange(N_EXPERTS, dtype=jnp.int32)[None, :]).astype(jnp.int32)
    running = jnp.cumsum(onehot, axis=0)
    pos_in_e = jnp.sum((running - onehot) * onehot, axis=1)
    counts = running[-1]
    padded = (counts + tmoe - 1) // tmoe * tmoe
    pad_end = jnp.cumsum(padded)
    pad_start = pad_end - padded
    dest = (pad_start[flat_e] + pos_in_e).astype(jnp.int32)
    n_tiles = -(-n_assign // tmoe) + N_EXPERTS
    n_rows = n_tiles * tmoe
    row_tok = jnp.zeros((n_rows,), jnp.int32).at[dest].set(jnp.arange(n_assign, dtype=jnp.int32) // TOP_K)
    n_active = (pad_end[-1] // tmoe).astype(jnp.int32)
    tile_ids = jnp.minimum(jnp.arange(n_tiles, dtype=jnp.int32), n_active - 1)
    tile_e = jnp.minimum(jnp.searchsorted(pad_end, tile_ids * tmoe, side="right"), N_EXPERTS - 1).astype(jnp.int32)
    x_sorted = _gather_rows(h_all, row_tok, tmoe)
    yb = _moe_ffn(x_sorted, tile_e, n_active.reshape(1), bf(moe_w_gate), bf(moe_w_up), bf(moe_w_down), tmoe)
    tm_c = _tile(lp, 256)
    y_p = _combine_final(xp, mod1_p[5], gate_p, dest, yb, norm_final, 0, lp, tm_c)
    y_s = _combine_final(xs, mod1_s[5], gate_s, dest, yb, norm_final, mp, ls, tm_s)

    return (y_p.reshape(bp, lp, d), y_s.reshape(bs, ls, d), k_p, v_p, k_s, v_s, conv_p, conv_s,
            gdn_p, gdn_s, shift_p, shift_s, rw_p, rw_s)
```

```python
import functools
import math

import jax
import jax.numpy as jnp
from jax import lax
from jax.experimental import pallas as pl
from jax.experimental.pallas import tpu as pltpu

F32 = jnp.float32
BF16 = jnp.bfloat16

D_MODEL = 2048
NORM_EPS = 1e-6
GDN_HEADS = 8
GDN_DK = 128
GDN_CONV = 4
GDN_CHUNK = 64
DA_HEADS = 8
DA_DK = 64
DA_DV = 128
ROPE_DIM = 16
ROPE_THETA = 500000.0
DA_LAMBDA_INIT = 0.8 - 0.6 * math.exp(-0.3 * 0)
RW_HEAD = 64
RW_CHUNK = 64
RW_GN_EPS = 1e-5 * RW_HEAD
N_EXPERTS = 8
TOP_K = 2
PAGE_SIZE = 128
LANES = 128
SUBLANES = 8
VMEM_LIMIT = 56 * 1024 * 1024
MOE_TILE = 512
WKV_ITEMS_PER_STEP = 8
GDN_ITEMS_PER_STEP = 8


def _cparams(sem):
    return pltpu.CompilerParams(dimension_semantics=sem, vmem_limit_bytes=VMEM_LIMIT)


def _tile(n, pref):
    if n <= pref:
        return n
    t = pref
    while t >= SUBLANES:
        if n % t == 0 and t % SUBLANES == 0:
            return t
        t -= SUBLANES
    return n


def _dot(a, b):
    return jnp.dot(a.astype(BF16), b.astype(BF16), preferred_element_type=F32)


def _dot_nt(a, b):
    return lax.dot_general(a.astype(BF16), b.astype(BF16), (((1,), (1,)), ((), ())),
                           preferred_element_type=F32)


def _dot_tn(a, b):
    return lax.dot_general(a.astype(BF16), b.astype(BF16), (((0,), (0,)), ((), ())),
                           preferred_element_type=F32)


def _sigmoid(x):
    return 1.0 / (1.0 + jnp.exp(-x))


def _silu(x):
    return x * _sigmoid(x)


def _softplus(x):
    return jnp.maximum(x, 0.0) + jnp.log(1.0 + jnp.exp(-jnp.abs(x)))


def _rms(x, eps=NORM_EPS):
    return x * lax.rsqrt(jnp.mean(x * x, axis=-1, keepdims=True) + eps)


def _unit_lower_inverse(a, block):
    n = a.shape[0]
    row = lax.broadcasted_iota(jnp.int32, (n, n), 0)
    col = lax.broadcasted_iota(jnp.int32, (n, n), 1)
    inv = (row == col).astype(F32) - jnp.where((row // 2) == (col // 2), a, 0.0)
    s = 4
    while s <= block:
        a_s = jnp.where(((row // s) == (col // s)) & ((row // (s // 2)) != (col // (s // 2))), a, 0.0)
        inv = inv - _dot(_dot(inv, a_s), inv)
        s *= 2
    return inv


def _unit_lower_inverses(mats, block):
    n = mats[0].shape[0]
    row = lax.broadcasted_iota(jnp.int32, (n, n), 0)
    col = lax.broadcasted_iota(jnp.int32, (n, n), 1)
    eye = (row == col).astype(F32)
    pair = (row // 2) == (col // 2)
    invs = [eye - jnp.where(pair, a, 0.0) for a in mats]
    s = 4
    while s <= block:
        level = ((row // s) == (col // s)) & ((row // (s // 2)) != (col // (s // 2)))
        left = [_dot(inv, jnp.where(level, a, 0.0)) for inv, a in zip(invs, mats)]
        invs = [inv - _dot(lf, inv) for inv, lf in zip(invs, left)]
        s *= 2
    return invs


class _Mod:
    def __init__(self, m, seq_len, tm):
        self.width = m.shape[1]
        if seq_len % tm == 0:
            self.per_batch = True
            self.tiles_per_batch = seq_len // tm
            self.arr = m.reshape(m.shape[0], 1, self.width)
        else:
            assert tm % seq_len == 0
            self.per_batch = False
            self.arr = jnp.repeat(m, seq_len, axis=0)
        self.tm = tm

    def spec(self, tn=None, col=None):
        tn = self.width if tn is None else tn
        col = (lambda i, j: 0) if col is None else col
        if self.per_batch:
            tpb = self.tiles_per_batch
            return pl.BlockSpec((None, 1, tn), lambda i, j: (i // tpb, 0, col(i, j)))
        return pl.BlockSpec((self.tm, tn), lambda i, j: (i, col(i, j)))

    def spec1(self):
        if self.per_batch:
            tpb = self.tiles_per_batch
            return pl.BlockSpec((None, 1, self.width), lambda i: (i // tpb, 0, 0))
        return pl.BlockSpec((self.tm, self.width), lambda i: (i, 0))


def _ada_kernel(c_ref, w_ref, b_ref, o_ref):
    a = _silu(c_ref[...])
    o_ref[...] = _dot(a, w_ref[...]) + b_ref[...]


def _ada_mod(c, w, b):
    r, d = c.shape
    n = w.shape[1]
    tn = _tile(n, 1024)
    return pl.pallas_call(
        _ada_kernel,
        grid=(n // tn,),
        in_specs=[pl.BlockSpec((r, d), lambda j: (0, 0)),
                  pl.BlockSpec((d, tn), lambda j: (0, j)),
                  pl.BlockSpec((1, tn), lambda j: (0, j))],
        out_specs=pl.BlockSpec((r, tn), lambda j: (0, j)),
        out_shape=jax.ShapeDtypeStruct((r, n), F32),
        compiler_params=_cparams(("parallel",)),
        name="ada_mod",
    )(c, w, b.reshape(1, n))


def _rope_tile(acc, cos, sin_lo, sin_hi):
    fwd = pltpu.roll(acc, LANES - ROPE_DIM // 2, 1)
    bwd = pltpu.roll(acc, ROPE_DIM // 2, 1)
    return acc * cos + fwd * sin_lo + bwd * sin_hi


def _norm_proj_kernel(*refs, n_w, rope, has_extra, tn):
    x_ref, sh_ref, sc_ref, g_ref = refs[:4]
    pos = 4
    w_refs = refs[pos:pos + n_w]
    pos += n_w
    if has_extra:
        we_ref = refs[pos]
        pos += 1
    if any(rope):
        cos_ref, slo_ref, shi_ref = refs[pos:pos + 3]
        pos += 3
    o_refs = refs[pos:pos + n_w]
    pos += n_w
    if has_extra:
        oe_ref = refs[pos]
        pos += 1
    h_ref = refs[pos]
    j = pl.program_id(1)

    @pl.when(j == 0)
    def _():
        h = _rms(x_ref[...]) * g_ref[...] * (1.0 + sc_ref[...]) + sh_ref[...]
        h_ref[...] = h.astype(BF16)
        if has_extra:
            oe_ref[...] = jnp.dot(h_ref[...], we_ref[...], preferred_element_type=F32)

    h = h_ref[...]
    for k in range(n_w):
        acc = jnp.dot(h, w_refs[k][...], preferred_element_type=F32)
        if rope[k]:
            cos, slo, shi = cos_ref[...], slo_ref[...], shi_ref[...]
            for c in range(tn // LANES):
                sl = slice(c * LANES, (c + 1) * LANES)
                o_refs[k][:, sl] = _rope_tile(acc[:, sl], cos, slo, shi)
        else:
            o_refs[k][...] = acc


def _norm_proj(x, sh, sc, g, ws, rope, extra_w, rope_tabs, seq_len, tm):
    m, d = x.shape
    n = ws[0].shape[1]
    tn = _tile(n, 256)
    n_w = len(ws)
    shm, scm = _Mod(sh, seq_len, tm), _Mod(sc, seq_len, tm)
    in_specs = [pl.BlockSpec((tm, d), lambda i, j: (i, 0)), shm.spec(), scm.spec(),
                pl.BlockSpec((1, d), lambda i, j: (0, 0))]
    args = [x, shm.arr, scm.arr, g.reshape(1, d)]
    for w in ws:
        in_specs.append(pl.BlockSpec((d, tn), lambda i, j: (0, j)))
        args.append(w)
    has_extra = extra_w is not None
    if has_extra:
        in_specs.append(pl.BlockSpec((d, LANES), lambda i, j: (0, 0)))
        args.append(extra_w)
    if any(rope):
        if seq_len % tm == 0:
            tpb = seq_len // tm
            tab_spec = pl.BlockSpec((tm, LANES), lambda i, j: (i % tpb, 0))
        else:
            tab_spec = pl.BlockSpec((tm, LANES), lambda i, j: (i, 0))
        for t in rope_tabs:
            in_specs.append(tab_spec)
            args.append(t)
    out_specs = [pl.BlockSpec((tm, tn), lambda i, j: (i, j)) for _ in ws]
    out_shape = [jax.ShapeDtypeStruct((m, n), F32) for _ in ws]
    if has_extra:
        out_specs.append(pl.BlockSpec((tm, LANES), lambda i, j: (i, 0)))
        out_shape.append(jax.ShapeDtypeStruct((m, LANES), F32))
    return pl.pallas_call(
        functools.partial(_norm_proj_kernel, n_w=n_w, rope=tuple(rope), has_extra=has_extra, tn=tn),
        grid=(m // tm, n // tn),
        in_specs=in_specs,
        out_specs=out_specs,
        out_shape=out_shape,
        scratch_shapes=[pltpu.VMEM((tm, d), BF16)],
        compiler_params=_cparams(("parallel", "arbitrary")),
        name="norm_proj",
    )(*args)


def _rope_tables(pos):
    half = ROPE_DIM // 2
    inv_freq = ROPE_THETA ** (-jnp.arange(half, dtype=F32) * (2.0 / ROPE_DIM))
    ang = pos.astype(F32)[:, None] * inv_freq
    cos, sin = jnp.cos(ang), jnp.sin(ang)
    n = pos.shape[0]
    ones = jnp.ones((n, DA_DK - ROPE_DIM), F32)
    zeros = jnp.zeros((n, DA_DK - ROPE_DIM), F32)
    zh = jnp.zeros((n, half), F32)
    cos_m = jnp.concatenate([cos, cos, ones], axis=1)
    slo_m = jnp.concatenate([-sin, zh, zeros], axis=1)
    shi_m = jnp.concatenate([zh, sin, zeros], axis=1)
    return tuple(jnp.concatenate([t, t], axis=1) for t in (cos_m, slo_m, shi_m))


def _gdn_kernel(q_ref, k_ref, v_ref, z_ref, ba_ref, csq_ref, csk_ref, csv_ref, cwq_ref, cwk_ref, cwv_ref,
                s0_ref, alog_ref, dtb_ref, nw_ref, o_ref, sout_ref, s_scr, hist_scr, ext_scr, *, tt, chunk):
    hp = pl.program_id(1)
    t = pl.program_id(2)
    nt = pl.num_programs(2)
    n_hist = SUBLANES

    @pl.when(t == 0)
    def _():
        s_scr[...] = s0_ref[...]
        hist_scr[0] = csq_ref[...]
        hist_scr[1] = csk_ref[...]
        hist_scr[2] = csv_ref[...]

    conv = []
    for s, (raw_ref, cw_ref) in enumerate(((q_ref, cwq_ref), (k_ref, cwk_ref), (v_ref, cwv_ref))):
        raw = raw_ref[...]
        ext_scr[s, 0:n_hist, :] = hist_scr[s]
        ext_scr[s, n_hist:n_hist + tt, :] = raw
        cw = cw_ref[...]
        y = raw * cw[GDN_CONV - 1:GDN_CONV, :]
        for dly in range(1, GDN_CONV):
            y = y + ext_scr[s, n_hist - dly:n_hist - dly + tt, :] * cw[GDN_CONV - 1 - dly:GDN_CONV - dly, :]
        hist_scr[s] = raw[tt - n_hist:tt, :]
        conv.append(_silu(y))
    q_all, k_all, v_all = conv
    z_all = z_ref[...]

    ba = ba_ref[...]
    lane = lax.broadcasted_iota(jnp.int32, ba.shape, 1)
    gates = jnp.where(lane < GDN_HEADS, _sigmoid(ba), -jnp.exp(alog_ref[...]) * _softplus(ba + dtb_ref[...]))
    n_heads = q_ref.shape[1] // LANES
    n_pairs = n_heads // 2
    beta_cols, g_cols = [], []
    for hq in range(n_heads):
        hh = n_heads * hp + hq
        beta_cols.append(jnp.sum(jnp.where(lane == hh, gates, 0.0), axis=1, keepdims=True))
        g_cols.append(jnp.sum(jnp.where(lane == GDN_HEADS + hh, gates, 0.0), axis=1, keepdims=True))

    c = chunk
    n2 = 2 * c
    row = lax.broadcasted_iota(jnp.int32, (n2, n2), 0)
    col = lax.broadcasted_iota(jnp.int32, (n2, n2), 1)
    same = (row // c) == (col // c)
    low_incl = same & (col <= row)
    low_strict = same & (col < row)
    up_incl = same & (row <= col)
    eye = row == col
    nw = nw_ref[...]

    def stack(x_all, r0, pq):
        return jnp.concatenate([x_all[r0:r0 + c, (2 * pq + hl) * LANES:(2 * pq + hl + 1) * LANES]
                                for hl in range(2)], axis=0)

    def stack_col(cols, r0, pq):
        return jnp.concatenate([cols[2 * pq + hl][r0:r0 + c, :] for hl in range(2)], axis=0)

    n_chunks = tt // c
    items = [(pq, ci) for pq in range(n_pairs) for ci in range(n_chunks)]
    rng = range(len(items))
    a_mats, qks, kes, bvs, qes, kds, s_decay = ([] for _ in range(7))
    for pq, ci in items:
        r0 = ci * c
        q = stack(q_all, r0, pq)
        k = stack(k_all, r0, pq)
        beta = stack_col(beta_cols, r0, pq)
        g = stack_col(g_cols, r0, pq)
        q = q * lax.rsqrt(jnp.sum(q * q, axis=-1, keepdims=True) + 1e-12) * (GDN_DK ** -0.5)
        k = k * lax.rsqrt(jnp.sum(k * k, axis=-1, keepdims=True) + 1e-12)
        g_row = jnp.sum(jnp.where(eye, g, 0.0), axis=0, keepdims=True)
        gc_col = jnp.sum(jnp.where(low_incl, g_row, 0.0), axis=1, keepdims=True)
        gc_row = jnp.sum(jnp.where(up_incl, g, 0.0), axis=0, keepdims=True)
        decay = jnp.where(low_incl, jnp.exp(jnp.where(low_incl, gc_col - gc_row, 0.0)), 0.0)
        gram = _dot_nt(jnp.concatenate([k * beta, q], axis=0), k)
        a_mats.append(jnp.where(low_strict, gram[0:n2] * decay, 0.0))
        qks.append(jnp.where(low_incl, gram[n2:2 * n2] * decay, 0.0))
        egc = jnp.exp(gc_col)
        kes.append(k * (beta * egc))
        bvs.append(stack(v_all, r0, pq) * beta)
        qes.append(q * egc)
        g_last = [gc_col[(hl + 1) * c - 1:(hl + 1) * c, :] for hl in range(2)]
        kds.append([k[hl * c:(hl + 1) * c] * jnp.exp(g_last[hl] - gc_col[hl * c:(hl + 1) * c]) for hl in range(2)])
        s_decay.append([jnp.exp(g_last[hl]) for hl in range(2)])
    t_inv = _unit_lower_inverses(a_mats, c)
    mb1 = [_dot(t_inv[i], jnp.concatenate([kes[i], bvs[i]], axis=1)) for i in rng]
    mb2 = [_dot(qks[i], mb1[i]) for i in rng]
    m2 = [qes[i] - mb2[i][:, 0:LANES] for i in rng]
    b2 = [mb2[i][:, LANES:2 * LANES] for i in rng]
    mb3 = [[_dot_tn(kds[i][hl], mb1[i][hl * c:(hl + 1) * c]) for hl in range(2)] for i in rng]

    outs = []
    for pq in range(n_pairs):
        s = [s_scr[2 * pq], s_scr[2 * pq + 1]]
        for ci in range(n_chunks):
            i = pq * n_chunks + ci
            s_cat = jnp.concatenate(s, axis=1).astype(BF16)
            o_wide = jnp.dot(m2[i].astype(BF16), s_cat, preferred_element_type=F32)
            outs.append(jnp.concatenate([o_wide[0:c, 0:LANES], o_wide[c:n2, LANES:2 * LANES]], axis=0) + b2[i])
            s = [s[hl] * s_decay[i][hl] - _dot(mb3[i][hl][:, 0:LANES], s[hl]) + mb3[i][hl][:, LANES:2 * LANES]
                 for hl in range(2)]
        s_scr[2 * pq] = s[0]
        s_scr[2 * pq + 1] = s[1]

    for i, (pq, ci) in enumerate(items):
        r0 = ci * c
        o = _rms(outs[i]) * nw * _silu(stack(z_all, r0, pq))
        for hl in range(2):
            lanes = slice((2 * pq + hl) * LANES, (2 * pq + hl + 1) * LANES)
            o_ref[r0:r0 + c, lanes] = o[hl * c:(hl + 1) * c].astype(o_ref.dtype)

    @pl.when(t == nt - 1)
    def _():
        sout_ref[...] = s_scr[...]


def _gated_deltanet(qraw, kraw, vraw, z, ba, conv_state, s0, conv_w, a_log, dt_bias, norm_w, n_batch, seq_len):
    m = qraw.shape[0]
    tt = _tile(seq_len, 256)
    nt = seq_len // tt
    chunk = min(GDN_CHUNK, tt)
    assert tt % chunk == 0
    n_hist = SUBLANES
    pairs = min(GDN_HEADS // 2, max(1, GDN_ITEMS_PER_STEP // (tt // chunk)))
    w2 = 2 * pairs * LANES
    qkv_w = 3 * GDN_HEADS * GDN_DK
    cs = jnp.concatenate([jnp.zeros((n_batch, n_hist - (GDN_CONV - 1), qkv_w), F32), conv_state], axis=1)
    cw = jnp.concatenate([conv_w, jnp.zeros((n_hist - GDN_CONV, qkv_w), F32)], axis=0)
    alog = jnp.zeros((1, LANES), F32).at[0, GDN_HEADS:2 * GDN_HEADS].set(a_log)
    dtb = jnp.zeros((1, LANES), F32).at[0, GDN_HEADS:2 * GDN_HEADS].set(dt_bias)
    nblk = GDN_HEADS // (2 * pairs)
    row_spec = pl.BlockSpec((tt, w2), lambda b, hp, t: (b * nt + t, hp))
    cs_specs = [pl.BlockSpec((None, n_hist, w2), functools.partial(lambda b, hp, t, s: (b, 0, s * nblk + hp), s=s))
                for s in range(3)]
    cw_specs = [pl.BlockSpec((n_hist, w2), functools.partial(lambda b, hp, t, s: (0, s * nblk + hp), s=s))
                for s in range(3)]
    st_spec = pl.BlockSpec((None, 2 * pairs, GDN_DK, GDN_DK), lambda b, hp, t: (b, hp, 0, 0))
    vec_spec = pl.BlockSpec((1, LANES), lambda b, hp, t: (0, 0))
    return pl.pallas_call(
        functools.partial(_gdn_kernel, tt=tt, chunk=chunk),
        grid=(n_batch, nblk, nt),
        in_specs=[row_spec, row_spec, row_spec, row_spec,
                  pl.BlockSpec((tt, LANES), lambda b, hp, t: (b * nt + t, 0)),
                  *cs_specs, *cw_specs, st_spec, vec_spec, vec_spec, vec_spec],
        out_specs=[row_spec, st_spec],
        out_shape=[jax.ShapeDtypeStruct((m, GDN_HEADS * GDN_DK), BF16),
                   jax.ShapeDtypeStruct(s0.shape, F32)],
        scratch_shapes=[pltpu.VMEM((2 * pairs, GDN_DK, GDN_DK), F32),
                        pltpu.VMEM((3, n_hist, w2), F32),
                        pltpu.VMEM((3, n_hist + tt, w2), F32)],
        compiler_params=_cparams(("parallel", "parallel", "arbitrary")),
        name="gated_deltanet",
    )(qraw, kraw, vraw, z, ba, cs, cs, cs, cw, cw, cw, s0, alog, dtb, norm_w.reshape(1, LANES))


def _lambda(lq1_ref, lk1_ref, lq2_ref, lk2_ref):
    s1 = jnp.sum(lq1_ref[...] * lk1_ref[...], axis=-1, keepdims=True)
    s2 = jnp.sum(lq2_ref[...] * lk2_ref[...], axis=-1, keepdims=True)
    return jnp.exp(s1) - jnp.exp(s2) + DA_LAMBDA_INIT


def _flash_kernel(q_ref, k_ref, v_ref, lq1_ref, lk1_ref, lq2_ref, lk2_ref, sw_ref, o_ref,
                  *, tq, tk):
    qi = pl.program_id(2)
    q = q_ref[...] * (DA_DK ** -0.5)
    lane = lax.broadcasted_iota(jnp.int32, q.shape, 1)
    q_maps = (jnp.where(lane < DA_DK, q, 0.0).astype(BF16), jnp.where(lane >= DA_DK, q, 0.0).astype(BF16))

    def block(ki, carry, diagonal):
        start = pl.multiple_of(ki * tk, tk)
        k = k_ref[pl.ds(start, tk), :].astype(BF16)
        v = v_ref[pl.ds(start, tk), :].astype(BF16)
        if diagonal:
            visible = (lax.broadcasted_iota(jnp.int32, (tq, tk), 1) <= lax.broadcasted_iota(jnp.int32, (tq, tk), 0))
        new = []
        for mp in range(2):
            m_old, l_old, acc_old = carry[mp]
            s = lax.dot_general(q_maps[mp], k, (((1,), (1,)), ((), ())), preferred_element_type=F32)
            if diagonal:
                s = jnp.where(visible, s, -jnp.inf)
            m_new = jnp.maximum(m_old, jnp.max(s, axis=-1, keepdims=True))
            alpha = jnp.exp(m_old - m_new)
            p = jnp.exp(s - m_new)
            l_new = alpha * l_old + jnp.sum(p, axis=-1, keepdims=True)
            acc_new = alpha * acc_old + jnp.dot(p.astype(BF16), v, preferred_element_type=F32)
            new.append((m_new, l_new, acc_new))
        return tuple(new)

    init = tuple((jnp.full((tq, 1), -jnp.inf, F32), jnp.zeros((tq, 1), F32), jnp.zeros((tq, LANES), F32))
                 for _ in range(2))
    carry = lax.fori_loop(0, qi, lambda ki, c: block(ki, c, False), init)
    (_, l0, acc0), (_, l1, acc1) = block(qi, carry, True)
    lam = _lambda(lq1_ref, lk1_ref, lq2_ref, lk2_ref)
    o = acc0 / l0 - lam * (acc1 / l1)
    o = _rms(o) * sw_ref[...] * (1.0 - DA_LAMBDA_INIT)
    o_ref[...] = o.astype(o_ref.dtype)


def _diff_attention_prompt(q, k, v, lq1, lk1, lq2, lk2, subln_w, n_batch, seq_len):
    m = q.shape[0]
    tq = _tile(seq_len, 512)
    tk = tq
    nq = seq_len // tq
    vec = lambda a: a.reshape(1, -1)
    vspec = lambda n: pl.BlockSpec((1, n), lambda b, h, qi: (0, 0))
    kv_spec = pl.BlockSpec((seq_len, LANES), lambda b, h, qi: (b, h))
    return pl.pallas_call(
        functools.partial(_flash_kernel, tq=tq, tk=tk),
        grid=(n_batch, DA_HEADS, nq),
        in_specs=[pl.BlockSpec((tq, LANES), lambda b, h, qi: (b * nq + qi, h)), kv_spec, kv_spec,
                  vspec(DA_DK), vspec(DA_DK), vspec(DA_DK), vspec(DA_DK), vspec(DA_DV)],
        out_specs=pl.BlockSpec((tq, LANES), lambda b, h, qi: (b * nq + qi, h)),
        out_shape=jax.ShapeDtypeStruct((m, DA_HEADS * DA_DV), BF16),
        compiler_params=_cparams(("parallel", "parallel", "arbitrary")),
        name="diff_attn_prompt",
    )(q, k, v, vec(lq1), vec(lk1), vec(lq2), vec(lk2), vec(subln_w))


def _paged_attn_kernel(pt_ref, q_ref, kn_ref, vn_ref, *refs, n_group, ls):
    k_refs = refs[:n_group]
    v_refs = refs[n_group:2 * n_group]
    lq1_ref, lk1_ref, lq2_ref, lk2_ref, sw_ref, o_ref, qs_scr, m_scr, l_scr, acc_scr = refs[2 * n_group:]
    p_idx = pl.program_id(1)
    n_steps = pl.num_programs(1)
    n_rows = 2 * ls

    @pl.when(p_idx == 0)
    def _():
        q = q_ref[...] * (DA_DK ** -0.5)
        q2 = jnp.concatenate([q, q], axis=0)
        r = lax.broadcasted_iota(jnp.int32, q2.shape, 0)
        lane = lax.broadcasted_iota(jnp.int32, q2.shape, 1)
        keep = (r // ls) == ((lane % LANES) // DA_DK)
        qs_scr[...] = jnp.where(keep, q2, 0.0).astype(BF16)
        m_scr[...] = jnp.full(m_scr.shape, -jnp.inf, F32)
        l_scr[...] = jnp.zeros(l_scr.shape, F32)
        acc_scr[...] = jnp.zeros(acc_scr.shape, F32)

    def update(k_heads, v_heads, visible):
        s = jnp.concatenate(
            [lax.dot_general(qs_scr[:, h * LANES:(h + 1) * LANES], k_heads[h], (((1,), (1,)), ((), ())),
                             preferred_element_type=F32) for h in range(DA_HEADS)], axis=0)
        if visible is not None:
            s = jnp.where(visible, s, -jnp.inf)
        m_old = m_scr[...]
        m_new = jnp.maximum(m_old, jnp.max(s, axis=-1, keepdims=True))
        alpha = jnp.exp(m_old - m_new)
        p = jnp.exp(s - m_new)
        l_scr[...] = alpha * l_scr[...] + jnp.sum(p, axis=-1, keepdims=True)
        p = p.astype(BF16)
        pv = jnp.concatenate([jnp.dot(p[h * n_rows:(h + 1) * n_rows], v_heads[h], preferred_element_type=F32)
                              for h in range(DA_HEADS)], axis=0)
        acc_scr[...] = alpha * acc_scr[...] + pv
        m_scr[...] = m_new

    update([jnp.concatenate([kr[pl.ds(h, PAGE_SIZE, stride=DA_HEADS), :].astype(BF16) for kr in k_refs], axis=0)
            for h in range(DA_HEADS)],
           [jnp.concatenate([vr[pl.ds(h, PAGE_SIZE, stride=DA_HEADS), :].astype(BF16) for vr in v_refs], axis=0)
            for h in range(DA_HEADS)], None)

    @pl.when(p_idx == n_steps - 1)
    def _():
        lam = _lambda(lq1_ref, lk1_ref, lq2_ref, lk2_ref)
        pad = jnp.zeros((LANES - ls, DA_HEADS * LANES), F32)
        kn = jnp.concatenate([kn_ref[...], pad], axis=0).astype(BF16)
        vn = jnp.concatenate([vn_ref[...], pad], axis=0).astype(BF16)
        r = lax.broadcasted_iota(jnp.int32, (DA_HEADS * n_rows, LANES), 0)
        c = lax.broadcasted_iota(jnp.int32, (DA_HEADS * n_rows, LANES), 1)
        update([kn[:, h * LANES:(h + 1) * LANES] for h in range(DA_HEADS)],
               [vn[:, h * LANES:(h + 1) * LANES] for h in range(DA_HEADS)], c <= (r % ls))
        acc = acc_scr[...] / l_scr[...]
        for h in range(DA_HEADS):
            o = acc[h * n_rows:h * n_rows + ls] - lam * acc[h * n_rows + ls:(h + 1) * n_rows]
            o = _rms(o) * sw_ref[...] * (1.0 - DA_LAMBDA_INIT)
            o_ref[:, h * LANES:(h + 1) * LANES] = o.astype(o_ref.dtype)


def _diff_attention_sample(q, k_new, v_new, cache_k, cache_v, page_table, lq1, lk1, lq2, lk2, subln_w, n_batch, ls):
    n_pool = cache_k.shape[0]
    width = DA_HEADS * LANES
    ck = cache_k.reshape(n_pool, PAGE_SIZE * DA_HEADS, LANES)
    cv = cache_v.reshape(n_pool, PAGE_SIZE * DA_HEADS, LANES)
    n_pages = page_table.shape[1]
    n_group = 4 if n_pages % 4 == 0 else (2 if n_pages % 2 == 0 else 1)
    n_steps = n_pages // n_group
    row_spec = pl.BlockSpec((ls, width), lambda b, p, pt: (b, 0))
    page_specs = [pl.BlockSpec((None, PAGE_SIZE * DA_HEADS, LANES),
                               functools.partial(lambda b, p, pt, g: (pt[b, p * n_group + g], 0, 0), g=g))
                  for g in range(n_group)]
    vec = lambda a: a.reshape(1, -1)
    vspec = lambda n: pl.BlockSpec((1, n), lambda b, p, pt: (0, 0))
    grid_spec = pltpu.PrefetchScalarGridSpec(
        num_scalar_prefetch=1,
        grid=(n_batch, n_steps),
        in_specs=[row_spec, row_spec, row_spec, *page_specs, *page_specs,
                  vspec(DA_DK), vspec(DA_DK), vspec(DA_DK), vspec(DA_DK), vspec(DA_DV)],
        out_specs=row_spec,
        scratch_shapes=[pltpu.VMEM((2 * ls, width), BF16), pltpu.VMEM((DA_HEADS * 2 * ls, 1), F32),
                        pltpu.VMEM((DA_HEADS * 2 * ls, 1), F32), pltpu.VMEM((DA_HEADS * 2 * ls, LANES), F32)],
    )
    return pl.pallas_call(
        functools.partial(_paged_attn_kernel, n_group=n_group, ls=ls),
        grid_spec=grid_spec,
        out_shape=jax.ShapeDtypeStruct((n_batch * ls, width), BF16),
        compiler_params=_cparams(("parallel", "arbitrary")),
        name="diff_attn_sample",
    )(page_table, q, k_new, v_new, *([ck] * n_group), *([cv] * n_group),
      vec(lq1), vec(lk1), vec(lq2), vec(lk2), vec(subln_w))


def _proj_res_kernel(*refs, n_parts):
    x_ref, gt_ref = refs[:2]
    a_refs = refs[2:2 + n_parts]
    w_refs = refs[2 + n_parts:2 + 2 * n_parts]
    o_ref = refs[2 + 2 * n_parts]
    acc = jnp.dot(a_refs[0][...], w_refs[0][...], preferred_element_type=F32)
    for p in range(1, n_parts):
        acc = acc + jnp.dot(a_refs[p][...], w_refs[p][...], preferred_element_type=F32)
    o_ref[...] = x_ref[...] + gt_ref[...] * acc


def _proj_residual(x, gt, parts, ws, seq_len, tm):
    m, n = x.shape
    tn = _tile(n, 512)
    gtm = _Mod(gt, seq_len, tm)
    in_specs = [pl.BlockSpec((tm, tn), lambda i, j: (i, j)), gtm.spec(tn, lambda i, j: j)]
    for a in parts:
        in_specs.append(pl.BlockSpec((tm, a.shape[1]), lambda i, j: (i, 0)))
    for w in ws:
        in_specs.append(pl.BlockSpec((w.shape[0], tn), lambda i, j: (0, j)))
    return pl.pallas_call(
        functools.partial(_proj_res_kernel, n_parts=len(parts)),
        grid=(m // tm, n // tn),
        in_specs=in_specs,
        out_specs=pl.BlockSpec((tm, tn), lambda i, j: (i, j)),
        out_shape=jax.ShapeDtypeStruct((m, n), F32),
        compiler_params=_cparams(("parallel", "arbitrary")),
        name="proj_residual",
    )(x, gtm.arr, *parts, *ws)


def _norm_ffn_kernel(x_ref, sh_ref, sc_ref, gt_ref, g_ref, wg_ref, wu_ref, wd_ref, o_ref, h_ref):
    f = pl.program_id(1)
    nf = pl.num_programs(1)

    @pl.when(f == 0)
    def _():
        h = _rms(x_ref[...]) * g_ref[...] * (1.0 + sc_ref[...]) + sh_ref[...]
        h_ref[...] = h.astype(BF16)

    h = h_ref[...]
    gate = jnp.dot(h, wg_ref[...], preferred_element_type=F32)
    up = jnp.dot(h, wu_ref[...], preferred_element_type=F32)
    act = (_silu(gate) * up).astype(BF16)
    part = jnp.dot(act, wd_ref[...], preferred_element_type=F32)

    @pl.when(f == 0)
    def _():
        o_ref[...] = part

    @pl.when(f > 0)
    def _():
        o_ref[...] += part

    @pl.when(f == nf - 1)
    def _():
        o_ref[...] = x_ref[...] + gt_ref[...] * o_ref[...]


def _norm_ffn(x, sh, sc, gt, g, wg, wu, wd, seq_len, tm):
    m, d = x.shape
    dff = wg.shape[1]
    tf = _tile(dff, 512)
    tm = _tile(tm, 512)
    shm, scm, gtm = _Mod(sh, seq_len, tm), _Mod(sc, seq_len, tm), _Mod(gt, seq_len, tm)
    return pl.pallas_call(
        _norm_ffn_kernel,
        grid=(m // tm, dff // tf),
        in_specs=[pl.BlockSpec((tm, d), lambda i, f: (i, 0)), shm.spec(), scm.spec(), gtm.spec(),
                  pl.BlockSpec((1, d), lambda i, f: (0, 0)),
                  pl.BlockSpec((d, tf), lambda i, f: (0, f)),
                  pl.BlockSpec((d, tf), lambda i, f: (0, f)),
                  pl.BlockSpec((tf, d), lambda i, f: (f, 0))],
        out_specs=pl.BlockSpec((tm, d), lambda i, f: (i, 0)),
        out_shape=jax.ShapeDtypeStruct((m, d), F32),
        scratch_shapes=[pltpu.VMEM((tm, d), BF16)],
        compiler_params=_cparams(("parallel", "arbitrary")),
        name="norm_ffn",
    )(x, shm.arr, scm.arr, gtm.arr, g.reshape(1, d), wg, wu, wd)


def _rwkv_proj_kernel(x_ref, xp_ref, sh_ref, sc_ref, shift_ref, g_ref, mu_ref, w1_ref, a1_ref, g1_ref,
                      wr_ref, wk_ref, wv_ref, w2_ref, a2_ref, g2_ref, w0_ref, a0_ref,
                      r_ref, k_ref, v_ref, lw_ref, a_ref, gg_ref,
                      mix_scr, sw_scr, sa_scr, sg_scr, *, tm, seq_len):
    i = pl.program_id(0)
    j = pl.program_id(1)

    @pl.when(j == 0)
    def _():
        g = g_ref[...]
        scale = 1.0 + sc_ref[...]
        shift = sh_ref[...]
        h = _rms(x_ref[...]) * g * scale + shift
        row = lax.broadcasted_iota(jnp.int32, h.shape, 0)
        rolled = pltpu.roll(h, 1, 0)
        if seq_len % tm == 0:
            sc_row = scale[0:1] if scale.shape[0] > 1 else scale
            sh_row = shift[0:1] if shift.shape[0] > 1 else shift
            hp = _rms(xp_ref[...]) * g * sc_row + sh_row
            first = jnp.where((i % (seq_len // tm)) == 0, shift_ref[...], hp[SUBLANES - 1:SUBLANES])
            prev = jnp.where(row == 0, first, rolled)
        else:
            prev = jnp.where((row % seq_len) == 0, shift_ref[...], rolled)
        xx = prev - h
        mu = mu_ref[...]
        for n in range(6):
            mix_scr[n] = (h + xx * mu[n:n + 1]).astype(BF16)
        sw_scr[...] = jnp.tanh(jnp.dot(mix_scr[1], w1_ref[...], preferred_element_type=F32)).astype(BF16)
        sa_scr[...] = jnp.dot(mix_scr[4], a1_ref[...], preferred_element_type=F32).astype(BF16)
        sg_scr[...] = _sigmoid(jnp.dot(mix_scr[5], g1_ref[...], preferred_element_type=F32)).astype(BF16)

    r_ref[...] = jnp.dot(mix_scr[0], wr_ref[...], preferred_element_type=F32)
    k_ref[...] = jnp.dot(mix_scr[2], wk_ref[...], preferred_element_type=F32)
    v_ref[...] = jnp.dot(mix_scr[3], wv_ref[...], preferred_element_type=F32)
    wl = w0_ref[...] + jnp.dot(sw_scr[...], w2_ref[...], preferred_element_type=F32)
    lw_ref[...] = -jnp.exp(-_softplus(-wl) - 0.5)
    a_ref[...] = _sigmoid(a0_ref[...] + jnp.dot(sa_scr[...], a2_ref[...], preferred_element_type=F32))
    gg_ref[...] = jnp.dot(sg_scr[...], g2_ref[...], preferred_element_type=F32)


def _rwkv_proj(x, sh, sc, shift_state, g, mu, w1, a1, g1, wr, wk, wv, w2, a2, g2, w0, a0, n_batch, seq_len, tm):
    m, d = x.shape
    tn = _tile(d, 256)
    shm, scm = _Mod(sh, seq_len, tm), _Mod(sc, seq_len, tm)
    stm = _Mod(shift_state, seq_len, tm)
    rows8 = tm // SUBLANES
    full = lambda a: pl.BlockSpec(a.shape, lambda i, j: (0,) * a.ndim)
    colw = lambda a: pl.BlockSpec((a.shape[0], tn), lambda i, j: (0, j))
    out_spec = pl.BlockSpec((tm, tn), lambda i, j: (i, j))
    g2d, w0r, a0r = g.reshape(1, d), w0.reshape(1, d), a0.reshape(1, d)
    return pl.pallas_call(
        functools.partial(_rwkv_proj_kernel, tm=tm, seq_len=seq_len),
        grid=(m // tm, d // tn),
        in_specs=[pl.BlockSpec((tm, d), lambda i, j: (i, 0)),
                  pl.BlockSpec((SUBLANES, d), lambda i, j: (jnp.maximum(i * rows8 - 1, 0), 0)),
                  shm.spec(), scm.spec(), stm.spec(), full(g2d), full(mu), full(w1), full(a1), full(g1),
                  colw(wr), colw(wk), colw(wv), colw(w2), colw(a2), colw(g2), colw(w0r), colw(a0r)],
        out_specs=[out_spec] * 6,
        out_shape=[jax.ShapeDtypeStruct((m, d), F32)] * 6,
        scratch_shapes=[pltpu.VMEM((6, tm, d), BF16), pltpu.VMEM((tm, w1.shape[1]), BF16),
                        pltpu.VMEM((tm, a1.shape[1]), BF16), pltpu.VMEM((tm, g1.shape[1]), BF16)],
        compiler_params=_cparams(("parallel", "arbitrary")),
        name="rwkv_proj",
    )(x, x, shm.arr, scm.arr, stm.arr, g2d, mu, w1, a1, g1, wr, wk, wv, w2, a2, g2, w0r, a0r)


def _norm_mod_kernel(x_ref, sh_ref, sc_ref, g_ref, o_ref):
    o_ref[...] = _rms(x_ref[...]) * g_ref[...] * (1.0 + sc_ref[...]) + sh_ref[...]


def _norm_mod(x, sh, sc, g):
    r, d = x.shape
    spec = pl.BlockSpec((r, d), lambda i: (0, 0))
    return pl.pallas_call(
        _norm_mod_kernel, grid=(1,),
        in_specs=[spec, spec, spec, pl.BlockSpec((1, d), lambda i: (0, 0))],
        out_specs=spec, out_shape=jax.ShapeDtypeStruct((r, d), F32),
        compiler_params=_cparams(("arbitrary",)), name="norm_mod",
    )(x, sh, sc, g.reshape(1, d))


def _wkv_kernel(r_ref, k_ref, v_ref, lw_ref, a_ref, g_ref, kk_ref, ka_ref, rk_ref, lnw_ref, lnb_ref, s0_ref,
                o_ref, sout_ref, z_scr, *, tt, chunk):
    t = pl.program_id(2)
    nt = pl.num_programs(2)

    @pl.when(t == 0)
    def _():
        z_scr[...] = s0_ref[...]

    c = chunk
    n2 = 2 * c
    n_chunks = max(tt // c, 1)
    lane1 = lax.broadcasted_iota(jnp.int32, (1, LANES), 1)
    head_a = lane1 < RW_HEAD
    rr = lax.broadcasted_iota(jnp.int32, (LANES, LANES), 0)
    cc = lax.broadcasted_iota(jnp.int32, (LANES, LANES), 1)
    seg_ones = ((rr // RW_HEAD) == (cc // RW_HEAD)).astype(BF16)
    row = lax.broadcasted_iota(jnp.int32, (n2, n2), 0)
    col = lax.broadcasted_iota(jnp.int32, (n2, n2), 1)
    same = (row // c) == (col // c)
    low_incl = same & (col <= row)
    low_strict = same & (col < row)
    tri = (lax.broadcasted_iota(jnp.int32, (c, c), 1) <= lax.broadcasted_iota(jnp.int32, (c, c), 0)).astype(BF16)
    n_pairs = r_ref.shape[1] // LANES

    def seg_sum(x):
        return jnp.dot(x.astype(BF16), seg_ones, preferred_element_type=F32)

    def stack(x):
        return jnp.concatenate([jnp.where(head_a, x, 0.0), jnp.where(head_a, 0.0, x)], axis=0)

    pt_s, rt_s, ch_s, kh_s, v_s, w_end, l_pc, l_pk, a_rc, a_rk = ([] for _ in range(10))
    bonus, gate = [], []
    for p in range(n_pairs):
        ls = slice(p * LANES, (p + 1) * LANES)

        def load(ref):
            x = ref[:, ls]
            if tt < c:
                x = jnp.concatenate([x, jnp.zeros((c - tt, LANES), F32)], axis=0)
            return x

        r_all, k_all, v_all, lw_all, a_all = (load(ref) for ref in (r_ref, k_ref, v_ref, lw_ref, a_ref))
        gate.append(load(g_ref))
        kk_raw = k_all * kk_ref[:, ls]
        kk_all = kk_raw * lax.rsqrt(seg_sum(kk_raw * kk_raw) + 1e-12)
        kmod_all = k_all * (1.0 + (a_all - 1.0) * ka_ref[:, ls])
        bonus.append(seg_sum(r_all * kmod_all * rk_ref[:, ls]) * v_all)
        lw_hi = lw_all.astype(BF16)
        lw_lo = (lw_all - lw_hi.astype(F32)).astype(BF16)
        for ci in range(n_chunks):
            sl = slice(ci * c, (ci + 1) * c)
            cum = (jnp.dot(tri, lw_hi[sl], preferred_element_type=F32)
                   + jnp.dot(tri, lw_lo[sl], preferred_element_type=F32))
            cum_end = cum[c - 1:c, :]
            e_neg = jnp.exp(-cum)
            e_end = jnp.exp(cum_end - cum)
            pvec = -kk_all[sl]
            cvec = kk_all[sl] * a_all[sl]
            p_s = stack(pvec * jnp.exp(cum - lw_all[sl]))
            r_s = stack(r_all[sl] * jnp.exp(cum))
            gram = _dot_nt(jnp.concatenate([p_s, r_s], axis=0),
                           jnp.concatenate([stack(cvec * e_neg), stack(kmod_all[sl] * e_neg)], axis=0))
            l_pc.append(jnp.where(low_strict, gram[0:n2, 0:n2], 0.0))
            l_pk.append(jnp.where(low_strict, gram[0:n2, n2:2 * n2], 0.0))
            a_rc.append(jnp.where(low_incl, gram[n2:2 * n2, 0:n2], 0.0))
            a_rk.append(jnp.where(low_incl, gram[n2:2 * n2, n2:2 * n2], 0.0))
            pt_s.append(p_s)
            rt_s.append(r_s)
            ch_s.append(stack(cvec * e_end))
            kh_s.append(stack(kmod_all[sl] * e_end))
            v_s.append(stack(v_all[sl]))
            w_end.append(jnp.exp(cum_end))
    t_inv = _unit_lower_inverses([-x for x in l_pc], c)
    rng = range(n_pairs * n_chunks)
    lv = [_dot(l_pk[i], v_s[i]) for i in rng]
    m1 = [_dot(t_inv[i], pt_s[i]) for i in rng]
    b1 = [_dot(t_inv[i], lv[i]) for i in rng]
    m2 = [rt_s[i] + _dot(a_rc[i], m1[i]) for i in rng]
    bv = [jnp.concatenate([b1[i], v_s[i]], axis=0) for i in rng]
    b2 = [_dot(jnp.concatenate([a_rc[i], a_rk[i]], axis=1), bv[i]) for i in rng]
    m3 = [_dot_tn(ch_s[i], m1[i]) for i in rng]
    b3 = [_dot_tn(bv[i], jnp.concatenate([ch_s[i], kh_s[i]], axis=0)) for i in rng]

    for p in range(n_pairs):
        ls = slice(p * LANES, (p + 1) * LANES)
        z = z_scr[p]
        ys = []
        for ci in range(n_chunks):
            i = p * n_chunks + ci
            y_s = _dot_nt(m2[i], z) + b2[i]
            z = z * w_end[i] + _dot_nt(z, m3[i]) + b3[i]
            ys.append(y_s[0:c] + y_s[c:n2])
        z_scr[p] = z
        y = jnp.concatenate(ys, axis=0) if n_chunks > 1 else ys[0]
        mean = seg_sum(y) * (1.0 / RW_HEAD)
        dev = y - mean
        var = seg_sum(dev * dev) * (1.0 / RW_HEAD)
        yn = dev * lax.rsqrt(var + RW_GN_EPS) * lnw_ref[:, ls] + lnb_ref[:, ls]
        out = (yn + bonus[p]) * gate[p]
        o_ref[:, ls] = out[0:tt].astype(o_ref.dtype)

    @pl.when(t == nt - 1)
    def _():
        sout_ref[...] = z_scr[...]


def _wkv7(r, k, v, lw, a, g, k_k, k_a, r_k, ln_w, ln_b, s0_pairs, n_batch, seq_len):
    m, d = r.shape
    n_pairs = d // LANES
    tt = _tile(seq_len, 512)
    nt = seq_len // tt
    chunk = min(RW_CHUNK, max(seq_len, SUBLANES))
    group = min(n_pairs, max(1, WKV_ITEMS_PER_STEP // max(tt // chunk, 1)))
    row_spec = pl.BlockSpec((tt, group * LANES), lambda b, hp, t: (b * nt + t, hp))
    vec_spec = pl.BlockSpec((1, group * LANES), lambda b, hp, t: (0, hp))
    st_spec = pl.BlockSpec((None, group, LANES, LANES), lambda b, hp, t: (b, hp, 0, 0))
    vec = lambda x: x.reshape(1, d)
    return pl.pallas_call(
        functools.partial(_wkv_kernel, tt=tt, chunk=chunk),
        grid=(n_batch, n_pairs // group, nt),
        in_specs=[row_spec] * 6 + [vec_spec] * 5 + [st_spec],
        out_specs=[row_spec, st_spec],
        out_shape=[jax.ShapeDtypeStruct((m, d), BF16), jax.ShapeDtypeStruct(s0_pairs.shape, F32)],
        scratch_shapes=[pltpu.VMEM((group, LANES, LANES), F32)],
        compiler_params=_cparams(("parallel", "parallel", "arbitrary")),
        name="wkv7",
    )(r, k, v, lw, a, g, vec(k_k), vec(k_a), vec(r_k), vec(ln_w), vec(ln_b), s0_pairs)


def _pair_states(s):
    b, h, n, _ = s.shape
    s2 = s.reshape(b, h // 2, 2, n, n)
    zero = jnp.zeros((b, h // 2, n, n), s.dtype)
    top = jnp.concatenate([s2[:, :, 0], zero], axis=-1)
    bot = jnp.concatenate([zero, s2[:, :, 1]], axis=-1)
    return jnp.concatenate([top, bot], axis=-2)


def _unpair_states(sp):
    n = RW_HEAD
    b, hp = sp.shape[:2]
    return jnp.stack([sp[:, :, :n, :n], sp[:, :, n:, n:]], axis=2).reshape(b, 2 * hp, n, n)


def _router_kernel(x_ref, sh_ref, sc_ref, g_ref, rw_ref, rb_ref, h_ref, idx_ref, gate_ref):
    h = _rms(x_ref[...]) * g_ref[...] * (1.0 + sc_ref[...]) + sh_ref[...]
    h_ref[...] = h
    logits = jnp.dot(h, rw_ref[...], preferred_element_type=F32, precision=lax.Precision.HIGHEST) + rb_ref[...]
    lane = lax.broadcasted_iota(jnp.int32, logits.shape, 1)
    lane_f = lane.astype(F32)
    neg = -jnp.inf
    l1 = jnp.where(lane < N_EXPERTS, logits, neg)
    m1 = jnp.max(l1, axis=-1, keepdims=True)
    i1 = jnp.min(jnp.where(l1 == m1, lane_f, float(LANES)), axis=-1, keepdims=True)
    l2 = jnp.where(lane_f == i1, neg, l1)
    m2 = jnp.max(l2, axis=-1, keepdims=True)
    i2 = jnp.min(jnp.where(l2 == m2, lane_f, float(LANES)), axis=-1, keepdims=True)
    e = jnp.exp(m2 - m1)
    g0 = 1.0 / (1.0 + e)
    g1 = e / (1.0 + e)
    idx_ref[...] = jnp.where(lane == 0, i1, jnp.where(lane == 1, i2, 0.0)).astype(jnp.int32)
    gate_ref[...] = jnp.where(lane == 0, g0, jnp.where(lane == 1, g1, 0.0))


def _router(x, sh, sc, g, rw_pad, rb_pad, seq_len, tm):
    m, d = x.shape
    shm, scm = _Mod(sh, seq_len, tm), _Mod(sc, seq_len, tm)
    return pl.pallas_call(
        _router_kernel,
        grid=(m // tm,),
        in_specs=[pl.BlockSpec((tm, d), lambda i: (i, 0)), shm.spec1(), scm.spec1(),
                  pl.BlockSpec((1, d), lambda i: (0, 0)),
                  pl.BlockSpec((d, LANES), lambda i: (0, 0)), pl.BlockSpec((1, LANES), lambda i: (0, 0))],
        out_specs=[pl.BlockSpec((tm, d), lambda i: (i, 0)), pl.BlockSpec((tm, LANES), lambda i: (i, 0)),
                   pl.BlockSpec((tm, LANES), lambda i: (i, 0))],
        out_shape=[jax.ShapeDtypeStruct((m, d), F32), jax.ShapeDtypeStruct((m, LANES), jnp.int32),
                   jax.ShapeDtypeStruct((m, LANES), F32)],
        compiler_params=_cparams(("parallel",)),
        name="moe_router",
    )(x, shm.arr, scm.arr, g.reshape(1, d), rw_pad, rb_pad)


def _gather_rows_kernel(idx_ref, src_ref, o_ref, sem, *, tm):
    i = pl.program_id(0)

    def row_copy(r):
        tok = idx_ref[i * tm + r]
        return pltpu.make_async_copy(src_ref.at[pl.ds(tok, 1), :], o_ref.at[pl.ds(r, 1), :], sem)

    def start(r, carry):
        row_copy(r).start()
        return carry

    def wait(r, carry):
        row_copy(r).wait()
        return carry

    lax.fori_loop(0, tm, start, 0)
    lax.fori_loop(0, tm, wait, 0)


def _gather_rows(src, row_idx, tm):
    n_rows = row_idx.shape[0]
    d = src.shape[1]
    grid_spec = pltpu.PrefetchScalarGridSpec(
        num_scalar_prefetch=1,
        grid=(n_rows // tm,),
        in_specs=[pl.BlockSpec(memory_space=pl.ANY)],
        out_specs=pl.BlockSpec((tm, d), lambda i, idx: (i, 0)),
        scratch_shapes=[pltpu.SemaphoreType.DMA(())],
    )
    return pl.pallas_call(
        functools.partial(_gather_rows_kernel, tm=tm),
        grid_spec=grid_spec,
        out_shape=jax.ShapeDtypeStruct((n_rows, d), src.dtype),
        compiler_params=_cparams(("arbitrary",)),
        name="moe_gather",
    )(row_idx, src)


def _moe_ffn_kernel(te_ref, na_ref, x_ref, wg_ref, wu_ref, wd_ref, o_ref, h_ref):
    i = pl.program_id(0)
    f = pl.program_id(1)
    active = i < na_ref[0]

    @pl.when(jnp.logical_and(active, f == 0))
    def _():
        h_ref[...] = x_ref[...].astype(BF16)

    @pl.when(jnp.logical_not(active))
    def _():
        o_ref[...] = jnp.zeros(o_ref.shape, F32)

    @pl.when(active)
    def _():
        h = h_ref[...]
        gate = jnp.dot(h, wg_ref[...], preferred_element_type=F32)
        up = jnp.dot(h, wu_ref[...], preferred_element_type=F32)
        act = (_silu(gate) * up).astype(BF16)
        part = jnp.dot(act, wd_ref[...], preferred_element_type=F32)

        @pl.when(f == 0)
        def _():
            o_ref[...] = part

        @pl.when(f > 0)
        def _():
            o_ref[...] += part


def _moe_ffn(xs, tile_expert, n_active, wg, wu, wd, tm):
    n_rows, d = xs.shape
    dff = wg.shape[2]
    tf = _tile(dff, 512)
    nf = dff // tf

    def fcol(i, f, te, na):
        return jnp.where(i < na[0], f, nf - 1)

    grid_spec = pltpu.PrefetchScalarGridSpec(
        num_scalar_prefetch=2,
        grid=(n_rows // tm, nf),
        in_specs=[pl.BlockSpec((tm, d), lambda i, f, te, na: (i, 0)),
                  pl.BlockSpec((None, d, tf), lambda i, f, te, na: (te[i], 0, fcol(i, f, te, na))),
                  pl.BlockSpec((None, d, tf), lambda i, f, te, na: (te[i], 0, fcol(i, f, te, na))),
                  pl.BlockSpec((None, tf, d), lambda i, f, te, na: (te[i], fcol(i, f, te, na), 0))],
        out_specs=pl.BlockSpec((tm, d), lambda i, f, te, na: (i, 0)),
        scratch_shapes=[pltpu.VMEM((tm, d), BF16)],
    )
    return pl.pallas_call(
        _moe_ffn_kernel,
        grid_spec=grid_spec,
        out_shape=jax.ShapeDtypeStruct((n_rows, d), F32),
        compiler_params=_cparams(("parallel", "arbitrary")),
        name="moe_ffn",
    )(tile_expert, n_active, xs, wg, wu, wd)


def _combine_kernel(dest_ref, x_ref, gt_ref, gate_ref, g_ref, yb_ref, o_ref, ybuf, sem, *, tm, tok0):
    i = pl.program_id(0)

    def row_copy(n):
        r = n // TOP_K
        k = n % TOP_K
        src = dest_ref[(tok0 + i * tm) * TOP_K + n]
        return pltpu.make_async_copy(yb_ref.at[pl.ds(src, 1), :], ybuf.at[k, pl.ds(r, 1), :], sem)

    def start(n, carry):
        row_copy(n).start()
        return carry

    def wait(n, carry):
        row_copy(n).wait()
        return carry

    lax.fori_loop(0, tm * TOP_K, start, 0)
    lax.fori_loop(0, tm * TOP_K, wait, 0)
    gates = gate_ref[...]
    y = gates[:, 0:1] * ybuf[0] + gates[:, 1:2] * ybuf[1]
    x = x_ref[...] + gt_ref[...] * y
    o_ref[...] = _rms(x) * g_ref[...]


def _combine_final(x, gt, gates, dest, yb, g, tok0, seq_len, tm):
    m, d = x.shape
    gtm = _Mod(gt, seq_len, tm)
    if gtm.per_batch:
        tpb = gtm.tiles_per_batch
        gt_spec = pl.BlockSpec((None, 1, d), lambda i, dst: (i // tpb, 0, 0))
    else:
        gt_spec = pl.BlockSpec((tm, d), lambda i, dst: (i, 0))
    grid_spec = pltpu.PrefetchScalarGridSpec(
        num_scalar_prefetch=1,
        grid=(m // tm,),
        in_specs=[pl.BlockSpec((tm, d), lambda i, dst: (i, 0)), gt_spec,
                  pl.BlockSpec((tm, LANES), lambda i, dst: (i, 0)),
                  pl.BlockSpec((1, d), lambda i, dst: (0, 0)),
                  pl.BlockSpec(memory_space=pl.ANY)],
        out_specs=pl.BlockSpec((tm, d), lambda i, dst: (i, 0)),
        scratch_shapes=[pltpu.VMEM((TOP_K, tm, d), F32), pltpu.SemaphoreType.DMA(())],
    )
    return pl.pallas_call(
        functools.partial(_combine_kernel, tm=tm, tok0=tok0),
        grid_spec=grid_spec,
        out_shape=jax.ShapeDtypeStruct((m, d), F32),
        compiler_params=_cparams(("arbitrary",)),
        name="moe_combine_final",
    )(dest, x, gtm.arr, gates, g.reshape(1, d), yb)


def _split_mod(mod):
    return [mod[:, n * D_MODEL:(n + 1) * D_MODEL] for n in range(6)]


def _layer_even(x, mod, pos, conv_state, gdn_state, paged, p, n_batch, seq_len, tm):
    sh1, sc1, gt1, sh2, sc2, gt2 = mod
    m = x.shape[0]
    if seq_len % tm == 0:
        tabs = _rope_tables(pos)
    else:
        tabs = tuple(jnp.tile(t, (n_batch, 1)) for t in _rope_tables(pos))
    gq, gk, gv, gz, ba = _norm_proj(x, sh1, sc1, p["norm_mix0"], p["w_gdn"], (False,) * 4, p["w_ba"], None,
                                    seq_len, tm)
    dq, dk, dv = _norm_proj(x, sh1, sc1, p["norm_mix0"], p["w_da"], (True, True, False), None, tabs, seq_len, tm)
    o_gdn, new_gdn = _gated_deltanet(gq, gk, gv, gz, ba, conv_state, gdn_state, p["gdn_conv_w"], p["gdn_a_log"],
                                     p["gdn_dt_bias"], p["gdn_norm_w"], n_batch, seq_len)
    lam_args = (p["da_lq1"], p["da_lk1"], p["da_lq2"], p["da_lk2"], p["da_subln_w"])
    if paged is None:
        o_da = _diff_attention_prompt(dq, dk, dv, *lam_args, n_batch, seq_len)
    else:
        o_da = _diff_attention_sample(dq, dk, dv, paged[0], paged[1], paged[2], *lam_args, n_batch, seq_len)
    x = _proj_residual(x, gt1, [o_gdn, o_da], p["w_out"], seq_len, tm)
    x = _norm_ffn(x, sh2, sc2, gt2, p["norm_ffn0"], p["ffn_wg"], p["ffn_wu"], p["ffn_wd"], seq_len, tm)
    raw = jnp.concatenate([gq.reshape(n_batch, seq_len, -1), gk.reshape(n_batch, seq_len, -1),
                           gv.reshape(n_batch, seq_len, -1)], axis=-1)
    ext = jnp.concatenate([conv_state, raw[:, max(seq_len - (GDN_CONV - 1), 0):]], axis=1)
    new_conv = ext[:, ext.shape[1] - (GDN_CONV - 1):]
    k_out = dk.reshape(n_batch, seq_len, DA_HEADS, 2 * DA_DK)
    v_out = dv.reshape(n_batch, seq_len, DA_HEADS, DA_DV)
    return x, k_out, v_out, new_conv, new_gdn


def _layer_odd_mix(x, mod, shift_state, wkv_state, p, n_batch, seq_len, tm):
    sh1, sc1, gt1 = mod[:3]
    d = x.shape[1]
    tm_rw = _tile(tm, 512)
    r, k, v, lw, a, g = _rwkv_proj(x, sh1, sc1, shift_state, p["norm_mix1"], p["rw_mu"], p["rw_w1"], p["rw_a1"],
                                   p["rw_g1"], p["rw_wr"], p["rw_wk"], p["rw_wv"], p["rw_w2"], p["rw_a2"],
                                   p["rw_g2"], p["rw_w0"], p["rw_a0"], n_batch, seq_len, tm_rw)
    yg, s_pairs = _wkv7(r, k, v, lw, a, g, p["rw_k_k"], p["rw_k_a"], p["rw_r_k"], p["rw_ln_w"], p["rw_ln_b"],
                        _pair_states(wkv_state), n_batch, seq_len)
    last = x.reshape(n_batch, seq_len, d)[:, seq_len - 1]
    new_shift = _norm_mod(last, sh1, sc1, p["norm_mix1"])
    x = _proj_residual(x, gt1, [yg], [p["rw_wo"]], seq_len, tm)
    return x, new_shift, _unpair_states(s_pairs)


def kernel(x_prompt, x_sample, cache_k, cache_v, state_gdn_conv, state_gdn, state_rwkv_shift, state_rwkv, page_table, c_prompt, c_sample, ada_w0, ada_b0, norm_mix0, w_in0, gdn_conv_w, gdn_a_log, gdn_dt_bias, gdn_norm_w, da_lq1, da_lk1, da_lq2, da_lk2, da_subln_w, w_out0, norm_ffn0, ffn_w_gate, ffn_w_up, ffn_w_down, ada_w1, ada_b1, norm_mix1, rw_mu, rw_w0, rw_w1, rw_w2, rw_a0, rw_a1, rw_a2, rw_g1, rw_g2, rw_k_k, rw_k_a, rw_r_k, rw_wr, rw_wk, rw_wv, rw_wo, rw_ln_w, rw_ln_b, norm_ffn1, moe_router_w, moe_router_b, moe_w_gate, moe_w_up, moe_w_down, norm_final):
    bp, lp, d = x_prompt.shape
    bs, ls, _ = x_sample.shape
    n_pages = page_table.shape[1]
    past_len = n_pages * PAGE_SIZE
    mp, ms = bp * lp, bs * ls
    tm_p = _tile(lp, 1024)
    tm_s = _tile(ms, 256)

    qk_w = GDN_HEADS * GDN_DK
    c0 = 3 * qk_w
    c1 = c0 + qk_w
    c2 = c1 + 2 * GDN_HEADS
    da_w = DA_HEADS * 2 * DA_DK
    bf = lambda w: w.astype(BF16)
    w_ba = jnp.concatenate([w_in0[:, c1:c2], jnp.zeros((d, LANES - 2 * GDN_HEADS), F32)], axis=1)
    lora_pad = lambda w, axis: jnp.pad(w, [(0, (-w.shape[a]) % LANES if a == axis else 0) for a in range(2)])
    p = {
        "norm_mix0": norm_mix0,
        "w_gdn": [bf(w_in0[:, n * qk_w:(n + 1) * qk_w]) for n in range(4)],
        "w_ba": bf(w_ba),
        "w_da": [bf(w_in0[:, c2 + n * da_w:c2 + (n + 1) * da_w]) for n in range(3)],
        "gdn_conv_w": gdn_conv_w, "gdn_a_log": gdn_a_log, "gdn_dt_bias": gdn_dt_bias, "gdn_norm_w": gdn_norm_w,
        "da_lq1": da_lq1, "da_lk1": da_lk1, "da_lq2": da_lq2, "da_lk2": da_lk2, "da_subln_w": da_subln_w,
        "w_out": [bf(w_out0[:qk_w]), bf(w_out0[qk_w:])],
        "norm_ffn0": norm_ffn0, "ffn_wg": bf(ffn_w_gate), "ffn_wu": bf(ffn_w_up), "ffn_wd": bf(ffn_w_down),
        "norm_mix1": norm_mix1, "rw_mu": rw_mu, "rw_w0": rw_w0, "rw_a0": rw_a0,
        "rw_w1": bf(lora_pad(rw_w1, 1)), "rw_w2": bf(lora_pad(rw_w2, 0)),
        "rw_a1": bf(lora_pad(rw_a1, 1)), "rw_a2": bf(lora_pad(rw_a2, 0)),
        "rw_g1": bf(rw_g1), "rw_g2": bf(rw_g2),
        "rw_wr": bf(rw_wr), "rw_wk": bf(rw_wk), "rw_wv": bf(rw_wv), "rw_wo": bf(rw_wo),
        "rw_k_k": rw_k_k, "rw_k_a": rw_k_a, "rw_r_k": rw_r_k.reshape(-1), "rw_ln_w": rw_ln_w, "rw_ln_b": rw_ln_b,
    }

    c_all = jnp.concatenate([c_prompt, c_sample], axis=0)
    mod0 = _ada_mod(c_all, ada_w0, ada_b0)
    mod1 = _ada_mod(c_all, ada_w1, ada_b1)
    mod0_p, mod0_s = _split_mod(mod0[:bp]), _split_mod(mod0[bp:])
    mod1_p, mod1_s = _split_mod(mod1[:bp]), _split_mod(mod1[bp:])

    xp = x_prompt.reshape(mp, d)
    xs = x_sample.reshape(ms, d)
    pos_p = jnp.arange(lp, dtype=jnp.int32)
    pos_s = past_len + jnp.arange(ls, dtype=jnp.int32)

    xp, k_p, v_p, conv_p, gdn_p = _layer_even(
        xp, mod0_p, pos_p, jnp.zeros((bp, GDN_CONV - 1, 3 * qk_w), F32),
        jnp.zeros((bp, GDN_HEADS, GDN_DK, GDN_DK), F32), None, p, bp, lp, tm_p)
    xs, k_s, v_s, conv_s, gdn_s = _layer_even(
        xs, mod0_s, pos_s, state_gdn_conv, state_gdn, (cache_k, cache_v, page_table), p, bs, ls, tm_s)

    n_rw_heads = d // RW_HEAD
    xp, shift_p, rw_p = _layer_odd_mix(xp, mod1_p, jnp.zeros((bp, d), F32),
                                       jnp.zeros((bp, n_rw_heads, RW_HEAD, RW_HEAD), F32), p, bp, lp, tm_p)
    xs, shift_s, rw_s = _layer_odd_mix(xs, mod1_s, state_rwkv_shift, state_rwkv, p, bs, ls, tm_s)

    rw_pad = jnp.concatenate([moe_router_w, jnp.zeros((d, LANES - N_EXPERTS), F32)], axis=1)
    rb_pad = jnp.concatenate([moe_router_b, jnp.zeros((LANES - N_EXPERTS,), F32)]).reshape(1, LANES)
    tm_r = _tile(lp, 512)
    h_p, idx_p, gate_p = _router(xp, mod1_p[3], mod1_p[4], norm_ffn1, rw_pad, rb_pad, lp, tm_r)
    h_s, idx_s, gate_s = _router(xs, mod1_s[3], mod1_s[4], norm_ffn1, rw_pad, rb_pad, ls, tm_s)
    h_all = jnp.concatenate([h_p, h_s], axis=0)
    flat_e = jnp.concatenate([idx_p[:, :TOP_K], idx_s[:, :TOP_K]], axis=0).reshape(-1)
    n_assign = flat_e.shape[0]
    tmoe = MOE_TILE
    onehot = (flat_e[:, None] == jnp.arange(N_EXPERTS, dtype=jnp.int32)[None, :]).astype(jnp.int32)
    running = jnp.cumsum(onehot, axis=0)
    pos_in_e = jnp.sum((running - onehot) * onehot, axis=1)
    counts = running[-1]
    padded = (counts + tmoe - 1) // tmoe * tmoe
    pad_end = jnp.cumsum(padded)
    pad_start = pad_end - padded
    dest = (pad_start[flat_e] + pos_in_e).astype(jnp.int32)
    n_tiles = -(-n_assign // tmoe) + N_EXPERTS
    n_rows = n_tiles * tmoe
    row_tok = jnp.zeros((n_rows,), jnp.int32).at[dest].set(jnp.arange(n_assign, dtype=jnp.int32) // TOP_K)
    n_active = (pad_end[-1] // tmoe).astype(jnp.int32)
    tile_ids = jnp.minimum(jnp.arange(n_tiles, dtype=jnp.int32), n_active - 1)
    tile_e = jnp.minimum(jnp.searchsorted(pad_end, tile_ids * tmoe, side="right"), N_EXPERTS - 1).astype(jnp.int32)
    x_sorted = _gather_rows(h_all, row_tok, tmoe)
    yb = _moe_ffn(x_sorted, tile_e, n_active.reshape(1), bf(moe_w_gate), bf(moe_w_up), bf(moe_w_down), tmoe)
    tm_c = _tile(lp, 256)
    y_p = _combine_final(xp, mod1_p[5], gate_p, dest, yb, norm_final, 0, lp, tm_c)
    y_s = _combine_final(xs, mod1_s[5], gate_s, dest, yb, norm_final, mp, ls, tm_s)

    return (y_p.reshape(bp, lp, d), y_s.reshape(bs, ls, d), k_p, v_p, k_s, v_s, conv_p, conv_s,
            gdn_p, gdn_s, shift_p, shift_s, rw_p, rw_s)
```

```python
import functools
import math

import jax
import jax.numpy as jnp
from jax import lax
from jax.experimental import pallas as pl
from jax.experimental.pallas import tpu as pltpu

F32 = jnp.float32
BF16 = jnp.bfloat16

D_MODEL = 2048
NORM_EPS = 1e-6
GDN_HEADS = 8
GDN_DK = 128
GDN_CONV = 4
GDN_CHUNK = 64
DA_HEADS = 8
DA_DK = 64
DA_DV = 128
ROPE_DIM = 16
ROPE_THETA = 500000.0
DA_LAMBDA_INIT = 0.8 - 0.6 * math.exp(-0.3 * 0)
RW_HEAD = 64
RW_CHUNK = 64
RW_GN_EPS = 1e-5 * RW_HEAD
N_EXPERTS = 8
TOP_K = 2
PAGE_SIZE = 128
LANES = 128
SUBLANES = 8
VMEM_LIMIT = 56 * 1024 * 1024
MOE_TILE = 512
FFN_TILE = 1024
GATHER_ROWS = 512
GATHER_UNROLL = 8
WKV_ITEMS_PER_STEP = 8
GDN_ITEMS_PER_STEP = 8


def _cparams(sem):
    return pltpu.CompilerParams(dimension_semantics=sem, vmem_limit_bytes=VMEM_LIMIT)


def _tile(n, pref):
    if n <= pref:
        return n
    t = pref
    while t >= SUBLANES:
        if n % t == 0 and t % SUBLANES == 0:
            return t
        t -= SUBLANES
    return n


def _dot(a, b):
    return jnp.dot(a.astype(BF16), b.astype(BF16), preferred_element_type=F32)


def _dot_nt(a, b):
    return lax.dot_general(a.astype(BF16), b.astype(BF16), (((1,), (1,)), ((), ())),
                           preferred_element_type=F32)


def _dot_tn(a, b):
    return lax.dot_general(a.astype(BF16), b.astype(BF16), (((0,), (0,)), ((), ())),
                           preferred_element_type=F32)


def _sigmoid(x):
    return 1.0 / (1.0 + jnp.exp(-x))


def _silu(x):
    return x * _sigmoid(x)


def _softplus(x):
    return jnp.maximum(x, 0.0) + jnp.log(1.0 + jnp.exp(-jnp.abs(x)))


def _rms(x, eps=NORM_EPS):
    return x * lax.rsqrt(jnp.mean(x * x, axis=-1, keepdims=True) + eps)


def _unit_lower_inverse(a, block):
    n = a.shape[0]
    row = lax.broadcasted_iota(jnp.int32, (n, n), 0)
    col = lax.broadcasted_iota(jnp.int32, (n, n), 1)
    inv = (row == col).astype(F32) - jnp.where((row // 2) == (col // 2), a, 0.0)
    s = 4
    while s <= block:
        a_s = jnp.where(((row // s) == (col // s)) & ((row // (s // 2)) != (col // (s // 2))), a, 0.0)
        inv = inv - _dot(_dot(inv, a_s), inv)
        s *= 2
    return inv


def _unit_lower_inverses(mats, block):
    n = mats[0].shape[0]
    row = lax.broadcasted_iota(jnp.int32, (n, n), 0)
    col = lax.broadcasted_iota(jnp.int32, (n, n), 1)
    eye = (row == col).astype(F32)
    pair = (row // 2) == (col // 2)
    invs = [eye - jnp.where(pair, a, 0.0) for a in mats]
    s = 4
    while s <= block:
        level = ((row // s) == (col // s)) & ((row // (s // 2)) != (col // (s // 2)))
        left = [_dot(inv, jnp.where(level, a, 0.0)) for inv, a in zip(invs, mats)]
        invs = [inv - _dot(lf, inv) for inv, lf in zip(invs, left)]
        s *= 2
    return invs


class _Mod:
    def __init__(self, m, seq_len, tm):
        self.width = m.shape[1]
        if seq_len % tm == 0:
            self.per_batch = True
            self.tiles_per_batch = seq_len // tm
            self.arr = m.reshape(m.shape[0], 1, self.width)
        else:
            assert tm % seq_len == 0
            self.per_batch = False
            self.arr = jnp.repeat(m, seq_len, axis=0)
        self.tm = tm

    def spec(self, tn=None, col=None):
        tn = self.width if tn is None else tn
        col = (lambda i, j: 0) if col is None else col
        if self.per_batch:
            tpb = self.tiles_per_batch
            return pl.BlockSpec((None, 1, tn), lambda i, j: (i // tpb, 0, col(i, j)))
        return pl.BlockSpec((self.tm, tn), lambda i, j: (i, col(i, j)))

    def spec1(self):
        if self.per_batch:
            tpb = self.tiles_per_batch
            return pl.BlockSpec((None, 1, self.width), lambda i: (i // tpb, 0, 0))
        return pl.BlockSpec((self.tm, self.width), lambda i: (i, 0))


def _ada_kernel(c_ref, w_ref, b_ref, o_ref):
    a = _silu(c_ref[...])
    o_ref[...] = _dot(a, w_ref[...]) + b_ref[...]


def _ada_mod(c, w, b):
    r, d = c.shape
    n = w.shape[1]
    tn = _tile(n, 1024)
    return pl.pallas_call(
        _ada_kernel,
        grid=(n // tn,),
        in_specs=[pl.BlockSpec((r, d), lambda j: (0, 0)),
                  pl.BlockSpec((d, tn), lambda j: (0, j)),
                  pl.BlockSpec((1, tn), lambda j: (0, j))],
        out_specs=pl.BlockSpec((r, tn), lambda j: (0, j)),
        out_shape=jax.ShapeDtypeStruct((r, n), F32),
        compiler_params=_cparams(("parallel",)),
        name="ada_mod",
    )(c, w, b.reshape(1, n))


def _rope_tile(acc, cos, sin_lo, sin_hi):
    fwd = pltpu.roll(acc, LANES - ROPE_DIM // 2, 1)
    bwd = pltpu.roll(acc, ROPE_DIM // 2, 1)
    return acc * cos + fwd * sin_lo + bwd * sin_hi


def _norm_proj_kernel(*refs, n_w, rope, has_extra, tn):
    x_ref, sh_ref, sc_ref, g_ref = refs[:4]
    pos = 4
    w_refs = refs[pos:pos + n_w]
    pos += n_w
    if has_extra:
        we_ref = refs[pos]
        pos += 1
    if any(rope):
        cos_ref, slo_ref, shi_ref = refs[pos:pos + 3]
        pos += 3
    o_refs = refs[pos:pos + n_w]
    pos += n_w
    if has_extra:
        oe_ref = refs[pos]
        pos += 1
    h_ref = refs[pos]
    j = pl.program_id(1)

    @pl.when(j == 0)
    def _():
        h = _rms(x_ref[...]) * g_ref[...] * (1.0 + sc_ref[...]) + sh_ref[...]
        h_ref[...] = h.astype(BF16)
        if has_extra:
            oe_ref[...] = jnp.dot(h_ref[...], we_ref[...], preferred_element_type=F32)

    h = h_ref[...]
    for k in range(n_w):
        acc = jnp.dot(h, w_refs[k][...], preferred_element_type=F32)
        if rope[k]:
            cos, slo, shi = cos_ref[...], slo_ref[...], shi_ref[...]
            for c in range(tn // LANES):
                sl = slice(c * LANES, (c + 1) * LANES)
                o_refs[k][:, sl] = _rope_tile(acc[:, sl], cos, slo, shi)
        else:
            o_refs[k][...] = acc


def _norm_proj(x, sh, sc, g, ws, rope, extra_w, rope_tabs, seq_len, tm):
    m, d = x.shape
    n = ws[0].shape[1]
    tn = _tile(n, 256)
    n_w = len(ws)
    shm, scm = _Mod(sh, seq_len, tm), _Mod(sc, seq_len, tm)
    in_specs = [pl.BlockSpec((tm, d), lambda i, j: (i, 0)), shm.spec(), scm.spec(),
                pl.BlockSpec((1, d), lambda i, j: (0, 0))]
    args = [x, shm.arr, scm.arr, g.reshape(1, d)]
    for w in ws:
        in_specs.append(pl.BlockSpec((d, tn), lambda i, j: (0, j)))
        args.append(w)
    has_extra = extra_w is not None
    if has_extra:
        in_specs.append(pl.BlockSpec((d, LANES), lambda i, j: (0, 0)))
        args.append(extra_w)
    if any(rope):
        if seq_len % tm == 0:
            tpb = seq_len // tm
            tab_spec = pl.BlockSpec((tm, LANES), lambda i, j: (i % tpb, 0))
        else:
            tab_spec = pl.BlockSpec((tm, LANES), lambda i, j: (i, 0))
        for t in rope_tabs:
            in_specs.append(tab_spec)
            args.append(t)
    out_specs = [pl.BlockSpec((tm, tn), lambda i, j: (i, j)) for _ in ws]
    out_shape = [jax.ShapeDtypeStruct((m, n), F32) for _ in ws]
    if has_extra:
        out_specs.append(pl.BlockSpec((tm, LANES), lambda i, j: (i, 0)))
        out_shape.append(jax.ShapeDtypeStruct((m, LANES), F32))
    return pl.pallas_call(
        functools.partial(_norm_proj_kernel, n_w=n_w, rope=tuple(rope), has_extra=has_extra, tn=tn),
        grid=(m // tm, n // tn),
        in_specs=in_specs,
        out_specs=out_specs,
        out_shape=out_shape,
        scratch_shapes=[pltpu.VMEM((tm, d), BF16)],
        compiler_params=_cparams(("parallel", "arbitrary")),
        name="norm_proj",
    )(*args)


def _rope_tables(pos):
    half = ROPE_DIM // 2
    inv_freq = ROPE_THETA ** (-jnp.arange(half, dtype=F32) * (2.0 / ROPE_DIM))
    ang = pos.astype(F32)[:, None] * inv_freq
    cos, sin = jnp.cos(ang), jnp.sin(ang)
    n = pos.shape[0]
    ones = jnp.ones((n, DA_DK - ROPE_DIM), F32)
    zeros = jnp.zeros((n, DA_DK - ROPE_DIM), F32)
    zh = jnp.zeros((n, half), F32)
    cos_m = jnp.concatenate([cos, cos, ones], axis=1)
    slo_m = jnp.concatenate([-sin, zh, zeros], axis=1)
    shi_m = jnp.concatenate([zh, sin, zeros], axis=1)
    return tuple(jnp.concatenate([t, t], axis=1) for t in (cos_m, slo_m, shi_m))


def _gdn_kernel(q_ref, k_ref, v_ref, z_ref, ba_ref, csq_ref, csk_ref, csv_ref, cwq_ref, cwk_ref, cwv_ref,
                s0_ref, alog_ref, dtb_ref, nw_ref, o_ref, sout_ref, s_scr, hist_scr, ext_scr, *, tt, chunk):
    hp = pl.program_id(1)
    t = pl.program_id(2)
    nt = pl.num_programs(2)
    n_hist = SUBLANES

    @pl.when(t == 0)
    def _():
        s_scr[...] = s0_ref[...]
        hist_scr[0] = csq_ref[...]
        hist_scr[1] = csk_ref[...]
        hist_scr[2] = csv_ref[...]

    conv = []
    for s, (raw_ref, cw_ref) in enumerate(((q_ref, cwq_ref), (k_ref, cwk_ref), (v_ref, cwv_ref))):
        raw = raw_ref[...]
        ext_scr[s, 0:n_hist, :] = hist_scr[s]
        ext_scr[s, n_hist:n_hist + tt, :] = raw
        cw = cw_ref[...]
        y = raw * cw[GDN_CONV - 1:GDN_CONV, :]
        for dly in range(1, GDN_CONV):
            y = y + ext_scr[s, n_hist - dly:n_hist - dly + tt, :] * cw[GDN_CONV - 1 - dly:GDN_CONV - dly, :]
        hist_scr[s] = raw[tt - n_hist:tt, :]
        conv.append(_silu(y))
    q_all, k_all, v_all = conv
    z_all = z_ref[...]

    ba = ba_ref[...]
    lane = lax.broadcasted_iota(jnp.int32, ba.shape, 1)
    gates = jnp.where(lane < GDN_HEADS, _sigmoid(ba), -jnp.exp(alog_ref[...]) * _softplus(ba + dtb_ref[...]))
    n_heads = q_ref.shape[1] // LANES
    n_pairs = n_heads // 2
    beta_cols, g_cols = [], []
    for hq in range(n_heads):
        hh = n_heads * hp + hq
        beta_cols.append(jnp.sum(jnp.where(lane == hh, gates, 0.0), axis=1, keepdims=True))
        g_cols.append(jnp.sum(jnp.where(lane == GDN_HEADS + hh, gates, 0.0), axis=1, keepdims=True))

    c = chunk
    n2 = 2 * c
    row = lax.broadcasted_iota(jnp.int32, (n2, n2), 0)
    col = lax.broadcasted_iota(jnp.int32, (n2, n2), 1)
    same = (row // c) == (col // c)
    low_incl = same & (col <= row)
    low_strict = same & (col < row)
    up_incl = same & (row <= col)
    eye = row == col
    nw = nw_ref[...]

    def stack(x_all, r0, pq):
        return jnp.concatenate([x_all[r0:r0 + c, (2 * pq + hl) * LANES:(2 * pq + hl + 1) * LANES]
                                for hl in range(2)], axis=0)

    def stack_col(cols, r0, pq):
        return jnp.concatenate([cols[2 * pq + hl][r0:r0 + c, :] for hl in range(2)], axis=0)

    n_chunks = tt // c
    items = [(pq, ci) for pq in range(n_pairs) for ci in range(n_chunks)]
    rng = range(len(items))
    a_mats, qks, kes, bvs, qes, kds, s_decay = ([] for _ in range(7))
    for pq, ci in items:
        r0 = ci * c
        q = stack(q_all, r0, pq)
        k = stack(k_all, r0, pq)
        beta = stack_col(beta_cols, r0, pq)
        g = stack_col(g_cols, r0, pq)
        q = q * lax.rsqrt(jnp.sum(q * q, axis=-1, keepdims=True) + 1e-12) * (GDN_DK ** -0.5)
        k = k * lax.rsqrt(jnp.sum(k * k, axis=-1, keepdims=True) + 1e-12)
        g_row = jnp.sum(jnp.where(eye, g, 0.0), axis=0, keepdims=True)
        gc_col = jnp.sum(jnp.where(low_incl, g_row, 0.0), axis=1, keepdims=True)
        gc_row = jnp.sum(jnp.where(up_incl, g, 0.0), axis=0, keepdims=True)
        decay = jnp.where(low_incl, jnp.exp(jnp.where(low_incl, gc_col - gc_row, 0.0)), 0.0)
        gram = _dot_nt(jnp.concatenate([k * beta, q], axis=0), k)
        a_mats.append(jnp.where(low_strict, gram[0:n2] * decay, 0.0))
        qks.append(jnp.where(low_incl, gram[n2:2 * n2] * decay, 0.0))
        egc = jnp.exp(gc_col)
        kes.append(k * (beta * egc))
        bvs.append(stack(v_all, r0, pq) * beta)
        qes.append(q * egc)
        g_last = [gc_col[(hl + 1) * c - 1:(hl + 1) * c, :] for hl in range(2)]
        kds.append([k[hl * c:(hl + 1) * c] * jnp.exp(g_last[hl] - gc_col[hl * c:(hl + 1) * c]) for hl in range(2)])
        s_decay.append([jnp.exp(g_last[hl]) for hl in range(2)])
    t_inv = _unit_lower_inverses(a_mats, c)
    mb1 = [_dot(t_inv[i], jnp.concatenate([kes[i], bvs[i]], axis=1)) for i in rng]
    mb2 = [_dot(qks[i], mb1[i]) for i in rng]
    m2 = [qes[i] - mb2[i][:, 0:LANES] for i in rng]
    b2 = [mb2[i][:, LANES:2 * LANES] for i in rng]
    mb3 = [[_dot_tn(kds[i][hl], mb1[i][hl * c:(hl + 1) * c]) for hl in range(2)] for i in rng]

    outs = []
    for pq in range(n_pairs):
        s = [s_scr[2 * pq], s_scr[2 * pq + 1]]
        for ci in range(n_chunks):
            i = pq * n_chunks + ci
            s_cat = jnp.concatenate(s, axis=1).astype(BF16)
            o_wide = jnp.dot(m2[i].astype(BF16), s_cat, preferred_element_type=F32)
            outs.append(jnp.concatenate([o_wide[0:c, 0:LANES], o_wide[c:n2, LANES:2 * LANES]], axis=0) + b2[i])
            s = [s[hl] * s_decay[i][hl] - _dot(mb3[i][hl][:, 0:LANES], s[hl]) + mb3[i][hl][:, LANES:2 * LANES]
                 for hl in range(2)]
        s_scr[2 * pq] = s[0]
        s_scr[2 * pq + 1] = s[1]

    for i, (pq, ci) in enumerate(items):
        r0 = ci * c
        o = _rms(outs[i]) * nw * _silu(stack(z_all, r0, pq))
        for hl in range(2):
            lanes = slice((2 * pq + hl) * LANES, (2 * pq + hl + 1) * LANES)
            o_ref[r0:r0 + c, lanes] = o[hl * c:(hl + 1) * c].astype(o_ref.dtype)

    @pl.when(t == nt - 1)
    def _():
        sout_ref[...] = s_scr[...]


def _gated_deltanet(qraw, kraw, vraw, z, ba, conv_state, s0, conv_w, a_log, dt_bias, norm_w, n_batch, seq_len):
    m = qraw.shape[0]
    tt = _tile(seq_len, 256)
    nt = seq_len // tt
    chunk = min(GDN_CHUNK, tt)
    assert tt % chunk == 0
    n_hist = SUBLANES
    pairs = min(GDN_HEADS // 2, max(1, GDN_ITEMS_PER_STEP // (tt // chunk)))
    w2 = 2 * pairs * LANES
    qkv_w = 3 * GDN_HEADS * GDN_DK
    cs = jnp.concatenate([jnp.zeros((n_batch, n_hist - (GDN_CONV - 1), qkv_w), F32), conv_state], axis=1)
    cw = jnp.concatenate([conv_w, jnp.zeros((n_hist - GDN_CONV, qkv_w), F32)], axis=0)
    alog = jnp.zeros((1, LANES), F32).at[0, GDN_HEADS:2 * GDN_HEADS].set(a_log)
    dtb = jnp.zeros((1, LANES), F32).at[0, GDN_HEADS:2 * GDN_HEADS].set(dt_bias)
    nblk = GDN_HEADS // (2 * pairs)
    row_spec = pl.BlockSpec((tt, w2), lambda b, hp, t: (b * nt + t, hp))
    cs_specs = [pl.BlockSpec((None, n_hist, w2), functools.partial(lambda b, hp, t, s: (b, 0, s * nblk + hp), s=s))
                for s in range(3)]
    cw_specs = [pl.BlockSpec((n_hist, w2), functools.partial(lambda b, hp, t, s: (0, s * nblk + hp), s=s))
                for s in range(3)]
    st_spec = pl.BlockSpec((None, 2 * pairs, GDN_DK, GDN_DK), lambda b, hp, t: (b, hp, 0, 0))
    vec_spec = pl.BlockSpec((1, LANES), lambda b, hp, t: (0, 0))
    return pl.pallas_call(
        functools.partial(_gdn_kernel, tt=tt, chunk=chunk),
        grid=(n_batch, nblk, nt),
        in_specs=[row_spec, row_spec, row_spec, row_spec,
                  pl.BlockSpec((tt, LANES), lambda b, hp, t: (b * nt + t, 0)),
                  *cs_specs, *cw_specs, st_spec, vec_spec, vec_spec, vec_spec],
        out_specs=[row_spec, st_spec],
        out_shape=[jax.ShapeDtypeStruct((m, GDN_HEADS * GDN_DK), BF16),
                   jax.ShapeDtypeStruct(s0.shape, F32)],
        scratch_shapes=[pltpu.VMEM((2 * pairs, GDN_DK, GDN_DK), F32),
                        pltpu.VMEM((3, n_hist, w2), F32),
                        pltpu.VMEM((3, n_hist + tt, w2), F32)],
        compiler_params=_cparams(("parallel", "parallel", "arbitrary")),
        name="gated_deltanet",
    )(qraw, kraw, vraw, z, ba, cs, cs, cs, cw, cw, cw, s0, alog, dtb, norm_w.reshape(1, LANES))


def _lambda(lq1_ref, lk1_ref, lq2_ref, lk2_ref):
    s1 = jnp.sum(lq1_ref[...] * lk1_ref[...], axis=-1, keepdims=True)
    s2 = jnp.sum(lq2_ref[...] * lk2_ref[...], axis=-1, keepdims=True)
    return jnp.exp(s1) - jnp.exp(s2) + DA_LAMBDA_INIT


def _flash_kernel(q_ref, k_ref, v_ref, lq1_ref, lk1_ref, lq2_ref, lk2_ref, sw_ref, o_ref,
                  *, tq, tk):
    qi = pl.program_id(2)
    q = q_ref[...] * (DA_DK ** -0.5)
    lane = lax.broadcasted_iota(jnp.int32, q.shape, 1)
    q_maps = (jnp.where(lane < DA_DK, q, 0.0).astype(BF16), jnp.where(lane >= DA_DK, q, 0.0).astype(BF16))

    def block(ki, carry, diagonal):
        start = pl.multiple_of(ki * tk, tk)
        k = k_ref[pl.ds(start, tk), :].astype(BF16)
        v = v_ref[pl.ds(start, tk), :].astype(BF16)
        if diagonal:
            visible = (lax.broadcasted_iota(jnp.int32, (tq, tk), 1) <= lax.broadcasted_iota(jnp.int32, (tq, tk), 0))
        new = []
        for mp in range(2):
            m_old, l_old, acc_old = carry[mp]
            s = lax.dot_general(q_maps[mp], k, (((1,), (1,)), ((), ())), preferred_element_type=F32)
            if diagonal:
                s = jnp.where(visible, s, -jnp.inf)
            m_new = jnp.maximum(m_old, jnp.max(s, axis=-1, keepdims=True))
            alpha = jnp.exp(m_old - m_new)
            p = jnp.exp(s - m_new)
            l_new = alpha * l_old + jnp.sum(p, axis=-1, keepdims=True)
            acc_new = alpha * acc_old + jnp.dot(p.astype(BF16), v, preferred_element_type=F32)
            new.append((m_new, l_new, acc_new))
        return tuple(new)

    init = tuple((jnp.full((tq, 1), -jnp.inf, F32), jnp.zeros((tq, 1), F32), jnp.zeros((tq, LANES), F32))
                 for _ in range(2))
    carry = lax.fori_loop(0, qi, lambda ki, c: block(ki, c, False), init)
    (_, l0, acc0), (_, l1, acc1) = block(qi, carry, True)
    lam = _lambda(lq1_ref, lk1_ref, lq2_ref, lk2_ref)
    o = acc0 / l0 - lam * (acc1 / l1)
    o = _rms(o) * sw_ref[...] * (1.0 - DA_LAMBDA_INIT)
    o_ref[...] = o.astype(o_ref.dtype)


def _diff_attention_prompt(q, k, v, lq1, lk1, lq2, lk2, subln_w, n_batch, seq_len):
    m = q.shape[0]
    tq = _tile(seq_len, 512)
    tk = tq
    nq = seq_len // tq
    vec = lambda a: a.reshape(1, -1)
    vspec = lambda n: pl.BlockSpec((1, n), lambda b, h, qi: (0, 0))
    kv_spec = pl.BlockSpec((seq_len, LANES), lambda b, h, qi: (b, h))
    return pl.pallas_call(
        functools.partial(_flash_kernel, tq=tq, tk=tk),
        grid=(n_batch, DA_HEADS, nq),
        in_specs=[pl.BlockSpec((tq, LANES), lambda b, h, qi: (b * nq + qi, h)), kv_spec, kv_spec,
                  vspec(DA_DK), vspec(DA_DK), vspec(DA_DK), vspec(DA_DK), vspec(DA_DV)],
        out_specs=pl.BlockSpec((tq, LANES), lambda b, h, qi: (b * nq + qi, h)),
        out_shape=jax.ShapeDtypeStruct((m, DA_HEADS * DA_DV), BF16),
        compiler_params=_cparams(("parallel", "parallel", "arbitrary")),
        name="diff_attn_prompt",
    )(q, k, v, vec(lq1), vec(lk1), vec(lq2), vec(lk2), vec(subln_w))


def _paged_attn_kernel(pt_ref, q_ref, kn_ref, vn_ref, *refs, n_group, ls):
    k_refs = refs[:n_group]
    v_refs = refs[n_group:2 * n_group]
    lq1_ref, lk1_ref, lq2_ref, lk2_ref, sw_ref, o_ref, qs_scr, m_scr, l_scr, acc_scr = refs[2 * n_group:]
    p_idx = pl.program_id(1)
    n_steps = pl.num_programs(1)
    n_rows = 2 * ls

    @pl.when(p_idx == 0)
    def _():
        q = q_ref[...] * (DA_DK ** -0.5)
        q2 = jnp.concatenate([q, q], axis=0)
        r = lax.broadcasted_iota(jnp.int32, q2.shape, 0)
        lane = lax.broadcasted_iota(jnp.int32, q2.shape, 1)
        keep = (r // ls) == ((lane % LANES) // DA_DK)
        qs_scr[...] = jnp.where(keep, q2, 0.0).astype(BF16)
        m_scr[...] = jnp.full(m_scr.shape, -jnp.inf, F32)
        l_scr[...] = jnp.zeros(l_scr.shape, F32)
        acc_scr[...] = jnp.zeros(acc_scr.shape, F32)

    def update(k_heads, v_heads, visible):
        s = jnp.concatenate(
            [lax.dot_general(qs_scr[:, h * LANES:(h + 1) * LANES], k_heads[h], (((1,), (1,)), ((), ())),
                             preferred_element_type=F32) for h in range(DA_HEADS)], axis=0)
        if visible is not None:
            s = jnp.where(visible, s, -jnp.inf)
        m_old = m_scr[...]
        m_new = jnp.maximum(m_old, jnp.max(s, axis=-1, keepdims=True))
        alpha = jnp.exp(m_old - m_new)
        p = jnp.exp(s - m_new)
        l_scr[...] = alpha * l_scr[...] + jnp.sum(p, axis=-1, keepdims=True)
        p = p.astype(BF16)
        pv = jnp.concatenate([jnp.dot(p[h * n_rows:(h + 1) * n_rows], v_heads[h], preferred_element_type=F32)
                              for h in range(DA_HEADS)], axis=0)
        acc_scr[...] = alpha * acc_scr[...] + pv
        m_scr[...] = m_new

    update([jnp.concatenate([kr[pl.ds(h, PAGE_SIZE, stride=DA_HEADS), :].astype(BF16) for kr in k_refs], axis=0)
            for h in range(DA_HEADS)],
           [jnp.concatenate([vr[pl.ds(h, PAGE_SIZE, stride=DA_HEADS), :].astype(BF16) for vr in v_refs], axis=0)
            for h in range(DA_HEADS)], None)

    @pl.when(p_idx == n_steps - 1)
    def _():
        lam = _lambda(lq1_ref, lk1_ref, lq2_ref, lk2_ref)
        pad = jnp.zeros((LANES - ls, DA_HEADS * LANES), F32)
        kn = jnp.concatenate([kn_ref[...], pad], axis=0).astype(BF16)
        vn = jnp.concatenate([vn_ref[...], pad], axis=0).astype(BF16)
        r = lax.broadcasted_iota(jnp.int32, (DA_HEADS * n_rows, LANES), 0)
        c = lax.broadcasted_iota(jnp.int32, (DA_HEADS * n_rows, LANES), 1)
        update([kn[:, h * LANES:(h + 1) * LANES] for h in range(DA_HEADS)],
               [vn[:, h * LANES:(h + 1) * LANES] for h in range(DA_HEADS)], c <= (r % ls))
        acc = acc_scr[...] / l_scr[...]
        for h in range(DA_HEADS):
            o = acc[h * n_rows:h * n_rows + ls] - lam * acc[h * n_rows + ls:(h + 1) * n_rows]
            o = _rms(o) * sw_ref[...] * (1.0 - DA_LAMBDA_INIT)
            o_ref[:, h * LANES:(h + 1) * LANES] = o.astype(o_ref.dtype)


def _diff_attention_sample(q, k_new, v_new, cache_k, cache_v, page_table, lq1, lk1, lq2, lk2, subln_w, n_batch, ls):
    n_pool = cache_k.shape[0]
    width = DA_HEADS * LANES
    ck = cache_k.reshape(n_pool, PAGE_SIZE * DA_HEADS, LANES)
    cv = cache_v.reshape(n_pool, PAGE_SIZE * DA_HEADS, LANES)
    n_pages = page_table.shape[1]
    n_group = 4 if n_pages % 4 == 0 else (2 if n_pages % 2 == 0 else 1)
    n_steps = n_pages // n_group
    row_spec = pl.BlockSpec((ls, width), lambda b, p, pt: (b, 0))
    page_specs = [pl.BlockSpec((None, PAGE_SIZE * DA_HEADS, LANES),
                               functools.partial(lambda b, p, pt, g: (pt[b, p * n_group + g], 0, 0), g=g))
                  for g in range(n_group)]
    vec = lambda a: a.reshape(1, -1)
    vspec = lambda n: pl.BlockSpec((1, n), lambda b, p, pt: (0, 0))
    grid_spec = pltpu.PrefetchScalarGridSpec(
        num_scalar_prefetch=1,
        grid=(n_batch, n_steps),
        in_specs=[row_spec, row_spec, row_spec, *page_specs, *page_specs,
                  vspec(DA_DK), vspec(DA_DK), vspec(DA_DK), vspec(DA_DK), vspec(DA_DV)],
        out_specs=row_spec,
        scratch_shapes=[pltpu.VMEM((2 * ls, width), BF16), pltpu.VMEM((DA_HEADS * 2 * ls, 1), F32),
                        pltpu.VMEM((DA_HEADS * 2 * ls, 1), F32), pltpu.VMEM((DA_HEADS * 2 * ls, LANES), F32)],
    )
    return pl.pallas_call(
        functools.partial(_paged_attn_kernel, n_group=n_group, ls=ls),
        grid_spec=grid_spec,
        out_shape=jax.ShapeDtypeStruct((n_batch * ls, width), BF16),
        compiler_params=_cparams(("parallel", "arbitrary")),
        name="diff_attn_sample",
    )(page_table, q, k_new, v_new, *([ck] * n_group), *([cv] * n_group),
      vec(lq1), vec(lk1), vec(lq2), vec(lk2), vec(subln_w))


def _proj_res_kernel(*refs, n_parts):
    x_ref, gt_ref = refs[:2]
    a_refs = refs[2:2 + n_parts]
    w_refs = refs[2 + n_parts:2 + 2 * n_parts]
    o_ref = refs[2 + 2 * n_parts]
    acc = jnp.dot(a_refs[0][...], w_refs[0][...], preferred_element_type=F32)
    for p in range(1, n_parts):
        acc = acc + jnp.dot(a_refs[p][...], w_refs[p][...], preferred_element_type=F32)
    o_ref[...] = x_ref[...] + gt_ref[...] * acc


def _proj_residual(x, gt, parts, ws, seq_len, tm):
    m, n = x.shape
    tn = _tile(n, 512)
    gtm = _Mod(gt, seq_len, tm)
    in_specs = [pl.BlockSpec((tm, tn), lambda i, j: (i, j)), gtm.spec(tn, lambda i, j: j)]
    for a in parts:
        in_specs.append(pl.BlockSpec((tm, a.shape[1]), lambda i, j: (i, 0)))
    for w in ws:
        in_specs.append(pl.BlockSpec((w.shape[0], tn), lambda i, j: (0, j)))
    return pl.pallas_call(
        functools.partial(_proj_res_kernel, n_parts=len(parts)),
        grid=(m // tm, n // tn),
        in_specs=in_specs,
        out_specs=pl.BlockSpec((tm, tn), lambda i, j: (i, j)),
        out_shape=jax.ShapeDtypeStruct((m, n), F32),
        compiler_params=_cparams(("parallel", "arbitrary")),
        name="proj_residual",
    )(x, gtm.arr, *parts, *ws)


def _norm_ffn_kernel(x_ref, sh_ref, sc_ref, gt_ref, g_ref, wg_ref, wu_ref, wd_ref, o_ref, h_ref):
    f = pl.program_id(1)
    nf = pl.num_programs(1)

    @pl.when(f == 0)
    def _():
        h = _rms(x_ref[...]) * g_ref[...] * (1.0 + sc_ref[...]) + sh_ref[...]
        h_ref[...] = h.astype(BF16)
        o_ref[...] = jnp.zeros(o_ref.shape, F32)

    h = h_ref[...]
    gate = jnp.dot(h, wg_ref[...], preferred_element_type=F32)
    up = jnp.dot(h, wu_ref[...], preferred_element_type=F32)
    act = (_silu(gate) * up).astype(BF16)
    o_ref[...] += jnp.dot(act, wd_ref[...], preferred_element_type=F32)

    @pl.when(f == nf - 1)
    def _():
        o_ref[...] = x_ref[...] + gt_ref[...] * o_ref[...]


def _norm_ffn(x, sh, sc, gt, g, wg, wu, wd, seq_len, tm):
    m, d = x.shape
    dff = wg.shape[1]
    tf = _tile(dff, FFN_TILE)
    tm = _tile(tm, 512)
    shm, scm, gtm = _Mod(sh, seq_len, tm), _Mod(sc, seq_len, tm), _Mod(gt, seq_len, tm)
    return pl.pallas_call(
        _norm_ffn_kernel,
        grid=(m // tm, dff // tf),
        in_specs=[pl.BlockSpec((tm, d), lambda i, f: (i, 0)), shm.spec(), scm.spec(), gtm.spec(),
                  pl.BlockSpec((1, d), lambda i, f: (0, 0)),
                  pl.BlockSpec((d, tf), lambda i, f: (0, f)),
                  pl.BlockSpec((d, tf), lambda i, f: (0, f)),
                  pl.BlockSpec((tf, d), lambda i, f: (f, 0))],
        out_specs=pl.BlockSpec((tm, d), lambda i, f: (i, 0)),
        out_shape=jax.ShapeDtypeStruct((m, d), F32),
        scratch_shapes=[pltpu.VMEM((tm, d), BF16)],
        compiler_params=_cparams(("parallel", "arbitrary")),
        name="norm_ffn",
    )(x, shm.arr, scm.arr, gtm.arr, g.reshape(1, d), wg, wu, wd)


def _rwkv_proj_kernel(x_ref, xp_ref, sh_ref, sc_ref, shift_ref, g_ref, mu_ref, w1_ref, a1_ref, g1_ref,
                      wr_ref, wk_ref, wv_ref, w2_ref, a2_ref, g2_ref, w0_ref, a0_ref,
                      r_ref, k_ref, v_ref, lw_ref, a_ref, gg_ref,
                      mix_scr, sw_scr, sa_scr, sg_scr, *, tm, seq_len):
    i = pl.program_id(0)
    j = pl.program_id(1)

    @pl.when(j == 0)
    def _():
        g = g_ref[...]
        scale = 1.0 + sc_ref[...]
        shift = sh_ref[...]
        h = _rms(x_ref[...]) * g * scale + shift
        row = lax.broadcasted_iota(jnp.int32, h.shape, 0)
        rolled = pltpu.roll(h, 1, 0)
        if seq_len % tm == 0:
            sc_row = scale[0:1] if scale.shape[0] > 1 else scale
            sh_row = shift[0:1] if shift.shape[0] > 1 else shift
            hp = _rms(xp_ref[...]) * g * sc_row + sh_row
            first = jnp.where((i % (seq_len // tm)) == 0, shift_ref[...], hp[SUBLANES - 1:SUBLANES])
            prev = jnp.where(row == 0, first, rolled)
        else:
            prev = jnp.where((row % seq_len) == 0, shift_ref[...], rolled)
        xx = prev - h
        mu = mu_ref[...]
        for n in range(6):
            mix_scr[n] = (h + xx * mu[n:n + 1]).astype(BF16)
        sw_scr[...] = jnp.tanh(jnp.dot(mix_scr[1], w1_ref[...], preferred_element_type=F32)).astype(BF16)
        sa_scr[...] = jnp.dot(mix_scr[4], a1_ref[...], preferred_element_type=F32).astype(BF16)
        sg_scr[...] = _sigmoid(jnp.dot(mix_scr[5], g1_ref[...], preferred_element_type=F32)).astype(BF16)

    r_ref[...] = jnp.dot(mix_scr[0], wr_ref[...], preferred_element_type=F32)
    k_ref[...] = jnp.dot(mix_scr[2], wk_ref[...], preferred_element_type=F32)
    v_ref[...] = jnp.dot(mix_scr[3], wv_ref[...], preferred_element_type=F32)
    wl = w0_ref[...] + jnp.dot(sw_scr[...], w2_ref[...], preferred_element_type=F32)
    lw_ref[...] = -jnp.exp(-_softplus(-wl) - 0.5)
    a_ref[...] = _sigmoid(a0_ref[...] + jnp.dot(sa_scr[...], a2_ref[...], preferred_element_type=F32))
    gg_ref[...] = jnp.dot(sg_scr[...], g2_ref[...], preferred_element_type=F32)


def _rwkv_proj(x, sh, sc, shift_state, g, mu, w1, a1, g1, wr, wk, wv, w2, a2, g2, w0, a0, n_batch, seq_len, tm):
    m, d = x.shape
    tn = _tile(d, 256)
    shm, scm = _Mod(sh, seq_len, tm), _Mod(sc, seq_len, tm)
    stm = _Mod(shift_state, seq_len, tm)
    rows8 = tm // SUBLANES
    full = lambda a: pl.BlockSpec(a.shape, lambda i, j: (0,) * a.ndim)
    colw = lambda a: pl.BlockSpec((a.shape[0], tn), lambda i, j: (0, j))
    out_spec = pl.BlockSpec((tm, tn), lambda i, j: (i, j))
    g2d, w0r, a0r = g.reshape(1, d), w0.reshape(1, d), a0.reshape(1, d)
    return pl.pallas_call(
        functools.partial(_rwkv_proj_kernel, tm=tm, seq_len=seq_len),
        grid=(m // tm, d // tn),
        in_specs=[pl.BlockSpec((tm, d), lambda i, j: (i, 0)),
                  pl.BlockSpec((SUBLANES, d), lambda i, j: (jnp.maximum(i * rows8 - 1, 0), 0)),
                  shm.spec(), scm.spec(), stm.spec(), full(g2d), full(mu), full(w1), full(a1), full(g1),
                  colw(wr), colw(wk), colw(wv), colw(w2), colw(a2), colw(g2), colw(w0r), colw(a0r)],
        out_specs=[out_spec] * 6,
        out_shape=[jax.ShapeDtypeStruct((m, d), F32)] * 6,
        scratch_shapes=[pltpu.VMEM((6, tm, d), BF16), pltpu.VMEM((tm, w1.shape[1]), BF16),
                        pltpu.VMEM((tm, a1.shape[1]), BF16), pltpu.VMEM((tm, g1.shape[1]), BF16)],
        compiler_params=_cparams(("parallel", "arbitrary")),
        name="rwkv_proj",
    )(x, x, shm.arr, scm.arr, stm.arr, g2d, mu, w1, a1, g1, wr, wk, wv, w2, a2, g2, w0r, a0r)


def _norm_mod_kernel(x_ref, sh_ref, sc_ref, g_ref, o_ref):
    o_ref[...] = _rms(x_ref[...]) * g_ref[...] * (1.0 + sc_ref[...]) + sh_ref[...]


def _norm_mod(x, sh, sc, g):
    r, d = x.shape
    spec = pl.BlockSpec((r, d), lambda i: (0, 0))
    return pl.pallas_call(
        _norm_mod_kernel, grid=(1,),
        in_specs=[spec, spec, spec, pl.BlockSpec((1, d), lambda i: (0, 0))],
        out_specs=spec, out_shape=jax.ShapeDtypeStruct((r, d), F32),
        compiler_params=_cparams(("arbitrary",)), name="norm_mod",
    )(x, sh, sc, g.reshape(1, d))


def _wkv_kernel(r_ref, k_ref, v_ref, lw_ref, a_ref, g_ref, kk_ref, ka_ref, rk_ref, lnw_ref, lnb_ref, s0_ref,
                o_ref, sout_ref, z_scr, *, tt, chunk):
    t = pl.program_id(2)
    nt = pl.num_programs(2)

    @pl.when(t == 0)
    def _():
        z_scr[...] = s0_ref[...]

    c = chunk
    n2 = 2 * c
    n_chunks = max(tt // c, 1)
    lane1 = lax.broadcasted_iota(jnp.int32, (1, LANES), 1)
    head_a = lane1 < RW_HEAD
    rr = lax.broadcasted_iota(jnp.int32, (LANES, LANES), 0)
    cc = lax.broadcasted_iota(jnp.int32, (LANES, LANES), 1)
    seg_ones = ((rr // RW_HEAD) == (cc // RW_HEAD)).astype(BF16)
    row = lax.broadcasted_iota(jnp.int32, (n2, n2), 0)
    col = lax.broadcasted_iota(jnp.int32, (n2, n2), 1)
    same = (row // c) == (col // c)
    low_incl = same & (col <= row)
    low_strict = same & (col < row)
    tri = (lax.broadcasted_iota(jnp.int32, (c, c), 1) <= lax.broadcasted_iota(jnp.int32, (c, c), 0)).astype(BF16)
    n_pairs = r_ref.shape[1] // LANES

    def seg_sum(x):
        return jnp.dot(x.astype(BF16), seg_ones, preferred_element_type=F32)

    def stack(x):
        return jnp.concatenate([jnp.where(head_a, x, 0.0), jnp.where(head_a, 0.0, x)], axis=0)

    pt_s, rt_s, ch_s, kh_s, v_s, w_end, l_pc, l_pk, a_rc, a_rk = ([] for _ in range(10))
    bonus, gate = [], []
    for p in range(n_pairs):
        ls = slice(p * LANES, (p + 1) * LANES)

        def load(ref):
            x = ref[:, ls]
            if tt < c:
                x = jnp.concatenate([x, jnp.zeros((c - tt, LANES), F32)], axis=0)
            return x

        r_all, k_all, v_all, lw_all, a_all = (load(ref) for ref in (r_ref, k_ref, v_ref, lw_ref, a_ref))
        gate.append(load(g_ref))
        kk_raw = k_all * kk_ref[:, ls]
        kk_all = kk_raw * lax.rsqrt(seg_sum(kk_raw * kk_raw) + 1e-12)
        kmod_all = k_all * (1.0 + (a_all - 1.0) * ka_ref[:, ls])
        bonus.append(seg_sum(r_all * kmod_all * rk_ref[:, ls]) * v_all)
        lw_hi = lw_all.astype(BF16)
        lw_lo = (lw_all - lw_hi.astype(F32)).astype(BF16)
        for ci in range(n_chunks):
            sl = slice(ci * c, (ci + 1) * c)
            cum = (jnp.dot(tri, lw_hi[sl], preferred_element_type=F32)
                   + jnp.dot(tri, lw_lo[sl], preferred_element_type=F32))
            cum_end = cum[c - 1:c, :]
            e_neg = jnp.exp(-cum)
            e_end = jnp.exp(cum_end - cum)
            pvec = -kk_all[sl]
            cvec = kk_all[sl] * a_all[sl]
            p_s = stack(pvec * jnp.exp(cum - lw_all[sl]))
            r_s = stack(r_all[sl] * jnp.exp(cum))
            gram = _dot_nt(jnp.concatenate([p_s, r_s], axis=0),
                           jnp.concatenate([stack(cvec * e_neg), stack(kmod_all[sl] * e_neg)], axis=0))
            l_pc.append(jnp.where(low_strict, gram[0:n2, 0:n2], 0.0))
            l_pk.append(jnp.where(low_strict, gram[0:n2, n2:2 * n2], 0.0))
            a_rc.append(jnp.where(low_incl, gram[n2:2 * n2, 0:n2], 0.0))
            a_rk.append(jnp.where(low_incl, gram[n2:2 * n2, n2:2 * n2], 0.0))
            pt_s.append(p_s)
            rt_s.append(r_s)
            ch_s.append(stack(cvec * e_end))
            kh_s.append(stack(kmod_all[sl] * e_end))
            v_s.append(stack(v_all[sl]))
            w_end.append(jnp.exp(cum_end))
    t_inv = _unit_lower_inverses([-x for x in l_pc], c)
    rng = range(n_pairs * n_chunks)
    lv = [_dot(l_pk[i], v_s[i]) for i in rng]
    m1 = [_dot(t_inv[i], pt_s[i]) for i in rng]
    b1 = [_dot(t_inv[i], lv[i]) for i in rng]
    m2 = [rt_s[i] + _dot(a_rc[i], m1[i]) for i in rng]
    bv = [jnp.concatenate([b1[i], v_s[i]], axis=0) for i in rng]
    b2 = [_dot(jnp.concatenate([a_rc[i], a_rk[i]], axis=1), bv[i]) for i in rng]
    m3 = [_dot_tn(ch_s[i], m1[i]) for i in rng]
    b3 = [_dot_tn(bv[i], jnp.concatenate([ch_s[i], kh_s[i]], axis=0)) for i in rng]

    for p in range(n_pairs):
        ls = slice(p * LANES, (p + 1) * LANES)
        z = z_scr[p]
        ys = []
        for ci in range(n_chunks):
            i = p * n_chunks + ci
            y_s = _dot_nt(m2[i], z) + b2[i]
            z = z * w_end[i] + _dot_nt(z, m3[i]) + b3[i]
            ys.append(y_s[0:c] + y_s[c:n2])
        z_scr[p] = z
        y = jnp.concatenate(ys, axis=0) if n_chunks > 1 else ys[0]
        mean = seg_sum(y) * (1.0 / RW_HEAD)
        dev = y - mean
        var = seg_sum(dev * dev) * (1.0 / RW_HEAD)
        yn = dev * lax.rsqrt(var + RW_GN_EPS) * lnw_ref[:, ls] + lnb_ref[:, ls]
        out = (yn + bonus[p]) * gate[p]
        o_ref[:, ls] = out[0:tt].astype(o_ref.dtype)

    @pl.when(t == nt - 1)
    def _():
        sout_ref[...] = z_scr[...]


def _wkv7(r, k, v, lw, a, g, k_k, k_a, r_k, ln_w, ln_b, s0_pairs, n_batch, seq_len):
    m, d = r.shape
    n_pairs = d // LANES
    tt = _tile(seq_len, 512)
    nt = seq_len // tt
    chunk = min(RW_CHUNK, max(seq_len, SUBLANES))
    group = min(n_pairs, max(1, WKV_ITEMS_PER_STEP // max(tt // chunk, 1)))
    row_spec = pl.BlockSpec((tt, group * LANES), lambda b, hp, t: (b * nt + t, hp))
    vec_spec = pl.BlockSpec((1, group * LANES), lambda b, hp, t: (0, hp))
    st_spec = pl.BlockSpec((None, group, LANES, LANES), lambda b, hp, t: (b, hp, 0, 0))
    vec = lambda x: x.reshape(1, d)
    return pl.pallas_call(
        functools.partial(_wkv_kernel, tt=tt, chunk=chunk),
        grid=(n_batch, n_pairs // group, nt),
        in_specs=[row_spec] * 6 + [vec_spec] * 5 + [st_spec],
        out_specs=[row_spec, st_spec],
        out_shape=[jax.ShapeDtypeStruct((m, d), BF16), jax.ShapeDtypeStruct(s0_pairs.shape, F32)],
        scratch_shapes=[pltpu.VMEM((group, LANES, LANES), F32)],
        compiler_params=_cparams(("parallel", "parallel", "arbitrary")),
        name="wkv7",
    )(r, k, v, lw, a, g, vec(k_k), vec(k_a), vec(r_k), vec(ln_w), vec(ln_b), s0_pairs)


def _pair_states(s):
    b, h, n, _ = s.shape
    s2 = s.reshape(b, h // 2, 2, n, n)
    zero = jnp.zeros((b, h // 2, n, n), s.dtype)
    top = jnp.concatenate([s2[:, :, 0], zero], axis=-1)
    bot = jnp.concatenate([zero, s2[:, :, 1]], axis=-1)
    return jnp.concatenate([top, bot], axis=-2)


def _unpair_states(sp):
    n = RW_HEAD
    b, hp = sp.shape[:2]
    return jnp.stack([sp[:, :, :n, :n], sp[:, :, n:, n:]], axis=2).reshape(b, 2 * hp, n, n)


def _router_kernel(x_ref, sh_ref, sc_ref, g_ref, rw_ref, rb_ref, h_ref, idx_ref, gate_ref):
    h = _rms(x_ref[...]) * g_ref[...] * (1.0 + sc_ref[...]) + sh_ref[...]
    h_ref[...] = h
    logits = jnp.dot(h, rw_ref[...], preferred_element_type=F32, precision=lax.Precision.HIGHEST) + rb_ref[...]
    lane = lax.broadcasted_iota(jnp.int32, logits.shape, 1)
    lane_f = lane.astype(F32)
    neg = -jnp.inf
    l1 = jnp.where(lane < N_EXPERTS, logits, neg)
    m1 = jnp.max(l1, axis=-1, keepdims=True)
    i1 = jnp.min(jnp.where(l1 == m1, lane_f, float(LANES)), axis=-1, keepdims=True)
    l2 = jnp.where(lane_f == i1, neg, l1)
    m2 = jnp.max(l2, axis=-1, keepdims=True)
    i2 = jnp.min(jnp.where(l2 == m2, lane_f, float(LANES)), axis=-1, keepdims=True)
    e = jnp.exp(m2 - m1)
    g0 = 1.0 / (1.0 + e)
    g1 = e / (1.0 + e)
    idx_ref[...] = jnp.where(lane == 0, i1, jnp.where(lane == 1, i2, 0.0)).astype(jnp.int32)
    gate_ref[...] = jnp.where(lane == 0, g0, jnp.where(lane == 1, g1, 0.0))


def _router(x, sh, sc, g, rw_pad, rb_pad, seq_len, tm):
    m, d = x.shape
    shm, scm = _Mod(sh, seq_len, tm), _Mod(sc, seq_len, tm)
    return pl.pallas_call(
        _router_kernel,
        grid=(m // tm,),
        in_specs=[pl.BlockSpec((tm, d), lambda i: (i, 0)), shm.spec1(), scm.spec1(),
                  pl.BlockSpec((1, d), lambda i: (0, 0)),
                  pl.BlockSpec((d, LANES), lambda i: (0, 0)), pl.BlockSpec((1, LANES), lambda i: (0, 0))],
        out_specs=[pl.BlockSpec((tm, d), lambda i: (i, 0)), pl.BlockSpec((tm, LANES), lambda i: (i, 0)),
                   pl.BlockSpec((tm, LANES), lambda i: (i, 0))],
        out_shape=[jax.ShapeDtypeStruct((m, d), F32), jax.ShapeDtypeStruct((m, LANES), jnp.int32),
                   jax.ShapeDtypeStruct((m, LANES), F32)],
        compiler_params=_cparams(("parallel",)),
        name="moe_router",
    )(x, shm.arr, scm.arr, g.reshape(1, d), rw_pad, rb_pad)


def _gather_rows_kernel(idx_ref, src_ref, o_ref, sem, *, rows):
    i = pl.program_id(0)

    def row_copy(r):
        return pltpu.make_async_copy(src_ref.at[pl.ds(idx_ref[i * rows + r], 1), :], o_ref.at[pl.ds(r, 1), :], sem)

    def start(r, carry):
        row_copy(r).start()
        return carry

    def wait(r, carry):
        row_copy(r).wait()
        return carry

    lax.fori_loop(0, rows, start, 0, unroll=GATHER_UNROLL)
    lax.fori_loop(0, rows, wait, 0, unroll=GATHER_UNROLL)


def _gather_rows(src, row_idx, rows):
    n_rows = row_idx.shape[0]
    d = src.shape[1]
    assert n_rows % rows == 0
    grid_spec = pltpu.PrefetchScalarGridSpec(
        num_scalar_prefetch=1,
        grid=(n_rows // rows,),
        in_specs=[pl.BlockSpec(memory_space=pl.ANY)],
        out_specs=pl.BlockSpec((rows, d), lambda i, idx: (i, 0)),
        scratch_shapes=[pltpu.SemaphoreType.DMA(())],
    )
    return pl.pallas_call(
        functools.partial(_gather_rows_kernel, rows=rows),
        grid_spec=grid_spec,
        out_shape=jax.ShapeDtypeStruct((n_rows, d), src.dtype),
        compiler_params=_cparams(("arbitrary",)),
        name="moe_gather",
    )(row_idx, src)


def _moe_ffn_kernel(te_ref, na_ref, x_ref, wg_ref, wu_ref, wd_ref, o_ref, h_ref):
    i = pl.program_id(0)
    f = pl.program_id(1)
    active = i < na_ref[0]

    @pl.when(f == 0)
    def _():
        h_ref[...] = x_ref[...].astype(BF16)
        o_ref[...] = jnp.zeros(o_ref.shape, F32)

    @pl.when(active)
    def _():
        h = h_ref[...]
        gate = jnp.dot(h, wg_ref[...], preferred_element_type=F32)
        up = jnp.dot(h, wu_ref[...], preferred_element_type=F32)
        act = (_silu(gate) * up).astype(BF16)
        o_ref[...] += jnp.dot(act, wd_ref[...], preferred_element_type=F32)


def _moe_ffn(xs, tile_expert, n_active, wg, wu, wd, tm):
    n_rows, d = xs.shape
    dff = wg.shape[2]
    tf = _tile(dff, FFN_TILE)
    nf = dff // tf

    def fcol(i, f, te, na):
        return jnp.where(i < na[0], f, nf - 1)

    grid_spec = pltpu.PrefetchScalarGridSpec(
        num_scalar_prefetch=2,
        grid=(n_rows // tm, nf),
        in_specs=[pl.BlockSpec((tm, d), lambda i, f, te, na: (i, 0)),
                  pl.BlockSpec((None, d, tf), lambda i, f, te, na: (te[i], 0, fcol(i, f, te, na))),
                  pl.BlockSpec((None, d, tf), lambda i, f, te, na: (te[i], 0, fcol(i, f, te, na))),
                  pl.BlockSpec((None, tf, d), lambda i, f, te, na: (te[i], fcol(i, f, te, na), 0))],
        out_specs=pl.BlockSpec((tm, d), lambda i, f, te, na: (i, 0)),
        scratch_shapes=[pltpu.VMEM((tm, d), BF16)],
    )
    return pl.pallas_call(
        _moe_ffn_kernel,
        grid_spec=grid_spec,
        out_shape=jax.ShapeDtypeStruct((n_rows, d), F32),
        compiler_params=_cparams(("parallel", "arbitrary")),
        name="moe_ffn",
    )(tile_expert, n_active, xs, wg, wu, wd)


def _combine_kernel(x_ref, gt_ref, gate_ref, g_ref, y0_ref, y1_ref, o_ref):
    gates = gate_ref[...]
    y = gates[:, 0:1] * y0_ref[...] + gates[:, 1:2] * y1_ref[...]
    x = x_ref[...] + gt_ref[...] * y
    o_ref[...] = _rms(x) * g_ref[...]


def _combine_final(x, gt, gates, y0, y1, g, row0, seq_len, tm):
    m, d = x.shape
    gtm = _Mod(gt, seq_len, tm)
    assert row0 % tm == 0
    blk0 = row0 // tm
    y_spec = pl.BlockSpec((tm, d), lambda i: (blk0 + i, 0))
    return pl.pallas_call(
        _combine_kernel,
        grid=(m // tm,),
        in_specs=[pl.BlockSpec((tm, d), lambda i: (i, 0)), gtm.spec1(),
                  pl.BlockSpec((tm, LANES), lambda i: (i, 0)),
                  pl.BlockSpec((1, d), lambda i: (0, 0)), y_spec, y_spec],
        out_specs=pl.BlockSpec((tm, d), lambda i: (i, 0)),
        out_shape=jax.ShapeDtypeStruct((m, d), F32),
        compiler_params=_cparams(("parallel",)),
        name="moe_combine_final",
    )(x, gtm.arr, gates, g.reshape(1, d), y0, y1)


def _split_mod(mod):
    return [mod[:, n * D_MODEL:(n + 1) * D_MODEL] for n in range(6)]


def _layer_even(x, mod, pos, conv_state, gdn_state, paged, p, n_batch, seq_len, tm):
    sh1, sc1, gt1, sh2, sc2, gt2 = mod
    m = x.shape[0]
    if seq_len % tm == 0:
        tabs = _rope_tables(pos)
    else:
        tabs = tuple(jnp.tile(t, (n_batch, 1)) for t in _rope_tables(pos))
    gq, gk, gv, gz, ba = _norm_proj(x, sh1, sc1, p["norm_mix0"], p["w_gdn"], (False,) * 4, p["w_ba"], None,
                                    seq_len, tm)
    dq, dk, dv = _norm_proj(x, sh1, sc1, p["norm_mix0"], p["w_da"], (True, True, False), None, tabs, seq_len, tm)
    o_gdn, new_gdn = _gated_deltanet(gq, gk, gv, gz, ba, conv_state, gdn_state, p["gdn_conv_w"], p["gdn_a_log"],
                                     p["gdn_dt_bias"], p["gdn_norm_w"], n_batch, seq_len)
    lam_args = (p["da_lq1"], p["da_lk1"], p["da_lq2"], p["da_lk2"], p["da_subln_w"])
    if paged is None:
        o_da = _diff_attention_prompt(dq, dk, dv, *lam_args, n_batch, seq_len)
    else:
        o_da = _diff_attention_sample(dq, dk, dv, paged[0], paged[1], paged[2], *lam_args, n_batch, seq_len)
    x = _proj_residual(x, gt1, [o_gdn, o_da], p["w_out"], seq_len, tm)
    x = _norm_ffn(x, sh2, sc2, gt2, p["norm_ffn0"], p["ffn_wg"], p["ffn_wu"], p["ffn_wd"], seq_len, tm)
    raw = jnp.concatenate([gq.reshape(n_batch, seq_len, -1), gk.reshape(n_batch, seq_len, -1),
                           gv.reshape(n_batch, seq_len, -1)], axis=-1)
    ext = jnp.concatenate([conv_state, raw[:, max(seq_len - (GDN_CONV - 1), 0):]], axis=1)
    new_conv = ext[:, ext.shape[1] - (GDN_CONV - 1):]
    k_out = dk.reshape(n_batch, seq_len, DA_HEADS, 2 * DA_DK)
    v_out = dv.reshape(n_batch, seq_len, DA_HEADS, DA_DV)
    return x, k_out, v_out, new_conv, new_gdn


def _layer_odd_mix(x, mod, shift_state, wkv_state, p, n_batch, seq_len, tm):
    sh1, sc1, gt1 = mod[:3]
    d = x.shape[1]
    tm_rw = _tile(tm, 512)
    r, k, v, lw, a, g = _rwkv_proj(x, sh1, sc1, shift_state, p["norm_mix1"], p["rw_mu"], p["rw_w1"], p["rw_a1"],
                                   p["rw_g1"], p["rw_wr"], p["rw_wk"], p["rw_wv"], p["rw_w2"], p["rw_a2"],
                                   p["rw_g2"], p["rw_w0"], p["rw_a0"], n_batch, seq_len, tm_rw)
    yg, s_pairs = _wkv7(r, k, v, lw, a, g, p["rw_k_k"], p["rw_k_a"], p["rw_r_k"], p["rw_ln_w"], p["rw_ln_b"],
                        _pair_states(wkv_state), n_batch, seq_len)
    last = x.reshape(n_batch, seq_len, d)[:, seq_len - 1]
    new_shift = _norm_mod(last, sh1, sc1, p["norm_mix1"])
    x = _proj_residual(x, gt1, [yg], [p["rw_wo"]], seq_len, tm)
    return x, new_shift, _unpair_states(s_pairs)


def kernel(x_prompt, x_sample, cache_k, cache_v, state_gdn_conv, state_gdn, state_rwkv_shift, state_rwkv, page_table, c_prompt, c_sample, ada_w0, ada_b0, norm_mix0, w_in0, gdn_conv_w, gdn_a_log, gdn_dt_bias, gdn_norm_w, da_lq1, da_lk1, da_lq2, da_lk2, da_subln_w, w_out0, norm_ffn0, ffn_w_gate, ffn_w_up, ffn_w_down, ada_w1, ada_b1, norm_mix1, rw_mu, rw_w0, rw_w1, rw_w2, rw_a0, rw_a1, rw_a2, rw_g1, rw_g2, rw_k_k, rw_k_a, rw_r_k, rw_wr, rw_wk, rw_wv, rw_wo, rw_ln_w, rw_ln_b, norm_ffn1, moe_router_w, moe_router_b, moe_w_gate, moe_w_up, moe_w_down, norm_final):
    bp, lp, d = x_prompt.shape
    bs, ls, _ = x_sample.shape
    n_pages = page_table.shape[1]
    past_len = n_pages * PAGE_SIZE
    mp, ms = bp * lp, bs * ls
    tm_p = _tile(lp, 1024)
    tm_s = _tile(ms, 256)

    qk_w = GDN_HEADS * GDN_DK
    c0 = 3 * qk_w
    c1 = c0 + qk_w
    c2 = c1 + 2 * GDN_HEADS
    da_w = DA_HEADS * 2 * DA_DK
    bf = lambda w: w.astype(BF16)
    w_ba = jnp.concatenate([w_in0[:, c1:c2], jnp.zeros((d, LANES - 2 * GDN_HEADS), F32)], axis=1)
    lora_pad = lambda w, axis: jnp.pad(w, [(0, (-w.shape[a]) % LANES if a == axis else 0) for a in range(2)])
    p = {
        "norm_mix0": norm_mix0,
        "w_gdn": [bf(w_in0[:, n * qk_w:(n + 1) * qk_w]) for n in range(4)],
        "w_ba": bf(w_ba),
        "w_da": [bf(w_in0[:, c2 + n * da_w:c2 + (n + 1) * da_w]) for n in range(3)],
        "gdn_conv_w": gdn_conv_w, "gdn_a_log": gdn_a_log, "gdn_dt_bias": gdn_dt_bias, "gdn_norm_w": gdn_norm_w,
        "da_lq1": da_lq1, "da_lk1": da_lk1, "da_lq2": da_lq2, "da_lk2": da_lk2, "da_subln_w": da_subln_w,
        "w_out": [bf(w_out0[:qk_w]), bf(w_out0[qk_w:])],
        "norm_ffn0": norm_ffn0, "ffn_wg": bf(ffn_w_gate), "ffn_wu": bf(ffn_w_up), "ffn_wd": bf(ffn_w_down),
        "norm_mix1": norm_mix1, "rw_mu": rw_mu, "rw_w0": rw_w0, "rw_a0": rw_a0,
        "rw_w1": bf(lora_pad(rw_w1, 1)), "rw_w2": bf(lora_pad(rw_w2, 0)),
        "rw_a1": bf(lora_pad(rw_a1, 1)), "rw_a2": bf(lora_pad(rw_a2, 0)),
        "rw_g1": bf(rw_g1), "rw_g2": bf(rw_g2),
        "rw_wr": bf(rw_wr), "rw_wk": bf(rw_wk), "rw_wv": bf(rw_wv), "rw_wo": bf(rw_wo),
        "rw_k_k": rw_k_k, "rw_k_a": rw_k_a, "rw_r_k": rw_r_k.reshape(-1), "rw_ln_w": rw_ln_w, "rw_ln_b": rw_ln_b,
    }

    c_all = jnp.concatenate([c_prompt, c_sample], axis=0)
    mod0 = _ada_mod(c_all, ada_w0, ada_b0)
    mod1 = _ada_mod(c_all, ada_w1, ada_b1)
    mod0_p, mod0_s = _split_mod(mod0[:bp]), _split_mod(mod0[bp:])
    mod1_p, mod1_s = _split_mod(mod1[:bp]), _split_mod(mod1[bp:])

    xp = x_prompt.reshape(mp, d)
    xs = x_sample.reshape(ms, d)
    pos_p = jnp.arange(lp, dtype=jnp.int32)
    pos_s = past_len + jnp.arange(ls, dtype=jnp.int32)

    xp, k_p, v_p, conv_p, gdn_p = _layer_even(
        xp, mod0_p, pos_p, jnp.zeros((bp, GDN_CONV - 1, 3 * qk_w), F32),
        jnp.zeros((bp, GDN_HEADS, GDN_DK, GDN_DK), F32), None, p, bp, lp, tm_p)
    xs, k_s, v_s, conv_s, gdn_s = _layer_even(
        xs, mod0_s, pos_s, state_gdn_conv, state_gdn, (cache_k, cache_v, page_table), p, bs, ls, tm_s)

    n_rw_heads = d // RW_HEAD
    xp, shift_p, rw_p = _layer_odd_mix(xp, mod1_p, jnp.zeros((bp, d), F32),
                                       jnp.zeros((bp, n_rw_heads, RW_HEAD, RW_HEAD), F32), p, bp, lp, tm_p)
    xs, shift_s, rw_s = _layer_odd_mix(xs, mod1_s, state_rwkv_shift, state_rwkv, p, bs, ls, tm_s)

    rw_pad = jnp.concatenate([moe_router_w, jnp.zeros((d, LANES - N_EXPERTS), F32)], axis=1)
    rb_pad = jnp.concatenate([moe_router_b, jnp.zeros((LANES - N_EXPERTS,), F32)]).reshape(1, LANES)
    tm_r = _tile(lp, 512)
    h_p, idx_p, gate_p = _router(xp, mod1_p[3], mod1_p[4], norm_ffn1, rw_pad, rb_pad, lp, tm_r)
    h_s, idx_s, gate_s = _router(xs, mod1_s[3], mod1_s[4], norm_ffn1, rw_pad, rb_pad, ls, tm_s)
    h_all = jnp.concatenate([h_p, h_s], axis=0)
    flat_e = jnp.concatenate([idx_p[:, :TOP_K], idx_s[:, :TOP_K]], axis=0).reshape(-1)
    n_assign = flat_e.shape[0]
    tmoe = MOE_TILE
    onehot = (flat_e[:, None] == jnp.arange(N_EXPERTS, dtype=jnp.int32)[None, :]).astype(jnp.int32)
    running = jnp.cumsum(onehot, axis=0)
    pos_in_e = jnp.sum((running - onehot) * onehot, axis=1)
    counts = running[-1]
    padded = (counts + tmoe - 1) // tmoe * tmoe
    pad_end = jnp.cumsum(padded)
    pad_start = pad_end - padded
    dest = (pad_start[flat_e] + pos_in_e).astype(jnp.int32)
    n_tiles = -(-n_assign // tmoe) + N_EXPERTS
    n_rows = n_tiles * tmoe
    row_tok = jnp.zeros((n_rows,), jnp.int32).at[dest].set(jnp.arange(n_assign, dtype=jnp.int32) // TOP_K)
    n_active = (pad_end[-1] // tmoe).astype(jnp.int32)
    tile_ids = jnp.minimum(jnp.arange(n_tiles, dtype=jnp.int32), n_active - 1)
    tile_e = jnp.minimum(jnp.searchsorted(pad_end, tile_ids * tmoe, side="right"), N_EXPERTS - 1).astype(jnp.int32)
    x_sorted = _gather_rows(h_all, row_tok, tmoe)
    yb = _moe_ffn(x_sorted, tile_e, n_active.reshape(1), bf(moe_w_gate), bf(moe_w_up), bf(moe_w_down), tmoe)
    dest2 = dest.reshape(-1, TOP_K)
    rows_c = _tile(mp + ms, GATHER_ROWS)
    y0 = _gather_rows(yb, dest2[:, 0], rows_c)
    y1 = _gather_rows(yb, dest2[:, 1], rows_c)
    tm_c = _tile(lp, 256)
    y_p = _combine_final(xp, mod1_p[5], gate_p, y0, y1, norm_final, 0, lp, tm_c)
    y_s = _combine_final(xs, mod1_s[5], gate_s, y0, y1, norm_final, mp, ls, tm_s)

    return (y_p.reshape(bp, lp, d), y_s.reshape(bs, ls, d), k_p, v_p, k_s, v_s, conv_p, conv_s,
            gdn_p, gdn_s, shift_p, shift_s, rw_p, rw_s)
```

```python
import functools
import math

import jax
import jax.numpy as jnp
from jax import lax
from jax.experimental import pallas as pl
from jax.experimental.pallas import tpu as pltpu

F32 = jnp.float32
BF16 = jnp.bfloat16

D_MODEL = 2048
NORM_EPS = 1e-6
GDN_HEADS = 8
GDN_DK = 128
GDN_CONV = 4
GDN_CHUNK = 64
DA_HEADS = 8
DA_DK = 64
DA_DV = 128
ROPE_DIM = 16
ROPE_THETA = 500000.0
DA_LAMBDA_INIT = 0.8 - 0.6 * math.exp(-0.3 * 0)
RW_HEAD = 64
RW_CHUNK = 64
RW_GN_EPS = 1e-5 * RW_HEAD
N_EXPERTS = 8
TOP_K = 2
PAGE_SIZE = 128
LANES = 128
SUBLANES = 8
VMEM_LIMIT = 56 * 1024 * 1024
MOE_TILE = 512
FFN_TILE = 1024
PAGES_PER_STEP = 8
GATHER_ROWS = 512
GATHER_UNROLL = 8
WKV_ITEMS_PER_STEP = 16
GDN_ITEMS_PER_STEP = 16


def _cparams(sem):
    return pltpu.CompilerParams(dimension_semantics=sem, vmem_limit_bytes=VMEM_LIMIT)


def _tile(n, pref):
    if n <= pref:
        return n
    t = pref
    while t >= SUBLANES:
        if n % t == 0 and t % SUBLANES == 0:
            return t
        t -= SUBLANES
    return n


def _dot(a, b):
    return jnp.dot(a.astype(BF16), b.astype(BF16), preferred_element_type=F32)


def _dot_nt(a, b):
    return lax.dot_general(a.astype(BF16), b.astype(BF16), (((1,), (1,)), ((), ())),
                           preferred_element_type=F32)


def _dot_tn(a, b):
    return lax.dot_general(a.astype(BF16), b.astype(BF16), (((0,), (0,)), ((), ())),
                           preferred_element_type=F32)


def _sigmoid(x):
    return 1.0 / (1.0 + jnp.exp(-x))


def _silu(x):
    return x * _sigmoid(x)


def _softplus(x):
    return jnp.maximum(x, 0.0) + jnp.log(1.0 + jnp.exp(-jnp.abs(x)))


def _rms(x, eps=NORM_EPS):
    return x * lax.rsqrt(jnp.mean(x * x, axis=-1, keepdims=True) + eps)


def _unit_lower_inverse(a, block):
    n = a.shape[0]
    row = lax.broadcasted_iota(jnp.int32, (n, n), 0)
    col = lax.broadcasted_iota(jnp.int32, (n, n), 1)
    inv = (row == col).astype(F32) - jnp.where((row // 2) == (col // 2), a, 0.0)
    s = 4
    while s <= block:
        a_s = jnp.where(((row // s) == (col // s)) & ((row // (s // 2)) != (col // (s // 2))), a, 0.0)
        inv = inv - _dot(_dot(inv, a_s), inv)
        s *= 2
    return inv


def _unit_lower_inverses(mats, block):
    n = mats[0].shape[0]
    row = lax.broadcasted_iota(jnp.int32, (n, n), 0)
    col = lax.broadcasted_iota(jnp.int32, (n, n), 1)
    eye = (row == col).astype(F32)
    pair = (row // 2) == (col // 2)
    invs = [eye - jnp.where(pair, a, 0.0) for a in mats]
    s = 4
    while s <= block:
        level = ((row // s) == (col // s)) & ((row // (s // 2)) != (col // (s // 2)))
        left = [_dot(inv, jnp.where(level, a, 0.0)) for inv, a in zip(invs, mats)]
        invs = [inv - _dot(lf, inv) for inv, lf in zip(invs, left)]
        s *= 2
    return invs


class _Mod:
    def __init__(self, m, seq_len, tm):
        self.width = m.shape[1]
        if seq_len % tm == 0:
            self.per_batch = True
            self.tiles_per_batch = seq_len // tm
            self.arr = m.reshape(m.shape[0], 1, self.width)
        else:
            assert tm % seq_len == 0
            self.per_batch = False
            self.arr = jnp.repeat(m, seq_len, axis=0)
        self.tm = tm

    def spec(self, tn=None, col=None):
        tn = self.width if tn is None else tn
        col = (lambda i, j: 0) if col is None else col
        if self.per_batch:
            tpb = self.tiles_per_batch
            return pl.BlockSpec((None, 1, tn), lambda i, j: (i // tpb, 0, col(i, j)))
        return pl.BlockSpec((self.tm, tn), lambda i, j: (i, col(i, j)))

    def spec1(self):
        if self.per_batch:
            tpb = self.tiles_per_batch
            return pl.BlockSpec((None, 1, self.width), lambda i: (i // tpb, 0, 0))
        return pl.BlockSpec((self.tm, self.width), lambda i: (i, 0))


def _ada_kernel(c_ref, w_ref, b_ref, o_ref):
    a = _silu(c_ref[...])
    o_ref[...] = _dot(a, w_ref[...]) + b_ref[...]


def _ada_mod(c, w, b):
    r, d = c.shape
    n = w.shape[1]
    tn = _tile(n, 1024)
    return pl.pallas_call(
        _ada_kernel,
        grid=(n // tn,),
        in_specs=[pl.BlockSpec((r, d), lambda j: (0, 0)),
                  pl.BlockSpec((d, tn), lambda j: (0, j)),
                  pl.BlockSpec((1, tn), lambda j: (0, j))],
        out_specs=pl.BlockSpec((r, tn), lambda j: (0, j)),
        out_shape=jax.ShapeDtypeStruct((r, n), F32),
        compiler_params=_cparams(("parallel",)),
        name="ada_mod",
    )(c, w, b.reshape(1, n))


def _rope_tile(acc, cos, sin_lo, sin_hi):
    fwd = pltpu.roll(acc, LANES - ROPE_DIM // 2, 1)
    bwd = pltpu.roll(acc, ROPE_DIM // 2, 1)
    return acc * cos + fwd * sin_lo + bwd * sin_hi


def _norm_proj_kernel(*refs, n_w, rope, has_extra, tn):
    x_ref, sh_ref, sc_ref, g_ref = refs[:4]
    pos = 4
    w_refs = refs[pos:pos + n_w]
    pos += n_w
    if has_extra:
        we_ref = refs[pos]
        pos += 1
    if any(rope):
        cos_ref, slo_ref, shi_ref = refs[pos:pos + 3]
        pos += 3
    o_refs = refs[pos:pos + n_w]
    pos += n_w
    if has_extra:
        oe_ref = refs[pos]
        pos += 1
    h_ref = refs[pos]
    j = pl.program_id(1)

    @pl.when(j == 0)
    def _():
        h = _rms(x_ref[...]) * g_ref[...] * (1.0 + sc_ref[...]) + sh_ref[...]
        h_ref[...] = h.astype(BF16)
        if has_extra:
            oe_ref[...] = jnp.dot(h_ref[...], we_ref[...], preferred_element_type=F32)

    h = h_ref[...]
    for k in range(n_w):
        acc = jnp.dot(h, w_refs[k][...], preferred_element_type=F32)
        if rope[k]:
            cos, slo, shi = cos_ref[...], slo_ref[...], shi_ref[...]
            for c in range(tn // LANES):
                sl = slice(c * LANES, (c + 1) * LANES)
                o_refs[k][:, sl] = _rope_tile(acc[:, sl], cos, slo, shi)
        else:
            o_refs[k][...] = acc


def _norm_proj(x, sh, sc, g, ws, rope, extra_w, rope_tabs, seq_len, tm):
    m, d = x.shape
    n = ws[0].shape[1]
    tn = _tile(n, 256)
    n_w = len(ws)
    shm, scm = _Mod(sh, seq_len, tm), _Mod(sc, seq_len, tm)
    in_specs = [pl.BlockSpec((tm, d), lambda i, j: (i, 0)), shm.spec(), scm.spec(),
                pl.BlockSpec((1, d), lambda i, j: (0, 0))]
    args = [x, shm.arr, scm.arr, g.reshape(1, d)]
    for w in ws:
        in_specs.append(pl.BlockSpec((d, tn), lambda i, j: (0, j)))
        args.append(w)
    has_extra = extra_w is not None
    if has_extra:
        in_specs.append(pl.BlockSpec((d, LANES), lambda i, j: (0, 0)))
        args.append(extra_w)
    if any(rope):
        if seq_len % tm == 0:
            tpb = seq_len // tm
            tab_spec = pl.BlockSpec((tm, LANES), lambda i, j: (i % tpb, 0))
        else:
            tab_spec = pl.BlockSpec((tm, LANES), lambda i, j: (i, 0))
        for t in rope_tabs:
            in_specs.append(tab_spec)
            args.append(t)
    out_specs = [pl.BlockSpec((tm, tn), lambda i, j: (i, j)) for _ in ws]
    out_shape = [jax.ShapeDtypeStruct((m, n), F32) for _ in ws]
    if has_extra:
        out_specs.append(pl.BlockSpec((tm, LANES), lambda i, j: (i, 0)))
        out_shape.append(jax.ShapeDtypeStruct((m, LANES), F32))
    return pl.pallas_call(
        functools.partial(_norm_proj_kernel, n_w=n_w, rope=tuple(rope), has_extra=has_extra, tn=tn),
        grid=(m // tm, n // tn),
        in_specs=in_specs,
        out_specs=out_specs,
        out_shape=out_shape,
        scratch_shapes=[pltpu.VMEM((tm, d), BF16)],
        compiler_params=_cparams(("parallel", "arbitrary")),
        name="norm_proj",
    )(*args)


def _rope_tables(pos):
    half = ROPE_DIM // 2
    inv_freq = ROPE_THETA ** (-jnp.arange(half, dtype=F32) * (2.0 / ROPE_DIM))
    ang = pos.astype(F32)[:, None] * inv_freq
    cos, sin = jnp.cos(ang), jnp.sin(ang)
    n = pos.shape[0]
    ones = jnp.ones((n, DA_DK - ROPE_DIM), F32)
    zeros = jnp.zeros((n, DA_DK - ROPE_DIM), F32)
    zh = jnp.zeros((n, half), F32)
    cos_m = jnp.concatenate([cos, cos, ones], axis=1)
    slo_m = jnp.concatenate([-sin, zh, zeros], axis=1)
    shi_m = jnp.concatenate([zh, sin, zeros], axis=1)
    return tuple(jnp.concatenate([t, t], axis=1) for t in (cos_m, slo_m, shi_m))


def _gdn_kernel(q_ref, k_ref, v_ref, z_ref, ba_ref, csq_ref, csk_ref, csv_ref, cwq_ref, cwk_ref, cwv_ref,
                s0_ref, alog_ref, dtb_ref, nw_ref, o_ref, sout_ref, s_scr, hist_scr, ext_scr, *, tt, chunk):
    hp = pl.program_id(1)
    t = pl.program_id(2)
    nt = pl.num_programs(2)
    n_hist = SUBLANES

    @pl.when(t == 0)
    def _():
        s_scr[...] = s0_ref[...]
        hist_scr[0] = csq_ref[...]
        hist_scr[1] = csk_ref[...]
        hist_scr[2] = csv_ref[...]

    conv = []
    for s, (raw_ref, cw_ref) in enumerate(((q_ref, cwq_ref), (k_ref, cwk_ref), (v_ref, cwv_ref))):
        raw = raw_ref[...]
        ext_scr[s, 0:n_hist, :] = hist_scr[s]
        ext_scr[s, n_hist:n_hist + tt, :] = raw
        cw = cw_ref[...]
        y = raw * cw[GDN_CONV - 1:GDN_CONV, :]
        for dly in range(1, GDN_CONV):
            y = y + ext_scr[s, n_hist - dly:n_hist - dly + tt, :] * cw[GDN_CONV - 1 - dly:GDN_CONV - dly, :]
        hist_scr[s] = raw[tt - n_hist:tt, :]
        conv.append(_silu(y))
    q_all, k_all, v_all = conv
    z_all = z_ref[...]

    ba = ba_ref[...]
    lane = lax.broadcasted_iota(jnp.int32, ba.shape, 1)
    gates = jnp.where(lane < GDN_HEADS, _sigmoid(ba), -jnp.exp(alog_ref[...]) * _softplus(ba + dtb_ref[...]))
    n_heads = q_ref.shape[1] // LANES
    n_pairs = n_heads // 2
    beta_cols, g_cols = [], []
    for hq in range(n_heads):
        hh = n_heads * hp + hq
        beta_cols.append(jnp.sum(jnp.where(lane == hh, gates, 0.0), axis=1, keepdims=True))
        g_cols.append(jnp.sum(jnp.where(lane == GDN_HEADS + hh, gates, 0.0), axis=1, keepdims=True))

    c = chunk
    n2 = 2 * c
    row = lax.broadcasted_iota(jnp.int32, (n2, n2), 0)
    col = lax.broadcasted_iota(jnp.int32, (n2, n2), 1)
    same = (row // c) == (col // c)
    low_incl = same & (col <= row)
    low_strict = same & (col < row)
    up_incl = same & (row <= col)
    eye = row == col
    nw = nw_ref[...]

    def stack(x_all, r0, pq):
        return jnp.concatenate([x_all[r0:r0 + c, (2 * pq + hl) * LANES:(2 * pq + hl + 1) * LANES]
                                for hl in range(2)], axis=0)

    def stack_col(cols, r0, pq):
        return jnp.concatenate([cols[2 * pq + hl][r0:r0 + c, :] for hl in range(2)], axis=0)

    n_chunks = tt // c
    items = [(pq, ci) for pq in range(n_pairs) for ci in range(n_chunks)]
    rng = range(len(items))
    a_mats, qks, kes, bvs, qes, kds, s_decay = ([] for _ in range(7))
    for pq, ci in items:
        r0 = ci * c
        q = stack(q_all, r0, pq)
        k = stack(k_all, r0, pq)
        beta = stack_col(beta_cols, r0, pq)
        g = stack_col(g_cols, r0, pq)
        q = q * lax.rsqrt(jnp.sum(q * q, axis=-1, keepdims=True) + 1e-12) * (GDN_DK ** -0.5)
        k = k * lax.rsqrt(jnp.sum(k * k, axis=-1, keepdims=True) + 1e-12)
        g_row = jnp.sum(jnp.where(eye, g, 0.0), axis=0, keepdims=True)
        gc_col = jnp.sum(jnp.where(low_incl, g_row, 0.0), axis=1, keepdims=True)
        gc_row = jnp.sum(jnp.where(up_incl, g, 0.0), axis=0, keepdims=True)
        decay = jnp.where(low_incl, jnp.exp(jnp.where(low_incl, gc_col - gc_row, 0.0)), 0.0)
        gram = _dot_nt(jnp.concatenate([k * beta, q], axis=0), k)
        a_mats.append(jnp.where(low_strict, gram[0:n2] * decay, 0.0))
        qks.append(jnp.where(low_incl, gram[n2:2 * n2] * decay, 0.0))
        egc = jnp.exp(gc_col)
        kes.append(k * (beta * egc))
        bvs.append(stack(v_all, r0, pq) * beta)
        qes.append(q * egc)
        g_last = [gc_col[(hl + 1) * c - 1:(hl + 1) * c, :] for hl in range(2)]
        kds.append([k[hl * c:(hl + 1) * c] * jnp.exp(g_last[hl] - gc_col[hl * c:(hl + 1) * c]) for hl in range(2)])
        s_decay.append([jnp.exp(g_last[hl]) for hl in range(2)])
    t_inv = _unit_lower_inverses(a_mats, c)
    mb1 = [_dot(t_inv[i], jnp.concatenate([kes[i], bvs[i]], axis=1)) for i in rng]
    mb2 = [_dot(qks[i], mb1[i]) for i in rng]
    m2 = [qes[i] - mb2[i][:, 0:LANES] for i in rng]
    b2 = [mb2[i][:, LANES:2 * LANES] for i in rng]
    mb3 = [[_dot_tn(kds[i][hl], mb1[i][hl * c:(hl + 1) * c]) for hl in range(2)] for i in rng]

    outs = []
    for pq in range(n_pairs):
        s = [s_scr[2 * pq], s_scr[2 * pq + 1]]
        for ci in range(n_chunks):
            i = pq * n_chunks + ci
            s_cat = jnp.concatenate(s, axis=1).astype(BF16)
            o_wide = jnp.dot(m2[i].astype(BF16), s_cat, preferred_element_type=F32)
            outs.append(jnp.concatenate([o_wide[0:c, 0:LANES], o_wide[c:n2, LANES:2 * LANES]], axis=0) + b2[i])
            s = [s[hl] * s_decay[i][hl] - _dot(mb3[i][hl][:, 0:LANES], s[hl]) + mb3[i][hl][:, LANES:2 * LANES]
                 for hl in range(2)]
        s_scr[2 * pq] = s[0]
        s_scr[2 * pq + 1] = s[1]

    for i, (pq, ci) in enumerate(items):
        r0 = ci * c
        o = _rms(outs[i]) * nw * _silu(stack(z_all, r0, pq))
        for hl in range(2):
            lanes = slice((2 * pq + hl) * LANES, (2 * pq + hl + 1) * LANES)
            o_ref[r0:r0 + c, lanes] = o[hl * c:(hl + 1) * c].astype(o_ref.dtype)

    @pl.when(t == nt - 1)
    def _():
        sout_ref[...] = s_scr[...]


def _gated_deltanet(qraw, kraw, vraw, z, ba, conv_state, s0, conv_w, a_log, dt_bias, norm_w, n_batch, seq_len):
    m = qraw.shape[0]
    tt = _tile(seq_len, 256)
    nt = seq_len // tt
    chunk = min(GDN_CHUNK, tt)
    assert tt % chunk == 0
    n_hist = SUBLANES
    pairs = min(GDN_HEADS // 2, max(1, GDN_ITEMS_PER_STEP // (tt // chunk)))
    w2 = 2 * pairs * LANES
    qkv_w = 3 * GDN_HEADS * GDN_DK
    cs = jnp.concatenate([jnp.zeros((n_batch, n_hist - (GDN_CONV - 1), qkv_w), F32), conv_state], axis=1)
    cw = jnp.concatenate([conv_w, jnp.zeros((n_hist - GDN_CONV, qkv_w), F32)], axis=0)
    alog = jnp.zeros((1, LANES), F32).at[0, GDN_HEADS:2 * GDN_HEADS].set(a_log)
    dtb = jnp.zeros((1, LANES), F32).at[0, GDN_HEADS:2 * GDN_HEADS].set(dt_bias)
    nblk = GDN_HEADS // (2 * pairs)
    row_spec = pl.BlockSpec((tt, w2), lambda b, hp, t: (b * nt + t, hp))
    cs_specs = [pl.BlockSpec((None, n_hist, w2), functools.partial(lambda b, hp, t, s: (b, 0, s * nblk + hp), s=s))
                for s in range(3)]
    cw_specs = [pl.BlockSpec((n_hist, w2), functools.partial(lambda b, hp, t, s: (0, s * nblk + hp), s=s))
                for s in range(3)]
    st_spec = pl.BlockSpec((None, 2 * pairs, GDN_DK, GDN_DK), lambda b, hp, t: (b, hp, 0, 0))
    vec_spec = pl.BlockSpec((1, LANES), lambda b, hp, t: (0, 0))
    return pl.pallas_call(
        functools.partial(_gdn_kernel, tt=tt, chunk=chunk),
        grid=(n_batch, nblk, nt),
        in_specs=[row_spec, row_spec, row_spec, row_spec,
                  pl.BlockSpec((tt, LANES), lambda b, hp, t: (b * nt + t, 0)),
                  *cs_specs, *cw_specs, st_spec, vec_spec, vec_spec, vec_spec],
        out_specs=[row_spec, st_spec],
        out_shape=[jax.ShapeDtypeStruct((m, GDN_HEADS * GDN_DK), BF16),
                   jax.ShapeDtypeStruct(s0.shape, F32)],
        scratch_shapes=[pltpu.VMEM((2 * pairs, GDN_DK, GDN_DK), F32),
                        pltpu.VMEM((3, n_hist, w2), F32),
                        pltpu.VMEM((3, n_hist + tt, w2), F32)],
        compiler_params=_cparams(("parallel", "parallel", "arbitrary")),
        name="gated_deltanet",
    )(qraw, kraw, vraw, z, ba, cs, cs, cs, cw, cw, cw, s0, alog, dtb, norm_w.reshape(1, LANES))


def _lambda(lq1_ref, lk1_ref, lq2_ref, lk2_ref):
    s1 = jnp.sum(lq1_ref[...] * lk1_ref[...], axis=-1, keepdims=True)
    s2 = jnp.sum(lq2_ref[...] * lk2_ref[...], axis=-1, keepdims=True)
    return jnp.exp(s1) - jnp.exp(s2) + DA_LAMBDA_INIT


def _flash_kernel(q_ref, k_ref, v_ref, lq1_ref, lk1_ref, lq2_ref, lk2_ref, sw_ref, o_ref,
                  *, tq, tk):
    qi = pl.program_id(2)
    q = q_ref[...] * (DA_DK ** -0.5)
    lane = lax.broadcasted_iota(jnp.int32, q.shape, 1)
    q_maps = (jnp.where(lane < DA_DK, q, 0.0).astype(BF16), jnp.where(lane >= DA_DK, q, 0.0).astype(BF16))

    def block(ki, carry, diagonal):
        start = pl.multiple_of(ki * tk, tk)
        k = k_ref[pl.ds(start, tk), :].astype(BF16)
        v = v_ref[pl.ds(start, tk), :].astype(BF16)
        if diagonal:
            visible = (lax.broadcasted_iota(jnp.int32, (tq, tk), 1) <= lax.broadcasted_iota(jnp.int32, (tq, tk), 0))
        new = []
        for mp in range(2):
            m_old, l_old, acc_old = carry[mp]
            s = lax.dot_general(q_maps[mp], k, (((1,), (1,)), ((), ())), preferred_element_type=F32)
            if diagonal:
                s = jnp.where(visible, s, -jnp.inf)
            m_new = jnp.maximum(m_old, jnp.max(s, axis=-1, keepdims=True))
            alpha = jnp.exp(m_old - m_new)
            p = jnp.exp(s - m_new)
            l_new = alpha * l_old + jnp.sum(p, axis=-1, keepdims=True)
            acc_new = alpha * acc_old + jnp.dot(p.astype(BF16), v, preferred_element_type=F32)
            new.append((m_new, l_new, acc_new))
        return tuple(new)

    init = tuple((jnp.full((tq, 1), -jnp.inf, F32), jnp.zeros((tq, 1), F32), jnp.zeros((tq, LANES), F32))
                 for _ in range(2))
    carry = lax.fori_loop(0, qi, lambda ki, c: block(ki, c, False), init)
    (_, l0, acc0), (_, l1, acc1) = block(qi, carry, True)
    lam = _lambda(lq1_ref, lk1_ref, lq2_ref, lk2_ref)
    o = acc0 / l0 - lam * (acc1 / l1)
    o = _rms(o) * sw_ref[...] * (1.0 - DA_LAMBDA_INIT)
    o_ref[...] = o.astype(o_ref.dtype)


def _diff_attention_prompt(q, k, v, lq1, lk1, lq2, lk2, subln_w, n_batch, seq_len):
    m = q.shape[0]
    tq = _tile(seq_len, 512)
    tk = tq
    nq = seq_len // tq
    vec = lambda a: a.reshape(1, -1)
    vspec = lambda n: pl.BlockSpec((1, n), lambda b, h, qi: (0, 0))
    kv_spec = pl.BlockSpec((seq_len, LANES), lambda b, h, qi: (b, h))
    return pl.pallas_call(
        functools.partial(_flash_kernel, tq=tq, tk=tk),
        grid=(n_batch, DA_HEADS, nq),
        in_specs=[pl.BlockSpec((tq, LANES), lambda b, h, qi: (b * nq + qi, h)), kv_spec, kv_spec,
                  vspec(DA_DK), vspec(DA_DK), vspec(DA_DK), vspec(DA_DK), vspec(DA_DV)],
        out_specs=pl.BlockSpec((tq, LANES), lambda b, h, qi: (b * nq + qi, h)),
        out_shape=jax.ShapeDtypeStruct((m, DA_HEADS * DA_DV), BF16),
        compiler_params=_cparams(("parallel", "parallel", "arbitrary")),
        name="diff_attn_prompt",
    )(q, k, v, vec(lq1), vec(lk1), vec(lq2), vec(lk2), vec(subln_w))


def _paged_attn_kernel(pt_ref, q_ref, kn_ref, vn_ref, *refs, n_group, ls):
    k_refs = refs[:n_group]
    v_refs = refs[n_group:2 * n_group]
    lq1_ref, lk1_ref, lq2_ref, lk2_ref, sw_ref, o_ref, qs_scr, m_scr, l_scr, acc_scr = refs[2 * n_group:]
    p_idx = pl.program_id(1)
    n_steps = pl.num_programs(1)
    n_rows = 2 * ls

    @pl.when(p_idx == 0)
    def _():
        q = q_ref[...] * (DA_DK ** -0.5)
        q2 = jnp.concatenate([q, q], axis=0)
        r = lax.broadcasted_iota(jnp.int32, q2.shape, 0)
        lane = lax.broadcasted_iota(jnp.int32, q2.shape, 1)
        keep = (r // ls) == ((lane % LANES) // DA_DK)
        qs_scr[...] = jnp.where(keep, q2, 0.0).astype(BF16)
        m_scr[...] = jnp.full(m_scr.shape, -jnp.inf, F32)
        l_scr[...] = jnp.zeros(l_scr.shape, F32)
        acc_scr[...] = jnp.zeros(acc_scr.shape, F32)

    def update(k_heads, v_heads, visible):
        s = jnp.concatenate(
            [lax.dot_general(qs_scr[:, h * LANES:(h + 1) * LANES], k_heads[h], (((1,), (1,)), ((), ())),
                             preferred_element_type=F32) for h in range(DA_HEADS)], axis=0)
        if visible is not None:
            s = jnp.where(visible, s, -jnp.inf)
        m_old = m_scr[...]
        m_new = jnp.maximum(m_old, jnp.max(s, axis=-1, keepdims=True))
        alpha = jnp.exp(m_old - m_new)
        p = jnp.exp(s - m_new)
        l_scr[...] = alpha * l_scr[...] + jnp.sum(p, axis=-1, keepdims=True)
        p = p.astype(BF16)
        pv = jnp.concatenate([jnp.dot(p[h * n_rows:(h + 1) * n_rows], v_heads[h], preferred_element_type=F32)
                              for h in range(DA_HEADS)], axis=0)
        acc_scr[...] = alpha * acc_scr[...] + pv
        m_scr[...] = m_new

    update([jnp.concatenate([kr[pl.ds(h, PAGE_SIZE, stride=DA_HEADS), :].astype(BF16) for kr in k_refs], axis=0)
            for h in range(DA_HEADS)],
           [jnp.concatenate([vr[pl.ds(h, PAGE_SIZE, stride=DA_HEADS), :].astype(BF16) for vr in v_refs], axis=0)
            for h in range(DA_HEADS)], None)

    @pl.when(p_idx == n_steps - 1)
    def _():
        lam = _lambda(lq1_ref, lk1_ref, lq2_ref, lk2_ref)
        pad = jnp.zeros((LANES - ls, DA_HEADS * LANES), F32)
        kn = jnp.concatenate([kn_ref[...], pad], axis=0).astype(BF16)
        vn = jnp.concatenate([vn_ref[...], pad], axis=0).astype(BF16)
        r = lax.broadcasted_iota(jnp.int32, (DA_HEADS * n_rows, LANES), 0)
        c = lax.broadcasted_iota(jnp.int32, (DA_HEADS * n_rows, LANES), 1)
        update([kn[:, h * LANES:(h + 1) * LANES] for h in range(DA_HEADS)],
               [vn[:, h * LANES:(h + 1) * LANES] for h in range(DA_HEADS)], c <= (r % ls))
        acc = acc_scr[...] / l_scr[...]
        for h in range(DA_HEADS):
            o = acc[h * n_rows:h * n_rows + ls] - lam * acc[h * n_rows + ls:(h + 1) * n_rows]
            o = _rms(o) * sw_ref[...] * (1.0 - DA_LAMBDA_INIT)
            o_ref[:, h * LANES:(h + 1) * LANES] = o.astype(o_ref.dtype)


def _diff_attention_sample(q, k_new, v_new, cache_k, cache_v, page_table, lq1, lk1, lq2, lk2, subln_w, n_batch, ls):
    n_pool = cache_k.shape[0]
    width = DA_HEADS * LANES
    ck = cache_k.reshape(n_pool, PAGE_SIZE * DA_HEADS, LANES)
    cv = cache_v.reshape(n_pool, PAGE_SIZE * DA_HEADS, LANES)
    n_pages = page_table.shape[1]
    n_group = next(g for g in (PAGES_PER_STEP, 4, 2, 1) if n_pages % g == 0)
    n_steps = n_pages // n_group
    row_spec = pl.BlockSpec((ls, width), lambda b, p, pt: (b, 0))
    page_specs = [pl.BlockSpec((None, PAGE_SIZE * DA_HEADS, LANES),
                               functools.partial(lambda b, p, pt, g: (pt[b, p * n_group + g], 0, 0), g=g))
                  for g in range(n_group)]
    vec = lambda a: a.reshape(1, -1)
    vspec = lambda n: pl.BlockSpec((1, n), lambda b, p, pt: (0, 0))
    grid_spec = pltpu.PrefetchScalarGridSpec(
        num_scalar_prefetch=1,
        grid=(n_batch, n_steps),
        in_specs=[row_spec, row_spec, row_spec, *page_specs, *page_specs,
                  vspec(DA_DK), vspec(DA_DK), vspec(DA_DK), vspec(DA_DK), vspec(DA_DV)],
        out_specs=row_spec,
        scratch_shapes=[pltpu.VMEM((2 * ls, width), BF16), pltpu.VMEM((DA_HEADS * 2 * ls, 1), F32),
                        pltpu.VMEM((DA_HEADS * 2 * ls, 1), F32), pltpu.VMEM((DA_HEADS * 2 * ls, LANES), F32)],
    )
    return pl.pallas_call(
        functools.partial(_paged_attn_kernel, n_group=n_group, ls=ls),
        grid_spec=grid_spec,
        out_shape=jax.ShapeDtypeStruct((n_batch * ls, width), BF16),
        compiler_params=_cparams(("parallel", "arbitrary")),
        name="diff_attn_sample",
    )(page_table, q, k_new, v_new, *([ck] * n_group), *([cv] * n_group),
      vec(lq1), vec(lk1), vec(lq2), vec(lk2), vec(subln_w))


def _proj_res_kernel(*refs, n_parts):
    x_ref, gt_ref = refs[:2]
    a_refs = refs[2:2 + n_parts]
    w_refs = refs[2 + n_parts:2 + 2 * n_parts]
    o_ref = refs[2 + 2 * n_parts]
    acc = jnp.dot(a_refs[0][...], w_refs[0][...], preferred_element_type=F32)
    for p in range(1, n_parts):
        acc = acc + jnp.dot(a_refs[p][...], w_refs[p][...], preferred_element_type=F32)
    o_ref[...] = x_ref[...] + gt_ref[...] * acc


def _proj_residual(x, gt, parts, ws, seq_len, tm):
    m, n = x.shape
    tn = _tile(n, 512)
    gtm = _Mod(gt, seq_len, tm)
    in_specs = [pl.BlockSpec((tm, tn), lambda i, j: (i, j)), gtm.spec(tn, lambda i, j: j)]
    for a in parts:
        in_specs.append(pl.BlockSpec((tm, a.shape[1]), lambda i, j: (i, 0)))
    for w in ws:
        in_specs.append(pl.BlockSpec((w.shape[0], tn), lambda i, j: (0, j)))
    return pl.pallas_call(
        functools.partial(_proj_res_kernel, n_parts=len(parts)),
        grid=(m // tm, n // tn),
        in_specs=in_specs,
        out_specs=pl.BlockSpec((tm, tn), lambda i, j: (i, j)),
        out_shape=jax.ShapeDtypeStruct((m, n), F32),
        compiler_params=_cparams(("parallel", "arbitrary")),
        name="proj_residual",
    )(x, gtm.arr, *parts, *ws)


def _norm_ffn_kernel(x_ref, sh_ref, sc_ref, gt_ref, g_ref, wg_ref, wu_ref, wd_ref, o_ref, h_ref):
    f = pl.program_id(1)
    nf = pl.num_programs(1)

    @pl.when(f == 0)
    def _():
        h = _rms(x_ref[...]) * g_ref[...] * (1.0 + sc_ref[...]) + sh_ref[...]
        h_ref[...] = h.astype(BF16)
        o_ref[...] = jnp.zeros(o_ref.shape, F32)

    h = h_ref[...]
    gate = jnp.dot(h, wg_ref[...], preferred_element_type=F32)
    up = jnp.dot(h, wu_ref[...], preferred_element_type=F32)
    act = (_silu(gate) * up).astype(BF16)
    o_ref[...] += jnp.dot(act, wd_ref[...], preferred_element_type=F32)

    @pl.when(f == nf - 1)
    def _():
        o_ref[...] = x_ref[...] + gt_ref[...] * o_ref[...]


def _norm_ffn(x, sh, sc, gt, g, wg, wu, wd, seq_len, tm):
    m, d = x.shape
    dff = wg.shape[1]
    tf = _tile(dff, FFN_TILE)
    tm = _tile(tm, 512)
    shm, scm, gtm = _Mod(sh, seq_len, tm), _Mod(sc, seq_len, tm), _Mod(gt, seq_len, tm)
    return pl.pallas_call(
        _norm_ffn_kernel,
        grid=(m // tm, dff // tf),
        in_specs=[pl.BlockSpec((tm, d), lambda i, f: (i, 0)), shm.spec(), scm.spec(), gtm.spec(),
                  pl.BlockSpec((1, d), lambda i, f: (0, 0)),
                  pl.BlockSpec((d, tf), lambda i, f: (0, f)),
                  pl.BlockSpec((d, tf), lambda i, f: (0, f)),
                  pl.BlockSpec((tf, d), lambda i, f: (f, 0))],
        out_specs=pl.BlockSpec((tm, d), lambda i, f: (i, 0)),
        out_shape=jax.ShapeDtypeStruct((m, d), F32),
        scratch_shapes=[pltpu.VMEM((tm, d), BF16)],
        compiler_params=_cparams(("parallel", "arbitrary")),
        name="norm_ffn",
    )(x, shm.arr, scm.arr, gtm.arr, g.reshape(1, d), wg, wu, wd)


def _rwkv_proj_kernel(x_ref, xp_ref, sh_ref, sc_ref, shift_ref, g_ref, mu_ref, w1_ref, a1_ref, g1_ref,
                      wr_ref, wk_ref, wv_ref, w2_ref, a2_ref, g2_ref, w0_ref, a0_ref,
                      r_ref, k_ref, v_ref, lw_ref, a_ref, gg_ref,
                      mix_scr, sw_scr, sa_scr, sg_scr, *, tm, seq_len):
    i = pl.program_id(0)
    j = pl.program_id(1)

    @pl.when(j == 0)
    def _():
        g = g_ref[...]
        scale = 1.0 + sc_ref[...]
        shift = sh_ref[...]
        h = _rms(x_ref[...]) * g * scale + shift
        row = lax.broadcasted_iota(jnp.int32, h.shape, 0)
        rolled = pltpu.roll(h, 1, 0)
        if seq_len % tm == 0:
            sc_row = scale[0:1] if scale.shape[0] > 1 else scale
            sh_row = shift[0:1] if shift.shape[0] > 1 else shift
            hp = _rms(xp_ref[...]) * g * sc_row + sh_row
            first = jnp.where((i % (seq_len // tm)) == 0, shift_ref[...], hp[SUBLANES - 1:SUBLANES])
            prev = jnp.where(row == 0, first, rolled)
        else:
            prev = jnp.where((row % seq_len) == 0, shift_ref[...], rolled)
        xx = prev - h
        mu = mu_ref[...]
        for n in range(6):
            mix_scr[n] = (h + xx * mu[n:n + 1]).astype(BF16)
        sw_scr[...] = jnp.tanh(jnp.dot(mix_scr[1], w1_ref[...], preferred_element_type=F32)).astype(BF16)
        sa_scr[...] = jnp.dot(mix_scr[4], a1_ref[...], preferred_element_type=F32).astype(BF16)
        sg_scr[...] = _sigmoid(jnp.dot(mix_scr[5], g1_ref[...], preferred_element_type=F32)).astype(BF16)

    r_ref[...] = jnp.dot(mix_scr[0], wr_ref[...], preferred_element_type=F32)
    k_ref[...] = jnp.dot(mix_scr[2], wk_ref[...], preferred_element_type=F32)
    v_ref[...] = jnp.dot(mix_scr[3], wv_ref[...], preferred_element_type=F32)
    wl = w0_ref[...] + jnp.dot(sw_scr[...], w2_ref[...], preferred_element_type=F32)
    lw_ref[...] = -jnp.exp(-_softplus(-wl) - 0.5)
    a_ref[...] = _sigmoid(a0_ref[...] + jnp.dot(sa_scr[...], a2_ref[...], preferred_element_type=F32))
    gg_ref[...] = jnp.dot(sg_scr[...], g2_ref[...], preferred_element_type=F32)


def _rwkv_proj(x, sh, sc, shift_state, g, mu, w1, a1, g1, wr, wk, wv, w2, a2, g2, w0, a0, n_batch, seq_len, tm):
    m, d = x.shape
    tn = _tile(d, 256)
    shm, scm = _Mod(sh, seq_len, tm), _Mod(sc, seq_len, tm)
    stm = _Mod(shift_state, seq_len, tm)
    rows8 = tm // SUBLANES
    full = lambda a: pl.BlockSpec(a.shape, lambda i, j: (0,) * a.ndim)
    colw = lambda a: pl.BlockSpec((a.shape[0], tn), lambda i, j: (0, j))
    out_spec = pl.BlockSpec((tm, tn), lambda i, j: (i, j))
    g2d, w0r, a0r = g.reshape(1, d), w0.reshape(1, d), a0.reshape(1, d)
    return pl.pallas_call(
        functools.partial(_rwkv_proj_kernel, tm=tm, seq_len=seq_len),
        grid=(m // tm, d // tn),
        in_specs=[pl.BlockSpec((tm, d), lambda i, j: (i, 0)),
                  pl.BlockSpec((SUBLANES, d), lambda i, j: (jnp.maximum(i * rows8 - 1, 0), 0)),
                  shm.spec(), scm.spec(), stm.spec(), full(g2d), full(mu), full(w1), full(a1), full(g1),
                  colw(wr), colw(wk), colw(wv), colw(w2), colw(a2), colw(g2), colw(w0r), colw(a0r)],
        out_specs=[out_spec] * 6,
        out_shape=[jax.ShapeDtypeStruct((m, d), F32)] * 6,
        scratch_shapes=[pltpu.VMEM((6, tm, d), BF16), pltpu.VMEM((tm, w1.shape[1]), BF16),
                        pltpu.VMEM((tm, a1.shape[1]), BF16), pltpu.VMEM((tm, g1.shape[1]), BF16)],
        compiler_params=_cparams(("parallel", "arbitrary")),
        name="rwkv_proj",
    )(x, x, shm.arr, scm.arr, stm.arr, g2d, mu, w1, a1, g1, wr, wk, wv, w2, a2, g2, w0r, a0r)


def _norm_mod_kernel(x_ref, sh_ref, sc_ref, g_ref, o_ref):
    o_ref[...] = _rms(x_ref[...]) * g_ref[...] * (1.0 + sc_ref[...]) + sh_ref[...]


def _norm_mod(x, sh, sc, g):
    r, d = x.shape
    spec = pl.BlockSpec((r, d), lambda i: (0, 0))
    return pl.pallas_call(
        _norm_mod_kernel, grid=(1,),
        in_specs=[spec, spec, spec, pl.BlockSpec((1, d), lambda i: (0, 0))],
        out_specs=spec, out_shape=jax.ShapeDtypeStruct((r, d), F32),
        compiler_params=_cparams(("arbitrary",)), name="norm_mod",
    )(x, sh, sc, g.reshape(1, d))


def _wkv_kernel(r_ref, k_ref, v_ref, lw_ref, a_ref, g_ref, kk_ref, ka_ref, rk_ref, lnw_ref, lnb_ref, s0_ref,
                o_ref, sout_ref, z_scr, *, tt, chunk):
    t = pl.program_id(2)
    nt = pl.num_programs(2)

    @pl.when(t == 0)
    def _():
        z_scr[...] = s0_ref[...]

    c = chunk
    n2 = 2 * c
    n_chunks = max(tt // c, 1)
    lane1 = lax.broadcasted_iota(jnp.int32, (1, LANES), 1)
    head_a = lane1 < RW_HEAD
    rr = lax.broadcasted_iota(jnp.int32, (LANES, LANES), 0)
    cc = lax.broadcasted_iota(jnp.int32, (LANES, LANES), 1)
    seg_ones = ((rr // RW_HEAD) == (cc // RW_HEAD)).astype(BF16)
    row = lax.broadcasted_iota(jnp.int32, (n2, n2), 0)
    col = lax.broadcasted_iota(jnp.int32, (n2, n2), 1)
    same = (row // c) == (col // c)
    low_incl = same & (col <= row)
    low_strict = same & (col < row)
    tri = (lax.broadcasted_iota(jnp.int32, (c, c), 1) <= lax.broadcasted_iota(jnp.int32, (c, c), 0)).astype(BF16)
    n_pairs = r_ref.shape[1] // LANES

    def seg_sum(x):
        return jnp.dot(x.astype(BF16), seg_ones, preferred_element_type=F32)

    def stack(x):
        return jnp.concatenate([jnp.where(head_a, x, 0.0), jnp.where(head_a, 0.0, x)], axis=0)

    pt_s, rt_s, ch_s, kh_s, v_s, w_end, l_pc, l_pk, a_rc, a_rk = ([] for _ in range(10))
    bonus, gate = [], []
    for p in range(n_pairs):
        ls = slice(p * LANES, (p + 1) * LANES)

        def load(ref):
            x = ref[:, ls]
            if tt < c:
                x = jnp.concatenate([x, jnp.zeros((c - tt, LANES), F32)], axis=0)
            return x

        r_all, k_all, v_all, lw_all, a_all = (load(ref) for ref in (r_ref, k_ref, v_ref, lw_ref, a_ref))
        gate.append(load(g_ref))
        kk_raw = k_all * kk_ref[:, ls]
        kk_all = kk_raw * lax.rsqrt(seg_sum(kk_raw * kk_raw) + 1e-12)
        kmod_all = k_all * (1.0 + (a_all - 1.0) * ka_ref[:, ls])
        bonus.append(seg_sum(r_all * kmod_all * rk_ref[:, ls]) * v_all)
        lw_hi = lw_all.astype(BF16)
        lw_lo = (lw_all - lw_hi.astype(F32)).astype(BF16)
        for ci in range(n_chunks):
            sl = slice(ci * c, (ci + 1) * c)
            cum = (jnp.dot(tri, lw_hi[sl], preferred_element_type=F32)
                   + jnp.dot(tri, lw_lo[sl], preferred_element_type=F32))
            cum_end = cum[c - 1:c, :]
            e_neg = jnp.exp(-cum)
            e_end = jnp.exp(cum_end - cum)
            pvec = -kk_all[sl]
            cvec = kk_all[sl] * a_all[sl]
            p_s = stack(pvec * jnp.exp(cum - lw_all[sl]))
            r_s = stack(r_all[sl] * jnp.exp(cum))
            gram = _dot_nt(jnp.concatenate([p_s, r_s], axis=0),
                           jnp.concatenate([stack(cvec * e_neg), stack(kmod_all[sl] * e_neg)], axis=0))
            l_pc.append(jnp.where(low_strict, gram[0:n2, 0:n2], 0.0))
            l_pk.append(jnp.where(low_strict, gram[0:n2, n2:2 * n2], 0.0))
            a_rc.append(jnp.where(low_incl, gram[n2:2 * n2, 0:n2], 0.0))
            a_rk.append(jnp.where(low_incl, gram[n2:2 * n2, n2:2 * n2], 0.0))
            pt_s.append(p_s)
            rt_s.append(r_s)
            ch_s.append(stack(cvec * e_end))
            kh_s.append(stack(kmod_all[sl] * e_end))
            v_s.append(stack(v_all[sl]))
            w_end.append(jnp.exp(cum_end))
    t_inv = _unit_lower_inverses([-x for x in l_pc], c)
    rng = range(n_pairs * n_chunks)
    lv = [_dot(l_pk[i], v_s[i]) for i in rng]
    m1 = [_dot(t_inv[i], pt_s[i]) for i in rng]
    b1 = [_dot(t_inv[i], lv[i]) for i in rng]
    m2 = [rt_s[i] + _dot(a_rc[i], m1[i]) for i in rng]
    bv = [jnp.concatenate([b1[i], v_s[i]], axis=0) for i in rng]
    b2 = [_dot(jnp.concatenate([a_rc[i], a_rk[i]], axis=1), bv[i]) for i in rng]
    m3 = [_dot_tn(ch_s[i], m1[i]) for i in rng]
    b3 = [_dot_tn(bv[i], jnp.concatenate([ch_s[i], kh_s[i]], axis=0)) for i in rng]

    for p in range(n_pairs):
        ls = slice(p * LANES, (p + 1) * LANES)
        z = z_scr[p]
        ys = []
        for ci in range(n_chunks):
            i = p * n_chunks + ci
            y_s = _dot_nt(m2[i], z) + b2[i]
            z = z * w_end[i] + _dot_nt(z, m3[i]) + b3[i]
            ys.append(y_s[0:c] + y_s[c:n2])
        z_scr[p] = z
        y = jnp.concatenate(ys, axis=0) if n_chunks > 1 else ys[0]
        mean = seg_sum(y) * (1.0 / RW_HEAD)
        dev = y - mean
        var = seg_sum(dev * dev) * (1.0 / RW_HEAD)
        yn = dev * lax.rsqrt(var + RW_GN_EPS) * lnw_ref[:, ls] + lnb_ref[:, ls]
        out = (yn + bonus[p]) * gate[p]
        o_ref[:, ls] = out[0:tt].astype(o_ref.dtype)

    @pl.when(t == nt - 1)
    def _():
        sout_ref[...] = z_scr[...]


def _wkv7(r, k, v, lw, a, g, k_k, k_a, r_k, ln_w, ln_b, s0_pairs, n_batch, seq_len):
    m, d = r.shape
    n_pairs = d // LANES
    tt = _tile(seq_len, 512)
    nt = seq_len // tt
    chunk = min(RW_CHUNK, max(seq_len, SUBLANES))
    group = min(n_pairs, max(1, WKV_ITEMS_PER_STEP // max(tt // chunk, 1)))
    row_spec = pl.BlockSpec((tt, group * LANES), lambda b, hp, t: (b * nt + t, hp))
    vec_spec = pl.BlockSpec((1, group * LANES), lambda b, hp, t: (0, hp))
    st_spec = pl.BlockSpec((None, group, LANES, LANES), lambda b, hp, t: (b, hp, 0, 0))
    vec = lambda x: x.reshape(1, d)
    return pl.pallas_call(
        functools.partial(_wkv_kernel, tt=tt, chunk=chunk),
        grid=(n_batch, n_pairs // group, nt),
        in_specs=[row_spec] * 6 + [vec_spec] * 5 + [st_spec],
        out_specs=[row_spec, st_spec],
        out_shape=[jax.ShapeDtypeStruct((m, d), BF16), jax.ShapeDtypeStruct(s0_pairs.shape, F32)],
        scratch_shapes=[pltpu.VMEM((group, LANES, LANES), F32)],
        compiler_params=_cparams(("parallel", "parallel", "arbitrary")),
        name="wkv7",
    )(r, k, v, lw, a, g, vec(k_k), vec(k_a), vec(r_k), vec(ln_w), vec(ln_b), s0_pairs)


def _pair_states(s):
    b, h, n, _ = s.shape
    s2 = s.reshape(b, h // 2, 2, n, n)
    zero = jnp.zeros((b, h // 2, n, n), s.dtype)
    top = jnp.concatenate([s2[:, :, 0], zero], axis=-1)
    bot = jnp.concatenate([zero, s2[:, :, 1]], axis=-1)
    return jnp.concatenate([top, bot], axis=-2)


def _unpair_states(sp):
    n = RW_HEAD
    b, hp = sp.shape[:2]
    return jnp.stack([sp[:, :, :n, :n], sp[:, :, n:, n:]], axis=2).reshape(b, 2 * hp, n, n)


def _router_kernel(x_ref, sh_ref, sc_ref, g_ref, rw_ref, rb_ref, h_ref, idx_ref, gate_ref):
    h = _rms(x_ref[...]) * g_ref[...] * (1.0 + sc_ref[...]) + sh_ref[...]
    h_ref[...] = h
    logits = jnp.dot(h, rw_ref[...], preferred_element_type=F32, precision=lax.Precision.HIGHEST) + rb_ref[...]
    lane = lax.broadcasted_iota(jnp.int32, logits.shape, 1)
    lane_f = lane.astype(F32)
    neg = -jnp.inf
    l1 = jnp.where(lane < N_EXPERTS, logits, neg)
    m1 = jnp.max(l1, axis=-1, keepdims=True)
    i1 = jnp.min(jnp.where(l1 == m1, lane_f, float(LANES)), axis=-1, keepdims=True)
    l2 = jnp.where(lane_f == i1, neg, l1)
    m2 = jnp.max(l2, axis=-1, keepdims=True)
    i2 = jnp.min(jnp.where(l2 == m2, lane_f, float(LANES)), axis=-1, keepdims=True)
    e = jnp.exp(m2 - m1)
    g0 = 1.0 / (1.0 + e)
    g1 = e / (1.0 + e)
    idx_ref[...] = jnp.where(lane == 0, i1, jnp.where(lane == 1, i2, 0.0)).astype(jnp.int32)
    gate_ref[...] = jnp.where(lane == 0, g0, jnp.where(lane == 1, g1, 0.0))


def _router(x, sh, sc, g, rw_pad, rb_pad, seq_len, tm):
    m, d = x.shape
    shm, scm = _Mod(sh, seq_len, tm), _Mod(sc, seq_len, tm)
    return pl.pallas_call(
        _router_kernel,
        grid=(m // tm,),
        in_specs=[pl.BlockSpec((tm, d), lambda i: (i, 0)), shm.spec1(), scm.spec1(),
                  pl.BlockSpec((1, d), lambda i: (0, 0)),
                  pl.BlockSpec((d, LANES), lambda i: (0, 0)), pl.BlockSpec((1, LANES), lambda i: (0, 0))],
        out_specs=[pl.BlockSpec((tm, d), lambda i: (i, 0)), pl.BlockSpec((tm, LANES), lambda i: (i, 0)),
                   pl.BlockSpec((tm, LANES), lambda i: (i, 0))],
        out_shape=[jax.ShapeDtypeStruct((m, d), F32), jax.ShapeDtypeStruct((m, LANES), jnp.int32),
                   jax.ShapeDtypeStruct((m, LANES), F32)],
        compiler_params=_cparams(("parallel",)),
        name="moe_router",
    )(x, shm.arr, scm.arr, g.reshape(1, d), rw_pad, rb_pad)


def _gather_rows_kernel(idx_ref, src_ref, o_ref, sem, *, rows):
    i = pl.program_id(0)

    def row_copy(r):
        return pltpu.make_async_copy(src_ref.at[pl.ds(idx_ref[i * rows + r], 1), :], o_ref.at[pl.ds(r, 1), :], sem)

    def start(r, carry):
        row_copy(r).start()
        return carry

    def wait(r, carry):
        row_copy(r).wait()
        return carry

    lax.fori_loop(0, rows, start, 0, unroll=GATHER_UNROLL)
    lax.fori_loop(0, rows, wait, 0, unroll=GATHER_UNROLL)


def _gather_rows(src, row_idx, rows):
    n_rows = row_idx.shape[0]
    d = src.shape[1]
    assert n_rows % rows == 0
    grid_spec = pltpu.PrefetchScalarGridSpec(
        num_scalar_prefetch=1,
        grid=(n_rows // rows,),
        in_specs=[pl.BlockSpec(memory_space=pl.ANY)],
        out_specs=pl.BlockSpec((rows, d), lambda i, idx: (i, 0)),
        scratch_shapes=[pltpu.SemaphoreType.DMA(())],
    )
    return pl.pallas_call(
        functools.partial(_gather_rows_kernel, rows=rows),
        grid_spec=grid_spec,
        out_shape=jax.ShapeDtypeStruct((n_rows, d), src.dtype),
        compiler_params=_cparams(("arbitrary",)),
        name="moe_gather",
    )(row_idx, src)


def _moe_ffn_kernel(te_ref, na_ref, x_ref, wg_ref, wu_ref, wd_ref, o_ref, h_ref):
    i = pl.program_id(0)
    f = pl.program_id(1)
    active = i < na_ref[0]

    @pl.when(f == 0)
    def _():
        h_ref[...] = x_ref[...].astype(BF16)
        o_ref[...] = jnp.zeros(o_ref.shape, F32)

    @pl.when(active)
    def _():
        h = h_ref[...]
        gate = jnp.dot(h, wg_ref[...], preferred_element_type=F32)
        up = jnp.dot(h, wu_ref[...], preferred_element_type=F32)
        act = (_silu(gate) * up).astype(BF16)
        o_ref[...] += jnp.dot(act, wd_ref[...], preferred_element_type=F32)


def _moe_ffn(xs, tile_expert, n_active, wg, wu, wd, tm):
    n_rows, d = xs.shape
    dff = wg.shape[2]
    tf = _tile(dff, FFN_TILE)
    nf = dff // tf

    def fcol(i, f, te, na):
        return jnp.where(i < na[0], f, nf - 1)

    grid_spec = pltpu.PrefetchScalarGridSpec(
        num_scalar_prefetch=2,
        grid=(n_rows // tm, nf),
        in_specs=[pl.BlockSpec((tm, d), lambda i, f, te, na: (i, 0)),
                  pl.BlockSpec((None, d, tf), lambda i, f, te, na: (te[i], 0, fcol(i, f, te, na))),
                  pl.BlockSpec((None, d, tf), lambda i, f, te, na: (te[i], 0, fcol(i, f, te, na))),
                  pl.BlockSpec((None, tf, d), lambda i, f, te, na: (te[i], fcol(i, f, te, na), 0))],
        out_specs=pl.BlockSpec((tm, d), lambda i, f, te, na: (i, 0)),
        scratch_shapes=[pltpu.VMEM((tm, d), BF16)],
    )
    return pl.pallas_call(
        _moe_ffn_kernel,
        grid_spec=grid_spec,
        out_shape=jax.ShapeDtypeStruct((n_rows, d), F32),
        compiler_params=_cparams(("parallel", "arbitrary")),
        name="moe_ffn",
    )(tile_expert, n_active, xs, wg, wu, wd)


def _combine_kernel(x_ref, gt_ref, gate_ref, g_ref, y0_ref, y1_ref, o_ref):
    gates = gate_ref[...]
    y = gates[:, 0:1] * y0_ref[...] + gates[:, 1:2] * y1_ref[...]
    x = x_ref[...] + gt_ref[...] * y
    o_ref[...] = _rms(x) * g_ref[...]


def _combine_final(x, gt, gates, y0, y1, g, row0, seq_len, tm):
    m, d = x.shape
    gtm = _Mod(gt, seq_len, tm)
    assert row0 % tm == 0
    blk0 = row0 // tm
    y_spec = pl.BlockSpec((tm, d), lambda i: (blk0 + i, 0))
    return pl.pallas_call(
        _combine_kernel,
        grid=(m // tm,),
        in_specs=[pl.BlockSpec((tm, d), lambda i: (i, 0)), gtm.spec1(),
                  pl.BlockSpec((tm, LANES), lambda i: (i, 0)),
                  pl.BlockSpec((1, d), lambda i: (0, 0)), y_spec, y_spec],
        out_specs=pl.BlockSpec((tm, d), lambda i: (i, 0)),
        out_shape=jax.ShapeDtypeStruct((m, d), F32),
        compiler_params=_cparams(("parallel",)),
        name="moe_combine_final",
    )(x, gtm.arr, gates, g.reshape(1, d), y0, y1)


def _split_mod(mod):
    return [mod[:, n * D_MODEL:(n + 1) * D_MODEL] for n in range(6)]


def _layer_even(x, mod, pos, conv_state, gdn_state, paged, p, n_batch, seq_len, tm):
    sh1, sc1, gt1, sh2, sc2, gt2 = mod
    m = x.shape[0]
    if seq_len % tm == 0:
        tabs = _rope_tables(pos)
    else:
        tabs = tuple(jnp.tile(t, (n_batch, 1)) for t in _rope_tables(pos))
    gq, gk, gv, gz, ba = _norm_proj(x, sh1, sc1, p["norm_mix0"], p["w_gdn"], (False,) * 4, p["w_ba"], None,
                                    seq_len, tm)
    dq, dk, dv = _norm_proj(x, sh1, sc1, p["norm_mix0"], p["w_da"], (True, True, False), None, tabs, seq_len, tm)
    o_gdn, new_gdn = _gated_deltanet(gq, gk, gv, gz, ba, conv_state, gdn_state, p["gdn_conv_w"], p["gdn_a_log"],
                                     p["gdn_dt_bias"], p["gdn_norm_w"], n_batch, seq_len)
    lam_args = (p["da_lq1"], p["da_lk1"], p["da_lq2"], p["da_lk2"], p["da_subln_w"])
    if paged is None:
        o_da = _diff_attention_prompt(dq, dk, dv, *lam_args, n_batch, seq_len)
    else:
        o_da = _diff_attention_sample(dq, dk, dv, paged[0], paged[1], paged[2], *lam_args, n_batch, seq_len)
    x = _proj_residual(x, gt1, [o_gdn, o_da], p["w_out"], seq_len, tm)
    x = _norm_ffn(x, sh2, sc2, gt2, p["norm_ffn0"], p["ffn_wg"], p["ffn_wu"], p["ffn_wd"], seq_len, tm)
    raw = jnp.concatenate([gq.reshape(n_batch, seq_len, -1), gk.reshape(n_batch, seq_len, -1),
                           gv.reshape(n_batch, seq_len, -1)], axis=-1)
    ext = jnp.concatenate([conv_state, raw[:, max(seq_len - (GDN_CONV - 1), 0):]], axis=1)
    new_conv = ext[:, ext.shape[1] - (GDN_CONV - 1):]
    k_out = dk.reshape(n_batch, seq_len, DA_HEADS, 2 * DA_DK)
    v_out = dv.reshape(n_batch, seq_len, DA_HEADS, DA_DV)
    return x, k_out, v_out, new_conv, new_gdn


def _layer_odd_mix(x, mod, shift_state, wkv_state, p, n_batch, seq_len, tm):
    sh1, sc1, gt1 = mod[:3]
    d = x.shape[1]
    tm_rw = _tile(tm, 512)
    r, k, v, lw, a, g = _rwkv_proj(x, sh1, sc1, shift_state, p["norm_mix1"], p["rw_mu"], p["rw_w1"], p["rw_a1"],
                                   p["rw_g1"], p["rw_wr"], p["rw_wk"], p["rw_wv"], p["rw_w2"], p["rw_a2"],
                                   p["rw_g2"], p["rw_w0"], p["rw_a0"], n_batch, seq_len, tm_rw)
    yg, s_pairs = _wkv7(r, k, v, lw, a, g, p["rw_k_k"], p["rw_k_a"], p["rw_r_k"], p["rw_ln_w"], p["rw_ln_b"],
                        _pair_states(wkv_state), n_batch, seq_len)
    last = x.reshape(n_batch, seq_len, d)[:, seq_len - 1]
    new_shift = _norm_mod(last, sh1, sc1, p["norm_mix1"])
    x = _proj_residual(x, gt1, [yg], [p["rw_wo"]], seq_len, tm)
    return x, new_shift, _unpair_states(s_pairs)


def kernel(x_prompt, x_sample, cache_k, cache_v, state_gdn_conv, state_gdn, state_rwkv_shift, state_rwkv, page_table, c_prompt, c_sample, ada_w0, ada_b0, norm_mix0, w_in0, gdn_conv_w, gdn_a_log, gdn_dt_bias, gdn_norm_w, da_lq1, da_lk1, da_lq2, da_lk2, da_subln_w, w_out0, norm_ffn0, ffn_w_gate, ffn_w_up, ffn_w_down, ada_w1, ada_b1, norm_mix1, rw_mu, rw_w0, rw_w1, rw_w2, rw_a0, rw_a1, rw_a2, rw_g1, rw_g2, rw_k_k, rw_k_a, rw_r_k, rw_wr, rw_wk, rw_wv, rw_wo, rw_ln_w, rw_ln_b, norm_ffn1, moe_router_w, moe_router_b, moe_w_gate, moe_w_up, moe_w_down, norm_final):
    bp, lp, d = x_prompt.shape
    bs, ls, _ = x_sample.shape
    n_pages = page_table.shape[1]
    past_len = n_pages * PAGE_SIZE
    mp, ms = bp * lp, bs * ls
    tm_p = _tile(lp, 1024)
    tm_s = _tile(ms, 256)

    qk_w = GDN_HEADS * GDN_DK
    c0 = 3 * qk_w
    c1 = c0 + qk_w
    c2 = c1 + 2 * GDN_HEADS
    da_w = DA_HEADS * 2 * DA_DK
    bf = lambda w: w.astype(BF16)
    w_ba = jnp.concatenate([w_in0[:, c1:c2], jnp.zeros((d, LANES - 2 * GDN_HEADS), F32)], axis=1)
    lora_pad = lambda w, axis: jnp.pad(w, [(0, (-w.shape[a]) % LANES if a == axis else 0) for a in range(2)])
    p = {
        "norm_mix0": norm_mix0,
        "w_gdn": [bf(w_in0[:, n * qk_w:(n + 1) * qk_w]) for n in range(4)],
        "w_ba": bf(w_ba),
        "w_da": [bf(w_in0[:, c2 + n * da_w:c2 + (n + 1) * da_w]) for n in range(3)],
        "gdn_conv_w": gdn_conv_w, "gdn_a_log": gdn_a_log, "gdn_dt_bias": gdn_dt_bias, "gdn_norm_w": gdn_norm_w,
        "da_lq1": da_lq1, "da_lk1": da_lk1, "da_lq2": da_lq2, "da_lk2": da_lk2, "da_subln_w": da_subln_w,
        "w_out": [bf(w_out0[:qk_w]), bf(w_out0[qk_w:])],
        "norm_ffn0": norm_ffn0, "ffn_wg": bf(ffn_w_gate), "ffn_wu": bf(ffn_w_up), "ffn_wd": bf(ffn_w_down),
        "norm_mix1": norm_mix1, "rw_mu": rw_mu, "rw_w0": rw_w0, "rw_a0": rw_a0,
        "rw_w1": bf(lora_pad(rw_w1, 1)), "rw_w2": bf(lora_pad(rw_w2, 0)),
        "rw_a1": bf(lora_pad(rw_a1, 1)), "rw_a2": bf(lora_pad(rw_a2, 0)),
        "rw_g1": bf(rw_g1), "rw_g2": bf(rw_g2),
        "rw_wr": bf(rw_wr), "rw_wk": bf(rw_wk), "rw_wv": bf(rw_wv), "rw_wo": bf(rw_wo),
        "rw_k_k": rw_k_k, "rw_k_a": rw_k_a, "rw_r_k": rw_r_k.reshape(-1), "rw_ln_w": rw_ln_w, "rw_ln_b": rw_ln_b,
    }

    c_all = jnp.concatenate([c_prompt, c_sample], axis=0)
    mod0 = _ada_mod(c_all, ada_w0, ada_b0)
    mod1 = _ada_mod(c_all, ada_w1, ada_b1)
    mod0_p, mod0_s = _split_mod(mod0[:bp]), _split_mod(mod0[bp:])
    mod1_p, mod1_s = _split_mod(mod1[:bp]), _split_mod(mod1[bp:])

    xp = x_prompt.reshape(mp, d)
    xs = x_sample.reshape(ms, d)
    pos_p = jnp.arange(lp, dtype=jnp.int32)
    pos_s = past_len + jnp.arange(ls, dtype=jnp.int32)

    xp, k_p, v_p, conv_p, gdn_p = _layer_even(
        xp, mod0_p, pos_p, jnp.zeros((bp, GDN_CONV - 1, 3 * qk_w), F32),
        jnp.zeros((bp, GDN_HEADS, GDN_DK, GDN_DK), F32), None, p, bp, lp, tm_p)
    xs, k_s, v_s, conv_s, gdn_s = _layer_even(
        xs, mod0_s, pos_s, state_gdn_conv, state_gdn, (cache_k, cache_v, page_table), p, bs, ls, tm_s)

    n_rw_heads = d // RW_HEAD
    xp, shift_p, rw_p = _layer_odd_mix(xp, mod1_p, jnp.zeros((bp, d), F32),
                                       jnp.zeros((bp, n_rw_heads, RW_HEAD, RW_HEAD), F32), p, bp, lp, tm_p)
    xs, shift_s, rw_s = _layer_odd_mix(xs, mod1_s, state_rwkv_shift, state_rwkv, p, bs, ls, tm_s)

    rw_pad = jnp.concatenate([moe_router_w, jnp.zeros((d, LANES - N_EXPERTS), F32)], axis=1)
    rb_pad = jnp.concatenate([moe_router_b, jnp.zeros((LANES - N_EXPERTS,), F32)]).reshape(1, LANES)
    tm_r = _tile(lp, 512)
    h_p, idx_p, gate_p = _router(xp, mod1_p[3], mod1_p[4], norm_ffn1, rw_pad, rb_pad, lp, tm_r)
    h_s, idx_s, gate_s = _router(xs, mod1_s[3], mod1_s[4], norm_ffn1, rw_pad, rb_pad, ls, tm_s)
    h_all = jnp.concatenate([h_p, h_s], axis=0)
    flat_e = jnp.concatenate([idx_p[:, :TOP_K], idx_s[:, :TOP_K]], axis=0).reshape(-1)
    n_assign = flat_e.shape[0]
    tmoe = MOE_TILE
    onehot = (flat_e[:, None] == jnp.arange(N_EXPERTS, dtype=jnp.int32)[None, :]).astype(jnp.int32)
    running = jnp.cumsum(onehot, axis=0)
    pos_in_e = jnp.sum((running - onehot) * onehot, axis=1)
    counts = running[-1]
    padded = (counts + tmoe - 1) // tmoe * tmoe
    pad_end = jnp.cumsum(padded)
    pad_start = pad_end - padded
    dest = (pad_start[flat_e] + pos_in_e).astype(jnp.int32)
    n_tiles = -(-n_assign // tmoe) + N_EXPERTS
    n_rows = n_tiles * tmoe
    row_tok = jnp.zeros((n_rows,), jnp.int32).at[dest].set(jnp.arange(n_assign, dtype=jnp.int32) // TOP_K)
    n_active = (pad_end[-1] // tmoe).astype(jnp.int32)
    tile_ids = jnp.minimum(jnp.arange(n_tiles, dtype=jnp.int32), n_active - 1)
    tile_e = jnp.minimum(jnp.searchsorted(pad_end, tile_ids * tmoe, side="right"), N_EXPERTS - 1).astype(jnp.int32)
    x_sorted = _gather_rows(h_all, row_tok, tmoe)
    yb = _moe_ffn(x_sorted, tile_e, n_active.reshape(1), bf(moe_w_gate), bf(moe_w_up), bf(moe_w_down), tmoe)
    dest2 = dest.reshape(-1, TOP_K)
    rows_c = _tile(mp + ms, GATHER_ROWS)
    y0 = _gather_rows(yb, dest2[:, 0], rows_c)
    y1 = _gather_rows(yb, dest2[:, 1], rows_c)
    tm_c = _tile(lp, 256)
    y_p = _combine_final(xp, mod1_p[5], gate_p, y0, y1, norm_final, 0, lp, tm_c)
    y_s = _combine_final(xs, mod1_s[5], gate_s, y0, y1, norm_final, mp, ls, tm_s)

    return (y_p.reshape(bp, lp, d), y_s.reshape(bs, ls, d), k_p, v_p, k_s, v_s, conv_p, conv_s,
            gdn_p, gdn_s, shift_p, shift_s, rw_p, rw_s)
```

```python
import functools
import math

import jax
import jax.numpy as jnp
from jax import lax
from jax.experimental import pallas as pl
from jax.experimental.pallas import tpu as pltpu

F32 = jnp.float32
BF16 = jnp.bfloat16

D_MODEL = 2048
NORM_EPS = 1e-6
GDN_HEADS = 8
GDN_DK = 128
GDN_CONV = 4
GDN_CHUNK = 64
DA_HEADS = 8
DA_DK = 64
DA_DV = 128
ROPE_DIM = 16
ROPE_THETA = 500000.0
DA_LAMBDA_INIT = 0.8 - 0.6 * math.exp(-0.3 * 0)
RW_HEAD = 64
RW_CHUNK = 64
RW_GN_EPS = 1e-5 * RW_HEAD
N_EXPERTS = 8
TOP_K = 2
PAGE_SIZE = 128
LANES = 128
SUBLANES = 8
VMEM_LIMIT = 56 * 1024 * 1024
MOE_TILE = 512
FFN_TILE = 1024
PAGES_PER_STEP = 8
GATHER_ROWS = 512
GATHER_UNROLL = 8
WKV_ITEMS_PER_STEP = 16
GDN_ITEMS_PER_STEP = 16


def _cparams(sem):
    return pltpu.CompilerParams(dimension_semantics=sem, vmem_limit_bytes=VMEM_LIMIT)


def _tile(n, pref):
    if n <= pref:
        return n
    t = pref
    while t >= SUBLANES:
        if n % t == 0 and t % SUBLANES == 0:
            return t
        t -= SUBLANES
    return n


def _dot(a, b):
    return jnp.dot(a.astype(BF16), b.astype(BF16), preferred_element_type=F32)


def _dot_nt(a, b):
    return lax.dot_general(a.astype(BF16), b.astype(BF16), (((1,), (1,)), ((), ())),
                           preferred_element_type=F32)


def _dot_tn(a, b):
    return lax.dot_general(a.astype(BF16), b.astype(BF16), (((0,), (0,)), ((), ())),
                           preferred_element_type=F32)


def _sigmoid(x):
    return 1.0 / (1.0 + jnp.exp(-x))


def _silu(x):
    return x * _sigmoid(x)


def _softplus(x):
    return jnp.maximum(x, 0.0) + jnp.log(1.0 + jnp.exp(-jnp.abs(x)))


def _rms(x, eps=NORM_EPS):
    return x * lax.rsqrt(jnp.mean(x * x, axis=-1, keepdims=True) + eps)


def _unit_lower_inverse(a, block):
    n = a.shape[0]
    row = lax.broadcasted_iota(jnp.int32, (n, n), 0)
    col = lax.broadcasted_iota(jnp.int32, (n, n), 1)
    inv = (row == col).astype(F32) - jnp.where((row // 2) == (col // 2), a, 0.0)
    s = 4
    while s <= block:
        a_s = jnp.where(((row // s) == (col // s)) & ((row // (s // 2)) != (col // (s // 2))), a, 0.0)
        inv = inv - _dot(_dot(inv, a_s), inv)
        s *= 2
    return inv


def _unit_lower_inverses(mats, block):
    n = mats[0].shape[0]
    row = lax.broadcasted_iota(jnp.int32, (n, n), 0)
    col = lax.broadcasted_iota(jnp.int32, (n, n), 1)
    eye = (row == col).astype(F32)
    pair = (row // 2) == (col // 2)
    invs = [eye - jnp.where(pair, a, 0.0) for a in mats]
    s = 4
    while s <= block:
        level = ((row // s) == (col // s)) & ((row // (s // 2)) != (col // (s // 2)))
        left = [_dot(inv, jnp.where(level, a, 0.0)) for inv, a in zip(invs, mats)]
        invs = [inv - _dot(lf, inv) for inv, lf in zip(invs, left)]
        s *= 2
    return invs


class _Mod:
    def __init__(self, m, seq_len, tm):
        self.width = m.shape[1]
        if seq_len % tm == 0:
            self.per_batch = True
            self.tiles_per_batch = seq_len // tm
            self.arr = m.reshape(m.shape[0], 1, self.width)
        else:
            assert tm % seq_len == 0
            self.per_batch = False
            self.arr = jnp.repeat(m, seq_len, axis=0)
        self.tm = tm

    def spec(self, tn=None, col=None):
        tn = self.width if tn is None else tn
        col = (lambda i, j: 0) if col is None else col
        if self.per_batch:
            tpb = self.tiles_per_batch
            return pl.BlockSpec((None, 1, tn), lambda i, j: (i // tpb, 0, col(i, j)))
        return pl.BlockSpec((self.tm, tn), lambda i, j: (i, col(i, j)))

    def spec1(self):
        if self.per_batch:
            tpb = self.tiles_per_batch
            return pl.BlockSpec((None, 1, self.width), lambda i: (i // tpb, 0, 0))
        return pl.BlockSpec((self.tm, self.width), lambda i: (i, 0))


def _ada_kernel(c_ref, w_ref, b_ref, o_ref):
    a = _silu(c_ref[...])
    o_ref[...] = _dot(a, w_ref[...]) + b_ref[...]


def _ada_mod(c, w, b):
    r, d = c.shape
    n = w.shape[1]
    tn = _tile(n, 1024)
    return pl.pallas_call(
        _ada_kernel,
        grid=(n // tn,),
        in_specs=[pl.BlockSpec((r, d), lambda j: (0, 0)),
                  pl.BlockSpec((d, tn), lambda j: (0, j)),
                  pl.BlockSpec((1, tn), lambda j: (0, j))],
        out_specs=pl.BlockSpec((r, tn), lambda j: (0, j)),
        out_shape=jax.ShapeDtypeStruct((r, n), F32),
        compiler_params=_cparams(("parallel",)),
        name="ada_mod",
    )(c, w, b.reshape(1, n))


def _rope_tile(acc, cos, sin_lo, sin_hi):
    fwd = pltpu.roll(acc, LANES - ROPE_DIM // 2, 1)
    bwd = pltpu.roll(acc, ROPE_DIM // 2, 1)
    return acc * cos + fwd * sin_lo + bwd * sin_hi


def _norm_proj_kernel(*refs, n_w, rope, has_extra, tn):
    x_ref, sh_ref, sc_ref, g_ref = refs[:4]
    pos = 4
    w_refs = refs[pos:pos + n_w]
    pos += n_w
    if has_extra:
        we_ref = refs[pos]
        pos += 1
    if any(rope):
        cos_ref, slo_ref, shi_ref = refs[pos:pos + 3]
        pos += 3
    o_refs = refs[pos:pos + n_w]
    pos += n_w
    if has_extra:
        oe_ref = refs[pos]
        pos += 1
    h_ref = refs[pos]
    j = pl.program_id(1)

    @pl.when(j == 0)
    def _():
        h = _rms(x_ref[...]) * g_ref[...] * (1.0 + sc_ref[...]) + sh_ref[...]
        h_ref[...] = h.astype(BF16)
        if has_extra:
            oe_ref[...] = jnp.dot(h_ref[...], we_ref[...], preferred_element_type=F32)

    h = h_ref[...]
    for k in range(n_w):
        acc = jnp.dot(h, w_refs[k][...], preferred_element_type=F32)
        if rope[k]:
            cos, slo, shi = cos_ref[...], slo_ref[...], shi_ref[...]
            for c in range(tn // LANES):
                sl = slice(c * LANES, (c + 1) * LANES)
                o_refs[k][:, sl] = _rope_tile(acc[:, sl], cos, slo, shi)
        else:
            o_refs[k][...] = acc


def _norm_proj(x, sh, sc, g, ws, rope, extra_w, rope_tabs, seq_len, tm):
    m, d = x.shape
    n = ws[0].shape[1]
    tn = _tile(n, 256)
    n_w = len(ws)
    shm, scm = _Mod(sh, seq_len, tm), _Mod(sc, seq_len, tm)
    in_specs = [pl.BlockSpec((tm, d), lambda i, j: (i, 0)), shm.spec(), scm.spec(),
                pl.BlockSpec((1, d), lambda i, j: (0, 0))]
    args = [x, shm.arr, scm.arr, g.reshape(1, d)]
    for w in ws:
        in_specs.append(pl.BlockSpec((d, tn), lambda i, j: (0, j)))
        args.append(w)
    has_extra = extra_w is not None
    if has_extra:
        in_specs.append(pl.BlockSpec((d, LANES), lambda i, j: (0, 0)))
        args.append(extra_w)
    if any(rope):
        if seq_len % tm == 0:
            tpb = seq_len // tm
            tab_spec = pl.BlockSpec((tm, LANES), lambda i, j: (i % tpb, 0))
        else:
            tab_spec = pl.BlockSpec((tm, LANES), lambda i, j: (i, 0))
        for t in rope_tabs:
            in_specs.append(tab_spec)
            args.append(t)
    out_specs = [pl.BlockSpec((tm, tn), lambda i, j: (i, j)) for _ in ws]
    out_shape = [jax.ShapeDtypeStruct((m, n), F32) for _ in ws]
    if has_extra:
        out_specs.append(pl.BlockSpec((tm, LANES), lambda i, j: (i, 0)))
        out_shape.append(jax.ShapeDtypeStruct((m, LANES), F32))
    return pl.pallas_call(
        functools.partial(_norm_proj_kernel, n_w=n_w, rope=tuple(rope), has_extra=has_extra, tn=tn),
        grid=(m // tm, n // tn),
        in_specs=in_specs,
        out_specs=out_specs,
        out_shape=out_shape,
        scratch_shapes=[pltpu.VMEM((tm, d), BF16)],
        compiler_params=_cparams(("parallel", "arbitrary")),
        name="norm_proj",
    )(*args)


def _rope_tables(pos):
    half = ROPE_DIM // 2
    inv_freq = ROPE_THETA ** (-jnp.arange(half, dtype=F32) * (2.0 / ROPE_DIM))
    ang = pos.astype(F32)[:, None] * inv_freq
    cos, sin = jnp.cos(ang), jnp.sin(ang)
    n = pos.shape[0]
    ones = jnp.ones((n, DA_DK - ROPE_DIM), F32)
    zeros = jnp.zeros((n, DA_DK - ROPE_DIM), F32)
    zh = jnp.zeros((n, half), F32)
    cos_m = jnp.concatenate([cos, cos, ones], axis=1)
    slo_m = jnp.concatenate([-sin, zh, zeros], axis=1)
    shi_m = jnp.concatenate([zh, sin, zeros], axis=1)
    return tuple(jnp.concatenate([t, t], axis=1) for t in (cos_m, slo_m, shi_m))


def _gdn_kernel(q_ref, k_ref, v_ref, z_ref, ba_ref, csq_ref, csk_ref, csv_ref, cwq_ref, cwk_ref, cwv_ref,
                s0_ref, alog_ref, dtb_ref, nw_ref, o_ref, sout_ref, s_scr, hist_scr, ext_scr, *, tt, chunk):
    hp = pl.program_id(1)
    t = pl.program_id(2)
    nt = pl.num_programs(2)
    n_hist = SUBLANES

    @pl.when(t == 0)
    def _():
        s_scr[...] = s0_ref[...]
        hist_scr[0] = csq_ref[...]
        hist_scr[1] = csk_ref[...]
        hist_scr[2] = csv_ref[...]

    conv = []
    for s, (raw_ref, cw_ref) in enumerate(((q_ref, cwq_ref), (k_ref, cwk_ref), (v_ref, cwv_ref))):
        raw = raw_ref[...]
        ext_scr[s, 0:n_hist, :] = hist_scr[s]
        ext_scr[s, n_hist:n_hist + tt, :] = raw
        cw = cw_ref[...]
        y = raw * cw[GDN_CONV - 1:GDN_CONV, :]
        for dly in range(1, GDN_CONV):
            y = y + ext_scr[s, n_hist - dly:n_hist - dly + tt, :] * cw[GDN_CONV - 1 - dly:GDN_CONV - dly, :]
        hist_scr[s] = raw[tt - n_hist:tt, :]
        conv.append(_silu(y))
    q_all, k_all, v_all = conv
    z_all = z_ref[...]

    ba = ba_ref[...]
    lane = lax.broadcasted_iota(jnp.int32, ba.shape, 1)
    gates = jnp.where(lane < GDN_HEADS, _sigmoid(ba), -jnp.exp(alog_ref[...]) * _softplus(ba + dtb_ref[...]))
    n_heads = q_ref.shape[1] // LANES
    n_pairs = n_heads // 2
    beta_cols, g_cols = [], []
    for hq in range(n_heads):
        hh = n_heads * hp + hq
        beta_cols.append(jnp.sum(jnp.where(lane == hh, gates, 0.0), axis=1, keepdims=True))
        g_cols.append(jnp.sum(jnp.where(lane == GDN_HEADS + hh, gates, 0.0), axis=1, keepdims=True))

    c = chunk
    n2 = 2 * c
    row = lax.broadcasted_iota(jnp.int32, (n2, n2), 0)
    col = lax.broadcasted_iota(jnp.int32, (n2, n2), 1)
    same = (row // c) == (col // c)
    low_incl = same & (col <= row)
    low_strict = same & (col < row)
    up_incl = same & (row <= col)
    eye = row == col
    nw = nw_ref[...]

    def stack(x_all, r0, pq):
        return jnp.concatenate([x_all[r0:r0 + c, (2 * pq + hl) * LANES:(2 * pq + hl + 1) * LANES]
                                for hl in range(2)], axis=0)

    def stack_col(cols, r0, pq):
        return jnp.concatenate([cols[2 * pq + hl][r0:r0 + c, :] for hl in range(2)], axis=0)

    n_chunks = tt // c
    items = [(pq, ci) for pq in range(n_pairs) for ci in range(n_chunks)]
    rng = range(len(items))
    a_mats, qks, kes, bvs, qes, kds, s_decay = ([] for _ in range(7))
    for pq, ci in items:
        r0 = ci * c
        q = stack(q_all, r0, pq)
        k = stack(k_all, r0, pq)
        beta = stack_col(beta_cols, r0, pq)
        g = stack_col(g_cols, r0, pq)
        q = q * lax.rsqrt(jnp.sum(q * q, axis=-1, keepdims=True) + 1e-12) * (GDN_DK ** -0.5)
        k = k * lax.rsqrt(jnp.sum(k * k, axis=-1, keepdims=True) + 1e-12)
        g_row = jnp.sum(jnp.where(eye, g, 0.0), axis=0, keepdims=True)
        gc_col = jnp.sum(jnp.where(low_incl, g_row, 0.0), axis=1, keepdims=True)
        gc_row = jnp.sum(jnp.where(up_incl, g, 0.0), axis=0, keepdims=True)
        decay = jnp.where(low_incl, jnp.exp(jnp.where(low_incl, gc_col - gc_row, 0.0)), 0.0)
        gram = _dot_nt(jnp.concatenate([k * beta, q], axis=0), k)
        a_mats.append(jnp.where(low_strict, gram[0:n2] * decay, 0.0))
        qks.append(jnp.where(low_incl, gram[n2:2 * n2] * decay, 0.0))
        egc = jnp.exp(gc_col)
        kes.append(k * (beta * egc))
        bvs.append(stack(v_all, r0, pq) * beta)
        qes.append(q * egc)
        g_last = [gc_col[(hl + 1) * c - 1:(hl + 1) * c, :] for hl in range(2)]
        kds.append([k[hl * c:(hl + 1) * c] * jnp.exp(g_last[hl] - gc_col[hl * c:(hl + 1) * c]) for hl in range(2)])
        s_decay.append([jnp.exp(g_last[hl]) for hl in range(2)])
    t_inv = _unit_lower_inverses(a_mats, c)
    mb1 = [_dot(t_inv[i], jnp.concatenate([kes[i], bvs[i]], axis=1)) for i in rng]
    mb2 = [_dot(qks[i], mb1[i]) for i in rng]
    m2 = [qes[i] - mb2[i][:, 0:LANES] for i in rng]
    b2 = [mb2[i][:, LANES:2 * LANES] for i in rng]
    mb3 = [[_dot_tn(kds[i][hl], mb1[i][hl * c:(hl + 1) * c]) for hl in range(2)] for i in rng]

    outs = []
    for pq in range(n_pairs):
        s = [s_scr[2 * pq], s_scr[2 * pq + 1]]
        for ci in range(n_chunks):
            i = pq * n_chunks + ci
            s_cat = jnp.concatenate(s, axis=1).astype(BF16)
            o_wide = jnp.dot(m2[i].astype(BF16), s_cat, preferred_element_type=F32)
            outs.append(jnp.concatenate([o_wide[0:c, 0:LANES], o_wide[c:n2, LANES:2 * LANES]], axis=0) + b2[i])
            s = [s[hl] * s_decay[i][hl] - _dot(mb3[i][hl][:, 0:LANES], s[hl]) + mb3[i][hl][:, LANES:2 * LANES]
                 for hl in range(2)]
        s_scr[2 * pq] = s[0]
        s_scr[2 * pq + 1] = s[1]

    for i, (pq, ci) in enumerate(items):
        r0 = ci * c
        o = _rms(outs[i]) * nw * _silu(stack(z_all, r0, pq))
        for hl in range(2):
            lanes = slice((2 * pq + hl) * LANES, (2 * pq + hl + 1) * LANES)
            o_ref[r0:r0 + c, lanes] = o[hl * c:(hl + 1) * c].astype(o_ref.dtype)

    @pl.when(t == nt - 1)
    def _():
        sout_ref[...] = s_scr[...]


def _gated_deltanet(qraw, kraw, vraw, z, ba, conv_state, s0, conv_w, a_log, dt_bias, norm_w, n_batch, seq_len):
    m = qraw.shape[0]
    tt = _tile(seq_len, 256)
    nt = seq_len // tt
    chunk = min(GDN_CHUNK, tt)
    assert tt % chunk == 0
    n_hist = SUBLANES
    pairs = min(GDN_HEADS // 2, max(1, GDN_ITEMS_PER_STEP // (tt // chunk)))
    w2 = 2 * pairs * LANES
    qkv_w = 3 * GDN_HEADS * GDN_DK
    cs = jnp.concatenate([jnp.zeros((n_batch, n_hist - (GDN_CONV - 1), qkv_w), F32), conv_state], axis=1)
    cw = jnp.concatenate([conv_w, jnp.zeros((n_hist - GDN_CONV, qkv_w), F32)], axis=0)
    alog = jnp.zeros((1, LANES), F32).at[0, GDN_HEADS:2 * GDN_HEADS].set(a_log)
    dtb = jnp.zeros((1, LANES), F32).at[0, GDN_HEADS:2 * GDN_HEADS].set(dt_bias)
    nblk = GDN_HEADS // (2 * pairs)
    row_spec = pl.BlockSpec((tt, w2), lambda b, hp, t: (b * nt + t, hp))
    cs_specs = [pl.BlockSpec((None, n_hist, w2), functools.partial(lambda b, hp, t, s: (b, 0, s * nblk + hp), s=s))
                for s in range(3)]
    cw_specs = [pl.BlockSpec((n_hist, w2), functools.partial(lambda b, hp, t, s: (0, s * nblk + hp), s=s))
                for s in range(3)]
    st_spec = pl.BlockSpec((None, 2 * pairs, GDN_DK, GDN_DK), lambda b, hp, t: (b, hp, 0, 0))
    vec_spec = pl.BlockSpec((1, LANES), lambda b, hp, t: (0, 0))
    return pl.pallas_call(
        functools.partial(_gdn_kernel, tt=tt, chunk=chunk),
        grid=(n_batch, nblk, nt),
        in_specs=[row_spec, row_spec, row_spec, row_spec,
                  pl.BlockSpec((tt, LANES), lambda b, hp, t: (b * nt + t, 0)),
                  *cs_specs, *cw_specs, st_spec, vec_spec, vec_spec, vec_spec],
        out_specs=[row_spec, st_spec],
        out_shape=[jax.ShapeDtypeStruct((m, GDN_HEADS * GDN_DK), BF16),
                   jax.ShapeDtypeStruct(s0.shape, F32)],
        scratch_shapes=[pltpu.VMEM((2 * pairs, GDN_DK, GDN_DK), F32),
                        pltpu.VMEM((3, n_hist, w2), F32),
                        pltpu.VMEM((3, n_hist + tt, w2), F32)],
        compiler_params=_cparams(("parallel", "parallel", "arbitrary")),
        name="gated_deltanet",
    )(qraw, kraw, vraw, z, ba, cs, cs, cs, cw, cw, cw, s0, alog, dtb, norm_w.reshape(1, LANES))


def _lambda(lq1_ref, lk1_ref, lq2_ref, lk2_ref):
    s1 = jnp.sum(lq1_ref[...] * lk1_ref[...], axis=-1, keepdims=True)
    s2 = jnp.sum(lq2_ref[...] * lk2_ref[...], axis=-1, keepdims=True)
    return jnp.exp(s1) - jnp.exp(s2) + DA_LAMBDA_INIT


def _flash_kernel(q_ref, k_ref, v_ref, lq1_ref, lk1_ref, lq2_ref, lk2_ref, sw_ref, o_ref,
                  *, tq, tk):
    qi = pl.program_id(2)
    q = q_ref[...] * (DA_DK ** -0.5)
    lane = lax.broadcasted_iota(jnp.int32, q.shape, 1)
    q_maps = (jnp.where(lane < DA_DK, q, 0.0).astype(BF16), jnp.where(lane >= DA_DK, q, 0.0).astype(BF16))

    def block(ki, carry, diagonal):
        start = pl.multiple_of(ki * tk, tk)
        k = k_ref[pl.ds(start, tk), :].astype(BF16)
        v = v_ref[pl.ds(start, tk), :].astype(BF16)
        if diagonal:
            visible = (lax.broadcasted_iota(jnp.int32, (tq, tk), 1) <= lax.broadcasted_iota(jnp.int32, (tq, tk), 0))
        new = []
        for mp in range(2):
            m_old, l_old, acc_old = carry[mp]
            s = lax.dot_general(q_maps[mp], k, (((1,), (1,)), ((), ())), preferred_element_type=F32)
            if diagonal:
                s = jnp.where(visible, s, -jnp.inf)
            m_new = jnp.maximum(m_old, jnp.max(s, axis=-1, keepdims=True))
            alpha = jnp.exp(m_old - m_new)
            p = jnp.exp(s - m_new)
            l_new = alpha * l_old + jnp.sum(p, axis=-1, keepdims=True)
            acc_new = alpha * acc_old + jnp.dot(p.astype(BF16), v, preferred_element_type=F32)
            new.append((m_new, l_new, acc_new))
        return tuple(new)

    init = tuple((jnp.full((tq, 1), -jnp.inf, F32), jnp.zeros((tq, 1), F32), jnp.zeros((tq, LANES), F32))
                 for _ in range(2))
    carry = lax.fori_loop(0, qi, lambda ki, c: block(ki, c, False), init)
    (_, l0, acc0), (_, l1, acc1) = block(qi, carry, True)
    lam = _lambda(lq1_ref, lk1_ref, lq2_ref, lk2_ref)
    o = acc0 / l0 - lam * (acc1 / l1)
    o = _rms(o) * sw_ref[...] * (1.0 - DA_LAMBDA_INIT)
    o_ref[...] = o.astype(o_ref.dtype)


def _diff_attention_prompt(q, k, v, lq1, lk1, lq2, lk2, subln_w, n_batch, seq_len):
    m = q.shape[0]
    tq = _tile(seq_len, 1024)
    tk = tq
    nq = seq_len // tq
    vec = lambda a: a.reshape(1, -1)
    vspec = lambda n: pl.BlockSpec((1, n), lambda b, h, qi: (0, 0))
    kv_spec = pl.BlockSpec((seq_len, LANES), lambda b, h, qi: (b, h))
    return pl.pallas_call(
        functools.partial(_flash_kernel, tq=tq, tk=tk),
        grid=(n_batch, DA_HEADS, nq),
        in_specs=[pl.BlockSpec((tq, LANES), lambda b, h, qi: (b * nq + qi, h)), kv_spec, kv_spec,
                  vspec(DA_DK), vspec(DA_DK), vspec(DA_DK), vspec(DA_DK), vspec(DA_DV)],
        out_specs=pl.BlockSpec((tq, LANES), lambda b, h, qi: (b * nq + qi, h)),
        out_shape=jax.ShapeDtypeStruct((m, DA_HEADS * DA_DV), BF16),
        compiler_params=_cparams(("parallel", "parallel", "arbitrary")),
        name="diff_attn_prompt",
    )(q, k, v, vec(lq1), vec(lk1), vec(lq2), vec(lk2), vec(subln_w))


def _paged_attn_kernel(pt_ref, q_ref, kn_ref, vn_ref, *refs, n_group, ls):
    k_refs = refs[:n_group]
    v_refs = refs[n_group:2 * n_group]
    lq1_ref, lk1_ref, lq2_ref, lk2_ref, sw_ref, o_ref, qs_scr, m_scr, l_scr, acc_scr = refs[2 * n_group:]
    p_idx = pl.program_id(1)
    n_steps = pl.num_programs(1)
    n_rows = 2 * ls

    @pl.when(p_idx == 0)
    def _():
        q = q_ref[...] * (DA_DK ** -0.5)
        q2 = jnp.concatenate([q, q], axis=0)
        r = lax.broadcasted_iota(jnp.int32, q2.shape, 0)
        lane = lax.broadcasted_iota(jnp.int32, q2.shape, 1)
        keep = (r // ls) == ((lane % LANES) // DA_DK)
        qs_scr[...] = jnp.where(keep, q2, 0.0).astype(BF16)
        m_scr[...] = jnp.full(m_scr.shape, -jnp.inf, F32)
        l_scr[...] = jnp.zeros(l_scr.shape, F32)
        acc_scr[...] = jnp.zeros(acc_scr.shape, F32)

    def update(k_heads, v_heads, visible):
        s = jnp.concatenate(
            [lax.dot_general(qs_scr[:, h * LANES:(h + 1) * LANES], k_heads[h], (((1,), (1,)), ((), ())),
                             preferred_element_type=F32) for h in range(DA_HEADS)], axis=0)
        if visible is not None:
            s = jnp.where(visible, s, -jnp.inf)
        m_old = m_scr[...]
        m_new = jnp.maximum(m_old, jnp.max(s, axis=-1, keepdims=True))
        alpha = jnp.exp(m_old - m_new)
        p = jnp.exp(s - m_new)
        l_scr[...] = alpha * l_scr[...] + jnp.sum(p, axis=-1, keepdims=True)
        p = p.astype(BF16)
        pv = jnp.concatenate([jnp.dot(p[h * n_rows:(h + 1) * n_rows], v_heads[h], preferred_element_type=F32)
                              for h in range(DA_HEADS)], axis=0)
        acc_scr[...] = alpha * acc_scr[...] + pv
        m_scr[...] = m_new

    update([jnp.concatenate([kr[pl.ds(h, PAGE_SIZE, stride=DA_HEADS), :].astype(BF16) for kr in k_refs], axis=0)
            for h in range(DA_HEADS)],
           [jnp.concatenate([vr[pl.ds(h, PAGE_SIZE, stride=DA_HEADS), :].astype(BF16) for vr in v_refs], axis=0)
            for h in range(DA_HEADS)], None)

    @pl.when(p_idx == n_steps - 1)
    def _():
        lam = _lambda(lq1_ref, lk1_ref, lq2_ref, lk2_ref)
        pad = jnp.zeros((LANES - ls, DA_HEADS * LANES), F32)
        kn = jnp.concatenate([kn_ref[...], pad], axis=0).astype(BF16)
        vn = jnp.concatenate([vn_ref[...], pad], axis=0).astype(BF16)
        r = lax.broadcasted_iota(jnp.int32, (DA_HEADS * n_rows, LANES), 0)
        c = lax.broadcasted_iota(jnp.int32, (DA_HEADS * n_rows, LANES), 1)
        update([kn[:, h * LANES:(h + 1) * LANES] for h in range(DA_HEADS)],
               [vn[:, h * LANES:(h + 1) * LANES] for h in range(DA_HEADS)], c <= (r % ls))
        acc = acc_scr[...] / l_scr[...]
        for h in range(DA_HEADS):
            o = acc[h * n_rows:h * n_rows + ls] - lam * acc[h * n_rows + ls:(h + 1) * n_rows]
            o = _rms(o) * sw_ref[...] * (1.0 - DA_LAMBDA_INIT)
            o_ref[:, h * LANES:(h + 1) * LANES] = o.astype(o_ref.dtype)


def _diff_attention_sample(q, k_new, v_new, cache_k, cache_v, page_table, lq1, lk1, lq2, lk2, subln_w, n_batch, ls):
    n_pool = cache_k.shape[0]
    width = DA_HEADS * LANES
    ck = cache_k.reshape(n_pool, PAGE_SIZE * DA_HEADS, LANES)
    cv = cache_v.reshape(n_pool, PAGE_SIZE * DA_HEADS, LANES)
    n_pages = page_table.shape[1]
    n_group = next(g for g in (PAGES_PER_STEP, 4, 2, 1) if n_pages % g == 0)
    n_steps = n_pages // n_group
    row_spec = pl.BlockSpec((ls, width), lambda b, p, pt: (b, 0))
    page_specs = [pl.BlockSpec((None, PAGE_SIZE * DA_HEADS, LANES),
                               functools.partial(lambda b, p, pt, g: (pt[b, p * n_group + g], 0, 0), g=g))
                  for g in range(n_group)]
    vec = lambda a: a.reshape(1, -1)
    vspec = lambda n: pl.BlockSpec((1, n), lambda b, p, pt: (0, 0))
    grid_spec = pltpu.PrefetchScalarGridSpec(
        num_scalar_prefetch=1,
        grid=(n_batch, n_steps),
        in_specs=[row_spec, row_spec, row_spec, *page_specs, *page_specs,
                  vspec(DA_DK), vspec(DA_DK), vspec(DA_DK), vspec(DA_DK), vspec(DA_DV)],
        out_specs=row_spec,
        scratch_shapes=[pltpu.VMEM((2 * ls, width), BF16), pltpu.VMEM((DA_HEADS * 2 * ls, 1), F32),
                        pltpu.VMEM((DA_HEADS * 2 * ls, 1), F32), pltpu.VMEM((DA_HEADS * 2 * ls, LANES), F32)],
    )
    return pl.pallas_call(
        functools.partial(_paged_attn_kernel, n_group=n_group, ls=ls),
        grid_spec=grid_spec,
        out_shape=jax.ShapeDtypeStruct((n_batch * ls, width), BF16),
        compiler_params=_cparams(("parallel", "arbitrary")),
        name="diff_attn_sample",
    )(page_table, q, k_new, v_new, *([ck] * n_group), *([cv] * n_group),
      vec(lq1), vec(lk1), vec(lq2), vec(lk2), vec(subln_w))


def _proj_res_kernel(*refs, n_parts):
    x_ref, gt_ref = refs[:2]
    a_refs = refs[2:2 + n_parts]
    w_refs = refs[2 + n_parts:2 + 2 * n_parts]
    o_ref = refs[2 + 2 * n_parts]
    acc = jnp.dot(a_refs[0][...], w_refs[0][...], preferred_element_type=F32)
    for p in range(1, n_parts):
        acc = acc + jnp.dot(a_refs[p][...], w_refs[p][...], preferred_element_type=F32)
    o_ref[...] = x_ref[...] + gt_ref[...] * acc


def _proj_residual(x, gt, parts, ws, seq_len, tm):
    m, n = x.shape
    tn = _tile(n, 512)
    gtm = _Mod(gt, seq_len, tm)
    in_specs = [pl.BlockSpec((tm, tn), lambda i, j: (i, j)), gtm.spec(tn, lambda i, j: j)]
    for a in parts:
        in_specs.append(pl.BlockSpec((tm, a.shape[1]), lambda i, j: (i, 0)))
    for w in ws:
        in_specs.append(pl.BlockSpec((w.shape[0], tn), lambda i, j: (0, j)))
    return pl.pallas_call(
        functools.partial(_proj_res_kernel, n_parts=len(parts)),
        grid=(m // tm, n // tn),
        in_specs=in_specs,
        out_specs=pl.BlockSpec((tm, tn), lambda i, j: (i, j)),
        out_shape=jax.ShapeDtypeStruct((m, n), F32),
        compiler_params=_cparams(("parallel", "arbitrary")),
        name="proj_residual",
    )(x, gtm.arr, *parts, *ws)


def _norm_ffn_kernel(x_ref, sh_ref, sc_ref, gt_ref, g_ref, wg_ref, wu_ref, wd_ref, o_ref, h_ref):
    f = pl.program_id(1)
    nf = pl.num_programs(1)

    @pl.when(f == 0)
    def _():
        h = _rms(x_ref[...]) * g_ref[...] * (1.0 + sc_ref[...]) + sh_ref[...]
        h_ref[...] = h.astype(BF16)
        o_ref[...] = jnp.zeros(o_ref.shape, F32)

    h = h_ref[...]
    gate = jnp.dot(h, wg_ref[...], preferred_element_type=F32)
    up = jnp.dot(h, wu_ref[...], preferred_element_type=F32)
    act = (_silu(gate) * up).astype(BF16)
    o_ref[...] += jnp.dot(act, wd_ref[...], preferred_element_type=F32)

    @pl.when(f == nf - 1)
    def _():
        o_ref[...] = x_ref[...] + gt_ref[...] * o_ref[...]


def _norm_ffn(x, sh, sc, gt, g, wg, wu, wd, seq_len, tm):
    m, d = x.shape
    dff = wg.shape[1]
    tf = _tile(dff, FFN_TILE)
    tm = _tile(tm, 512)
    shm, scm, gtm = _Mod(sh, seq_len, tm), _Mod(sc, seq_len, tm), _Mod(gt, seq_len, tm)
    return pl.pallas_call(
        _norm_ffn_kernel,
        grid=(m // tm, dff // tf),
        in_specs=[pl.BlockSpec((tm, d), lambda i, f: (i, 0)), shm.spec(), scm.spec(), gtm.spec(),
                  pl.BlockSpec((1, d), lambda i, f: (0, 0)),
                  pl.BlockSpec((d, tf), lambda i, f: (0, f)),
                  pl.BlockSpec((d, tf), lambda i, f: (0, f)),
                  pl.BlockSpec((tf, d), lambda i, f: (f, 0))],
        out_specs=pl.BlockSpec((tm, d), lambda i, f: (i, 0)),
        out_shape=jax.ShapeDtypeStruct((m, d), F32),
        scratch_shapes=[pltpu.VMEM((tm, d), BF16)],
        compiler_params=_cparams(("parallel", "arbitrary")),
        name="norm_ffn",
    )(x, shm.arr, scm.arr, gtm.arr, g.reshape(1, d), wg, wu, wd)


def _rwkv_proj_kernel(x_ref, xp_ref, sh_ref, sc_ref, shift_ref, g_ref, mu_ref, w1_ref, a1_ref, g1_ref,
                      wr_ref, wk_ref, wv_ref, w2_ref, a2_ref, g2_ref, w0_ref, a0_ref,
                      r_ref, k_ref, v_ref, lw_ref, a_ref, gg_ref,
                      mix_scr, sw_scr, sa_scr, sg_scr, *, tm, seq_len):
    i = pl.program_id(0)
    j = pl.program_id(1)

    @pl.when(j == 0)
    def _():
        g = g_ref[...]
        scale = 1.0 + sc_ref[...]
        shift = sh_ref[...]
        h = _rms(x_ref[...]) * g * scale + shift
        row = lax.broadcasted_iota(jnp.int32, h.shape, 0)
        rolled = pltpu.roll(h, 1, 0)
        if seq_len % tm == 0:
            sc_row = scale[0:1] if scale.shape[0] > 1 else scale
            sh_row = shift[0:1] if shift.shape[0] > 1 else shift
            hp = _rms(xp_ref[...]) * g * sc_row + sh_row
            first = jnp.where((i % (seq_len // tm)) == 0, shift_ref[...], hp[SUBLANES - 1:SUBLANES])
            prev = jnp.where(row == 0, first, rolled)
        else:
            prev = jnp.where((row % seq_len) == 0, shift_ref[...], rolled)
        xx = prev - h
        mu = mu_ref[...]
        for n in range(6):
            mix_scr[n] = (h + xx * mu[n:n + 1]).astype(BF16)
        sw_scr[...] = jnp.tanh(jnp.dot(mix_scr[1], w1_ref[...], preferred_element_type=F32)).astype(BF16)
        sa_scr[...] = jnp.dot(mix_scr[4], a1_ref[...], preferred_element_type=F32).astype(BF16)
        sg_scr[...] = _sigmoid(jnp.dot(mix_scr[5], g1_ref[...], preferred_element_type=F32)).astype(BF16)

    r_ref[...] = jnp.dot(mix_scr[0], wr_ref[...], preferred_element_type=F32)
    k_ref[...] = jnp.dot(mix_scr[2], wk_ref[...], preferred_element_type=F32)
    v_ref[...] = jnp.dot(mix_scr[3], wv_ref[...], preferred_element_type=F32)
    wl = w0_ref[...] + jnp.dot(sw_scr[...], w2_ref[...], preferred_element_type=F32)
    lw_ref[...] = -jnp.exp(-_softplus(-wl) - 0.5)
    a_ref[...] = _sigmoid(a0_ref[...] + jnp.dot(sa_scr[...], a2_ref[...], preferred_element_type=F32))
    gg_ref[...] = jnp.dot(sg_scr[...], g2_ref[...], preferred_element_type=F32)


def _rwkv_proj(x, sh, sc, shift_state, g, mu, w1, a1, g1, wr, wk, wv, w2, a2, g2, w0, a0, n_batch, seq_len, tm):
    m, d = x.shape
    tn = _tile(d, 256)
    shm, scm = _Mod(sh, seq_len, tm), _Mod(sc, seq_len, tm)
    stm = _Mod(shift_state, seq_len, tm)
    rows8 = tm // SUBLANES
    full = lambda a: pl.BlockSpec(a.shape, lambda i, j: (0,) * a.ndim)
    colw = lambda a: pl.BlockSpec((a.shape[0], tn), lambda i, j: (0, j))
    out_spec = pl.BlockSpec((tm, tn), lambda i, j: (i, j))
    g2d, w0r, a0r = g.reshape(1, d), w0.reshape(1, d), a0.reshape(1, d)
    return pl.pallas_call(
        functools.partial(_rwkv_proj_kernel, tm=tm, seq_len=seq_len),
        grid=(m // tm, d // tn),
        in_specs=[pl.BlockSpec((tm, d), lambda i, j: (i, 0)),
                  pl.BlockSpec((SUBLANES, d), lambda i, j: (jnp.maximum(i * rows8 - 1, 0), 0)),
                  shm.spec(), scm.spec(), stm.spec(), full(g2d), full(mu), full(w1), full(a1), full(g1),
                  colw(wr), colw(wk), colw(wv), colw(w2), colw(a2), colw(g2), colw(w0r), colw(a0r)],
        out_specs=[out_spec] * 6,
        out_shape=[jax.ShapeDtypeStruct((m, d), F32)] * 6,
        scratch_shapes=[pltpu.VMEM((6, tm, d), BF16), pltpu.VMEM((tm, w1.shape[1]), BF16),
                        pltpu.VMEM((tm, a1.shape[1]), BF16), pltpu.VMEM((tm, g1.shape[1]), BF16)],
        compiler_params=_cparams(("parallel", "arbitrary")),
        name="rwkv_proj",
    )(x, x, shm.arr, scm.arr, stm.arr, g2d, mu, w1, a1, g1, wr, wk, wv, w2, a2, g2, w0r, a0r)


def _norm_mod_kernel(x_ref, sh_ref, sc_ref, g_ref, o_ref):
    o_ref[...] = _rms(x_ref[...]) * g_ref[...] * (1.0 + sc_ref[...]) + sh_ref[...]


def _norm_mod(x, sh, sc, g):
    r, d = x.shape
    spec = pl.BlockSpec((r, d), lambda i: (0, 0))
    return pl.pallas_call(
        _norm_mod_kernel, grid=(1,),
        in_specs=[spec, spec, spec, pl.BlockSpec((1, d), lambda i: (0, 0))],
        out_specs=spec, out_shape=jax.ShapeDtypeStruct((r, d), F32),
        compiler_params=_cparams(("arbitrary",)), name="norm_mod",
    )(x, sh, sc, g.reshape(1, d))


def _wkv_kernel(r_ref, k_ref, v_ref, lw_ref, a_ref, g_ref, kk_ref, ka_ref, rk_ref, lnw_ref, lnb_ref, s0_ref,
                o_ref, sout_ref, z_scr, *, tt, chunk):
    t = pl.program_id(2)
    nt = pl.num_programs(2)

    @pl.when(t == 0)
    def _():
        z_scr[...] = s0_ref[...]

    c = chunk
    n2 = 2 * c
    n_chunks = max(tt // c, 1)
    lane1 = lax.broadcasted_iota(jnp.int32, (1, LANES), 1)
    head_a = lane1 < RW_HEAD
    rr = lax.broadcasted_iota(jnp.int32, (LANES, LANES), 0)
    cc = lax.broadcasted_iota(jnp.int32, (LANES, LANES), 1)
    seg_ones = ((rr // RW_HEAD) == (cc // RW_HEAD)).astype(BF16)
    row = lax.broadcasted_iota(jnp.int32, (n2, n2), 0)
    col = lax.broadcasted_iota(jnp.int32, (n2, n2), 1)
    same = (row // c) == (col // c)
    low_incl = same & (col <= row)
    low_strict = same & (col < row)
    tri = (lax.broadcasted_iota(jnp.int32, (c, c), 1) <= lax.broadcasted_iota(jnp.int32, (c, c), 0)).astype(BF16)
    n_pairs = r_ref.shape[1] // LANES

    def seg_sum(x):
        return jnp.dot(x.astype(BF16), seg_ones, preferred_element_type=F32)

    def stack(x):
        return jnp.concatenate([jnp.where(head_a, x, 0.0), jnp.where(head_a, 0.0, x)], axis=0)

    pt_s, rt_s, ch_s, kh_s, v_s, w_end, l_pc, l_pk, a_rc, a_rk = ([] for _ in range(10))
    bonus, gate = [], []
    for p in range(n_pairs):
        ls = slice(p * LANES, (p + 1) * LANES)

        def load(ref):
            x = ref[:, ls]
            if tt < c:
                x = jnp.concatenate([x, jnp.zeros((c - tt, LANES), F32)], axis=0)
            return x

        r_all, k_all, v_all, lw_all, a_all = (load(ref) for ref in (r_ref, k_ref, v_ref, lw_ref, a_ref))
        gate.append(load(g_ref))
        kk_raw = k_all * kk_ref[:, ls]
        kk_all = kk_raw * lax.rsqrt(seg_sum(kk_raw * kk_raw) + 1e-12)
        kmod_all = k_all * (1.0 + (a_all - 1.0) * ka_ref[:, ls])
        bonus.append(seg_sum(r_all * kmod_all * rk_ref[:, ls]) * v_all)
        lw_hi = lw_all.astype(BF16)
        lw_lo = (lw_all - lw_hi.astype(F32)).astype(BF16)
        for ci in range(n_chunks):
            sl = slice(ci * c, (ci + 1) * c)
            cum = (jnp.dot(tri, lw_hi[sl], preferred_element_type=F32)
                   + jnp.dot(tri, lw_lo[sl], preferred_element_type=F32))
            cum_end = cum[c - 1:c, :]
            e_neg = jnp.exp(-cum)
            e_end = jnp.exp(cum_end - cum)
            pvec = -kk_all[sl]
            cvec = kk_all[sl] * a_all[sl]
            p_s = stack(pvec * jnp.exp(cum - lw_all[sl]))
            r_s = stack(r_all[sl] * jnp.exp(cum))
            gram = _dot_nt(jnp.concatenate([p_s, r_s], axis=0),
                           jnp.concatenate([stack(cvec * e_neg), stack(kmod_all[sl] * e_neg)], axis=0))
            l_pc.append(jnp.where(low_strict, gram[0:n2, 0:n2], 0.0))
            l_pk.append(jnp.where(low_strict, gram[0:n2, n2:2 * n2], 0.0))
            a_rc.append(jnp.where(low_incl, gram[n2:2 * n2, 0:n2], 0.0))
            a_rk.append(jnp.where(low_incl, gram[n2:2 * n2, n2:2 * n2], 0.0))
            pt_s.append(p_s)
            rt_s.append(r_s)
            ch_s.append(stack(cvec * e_end))
            kh_s.append(stack(kmod_all[sl] * e_end))
            v_s.append(stack(v_all[sl]))
            w_end.append(jnp.exp(cum_end))
    t_inv = _unit_lower_inverses([-x for x in l_pc], c)
    rng = range(n_pairs * n_chunks)
    lv = [_dot(l_pk[i], v_s[i]) for i in rng]
    m1 = [_dot(t_inv[i], pt_s[i]) for i in rng]
    b1 = [_dot(t_inv[i], lv[i]) for i in rng]
    m2 = [rt_s[i] + _dot(a_rc[i], m1[i]) for i in rng]
    bv = [jnp.concatenate([b1[i], v_s[i]], axis=0) for i in rng]
    b2 = [_dot(jnp.concatenate([a_rc[i], a_rk[i]], axis=1), bv[i]) for i in rng]
    m3 = [_dot_tn(ch_s[i], m1[i]) for i in rng]
    b3 = [_dot_tn(bv[i], jnp.concatenate([ch_s[i], kh_s[i]], axis=0)) for i in rng]

    for p in range(n_pairs):
        ls = slice(p * LANES, (p + 1) * LANES)
        z = z_scr[p]
        ys = []
        for ci in range(n_chunks):
            i = p * n_chunks + ci
            y_s = _dot_nt(m2[i], z) + b2[i]
            z = z * w_end[i] + _dot_nt(z, m3[i]) + b3[i]
            ys.append(y_s[0:c] + y_s[c:n2])
        z_scr[p] = z
        y = jnp.concatenate(ys, axis=0) if n_chunks > 1 else ys[0]
        mean = seg_sum(y) * (1.0 / RW_HEAD)
        dev = y - mean
        var = seg_sum(dev * dev) * (1.0 / RW_HEAD)
        yn = dev * lax.rsqrt(var + RW_GN_EPS) * lnw_ref[:, ls] + lnb_ref[:, ls]
        out = (yn + bonus[p]) * gate[p]
        o_ref[:, ls] = out[0:tt].astype(o_ref.dtype)

    @pl.when(t == nt - 1)
    def _():
        sout_ref[...] = z_scr[...]


def _wkv7(r, k, v, lw, a, g, k_k, k_a, r_k, ln_w, ln_b, s0_pairs, n_batch, seq_len):
    m, d = r.shape
    n_pairs = d // LANES
    tt = _tile(seq_len, 512)
    nt = seq_len // tt
    chunk = min(RW_CHUNK, max(seq_len, SUBLANES))
    group = min(n_pairs, max(1, WKV_ITEMS_PER_STEP // max(tt // chunk, 1)))
    row_spec = pl.BlockSpec((tt, group * LANES), lambda b, hp, t: (b * nt + t, hp))
    vec_spec = pl.BlockSpec((1, group * LANES), lambda b, hp, t: (0, hp))
    st_spec = pl.BlockSpec((None, group, LANES, LANES), lambda b, hp, t: (b, hp, 0, 0))
    vec = lambda x: x.reshape(1, d)
    return pl.pallas_call(
        functools.partial(_wkv_kernel, tt=tt, chunk=chunk),
        grid=(n_batch, n_pairs // group, nt),
        in_specs=[row_spec] * 6 + [vec_spec] * 5 + [st_spec],
        out_specs=[row_spec, st_spec],
        out_shape=[jax.ShapeDtypeStruct((m, d), BF16), jax.ShapeDtypeStruct(s0_pairs.shape, F32)],
        scratch_shapes=[pltpu.VMEM((group, LANES, LANES), F32)],
        compiler_params=_cparams(("parallel", "parallel", "arbitrary")),
        name="wkv7",
    )(r, k, v, lw, a, g, vec(k_k), vec(k_a), vec(r_k), vec(ln_w), vec(ln_b), s0_pairs)


def _pair_states(s):
    b, h, n, _ = s.shape
    s2 = s.reshape(b, h // 2, 2, n, n)
    zero = jnp.zeros((b, h // 2, n, n), s.dtype)
    top = jnp.concatenate([s2[:, :, 0], zero], axis=-1)
    bot = jnp.concatenate([zero, s2[:, :, 1]], axis=-1)
    return jnp.concatenate([top, bot], axis=-2)


def _unpair_states(sp):
    n = RW_HEAD
    b, hp = sp.shape[:2]
    return jnp.stack([sp[:, :, :n, :n], sp[:, :, n:, n:]], axis=2).reshape(b, 2 * hp, n, n)


def _router_kernel(x_ref, sh_ref, sc_ref, g_ref, rw_ref, rb_ref, h_ref, idx_ref, gate_ref):
    h = _rms(x_ref[...]) * g_ref[...] * (1.0 + sc_ref[...]) + sh_ref[...]
    h_ref[...] = h
    logits = jnp.dot(h, rw_ref[...], preferred_element_type=F32, precision=lax.Precision.HIGHEST) + rb_ref[...]
    lane = lax.broadcasted_iota(jnp.int32, logits.shape, 1)
    lane_f = lane.astype(F32)
    neg = -jnp.inf
    l1 = jnp.where(lane < N_EXPERTS, logits, neg)
    m1 = jnp.max(l1, axis=-1, keepdims=True)
    i1 = jnp.min(jnp.where(l1 == m1, lane_f, float(LANES)), axis=-1, keepdims=True)
    l2 = jnp.where(lane_f == i1, neg, l1)
    m2 = jnp.max(l2, axis=-1, keepdims=True)
    i2 = jnp.min(jnp.where(l2 == m2, lane_f, float(LANES)), axis=-1, keepdims=True)
    e = jnp.exp(m2 - m1)
    g0 = 1.0 / (1.0 + e)
    g1 = e / (1.0 + e)
    idx_ref[...] = jnp.where(lane == 0, i1, jnp.where(lane == 1, i2, 0.0)).astype(jnp.int32)
    gate_ref[...] = jnp.where(lane == 0, g0, jnp.where(lane == 1, g1, 0.0))


def _router(x, sh, sc, g, rw_pad, rb_pad, seq_len, tm):
    m, d = x.shape
    shm, scm = _Mod(sh, seq_len, tm), _Mod(sc, seq_len, tm)
    return pl.pallas_call(
        _router_kernel,
        grid=(m // tm,),
        in_specs=[pl.BlockSpec((tm, d), lambda i: (i, 0)), shm.spec1(), scm.spec1(),
                  pl.BlockSpec((1, d), lambda i: (0, 0)),
                  pl.BlockSpec((d, LANES), lambda i: (0, 0)), pl.BlockSpec((1, LANES), lambda i: (0, 0))],
        out_specs=[pl.BlockSpec((tm, d), lambda i: (i, 0)), pl.BlockSpec((tm, LANES), lambda i: (i, 0)),
                   pl.BlockSpec((tm, LANES), lambda i: (i, 0))],
        out_shape=[jax.ShapeDtypeStruct((m, d), F32), jax.ShapeDtypeStruct((m, LANES), jnp.int32),
                   jax.ShapeDtypeStruct((m, LANES), F32)],
        compiler_params=_cparams(("parallel",)),
        name="moe_router",
    )(x, shm.arr, scm.arr, g.reshape(1, d), rw_pad, rb_pad)


def _gather_rows_kernel(idx_ref, src_ref, o_ref, sem, *, rows):
    i = pl.program_id(0)

    def row_copy(r):
        return pltpu.make_async_copy(src_ref.at[pl.ds(idx_ref[i * rows + r], 1), :], o_ref.at[pl.ds(r, 1), :], sem)

    def start_group(grp, carry):
        for j in range(GATHER_UNROLL):
            row_copy(grp * GATHER_UNROLL + j).start(priority=j % 2)
        return carry

    def wait(r, carry):
        row_copy(r).wait()
        return carry

    lax.fori_loop(0, rows // GATHER_UNROLL, start_group, 0)
    lax.fori_loop(0, rows, wait, 0, unroll=GATHER_UNROLL)


def _gather_rows(src, row_idx, rows):
    n_rows = row_idx.shape[0]
    d = src.shape[1]
    assert n_rows % rows == 0 and rows % GATHER_UNROLL == 0
    grid_spec = pltpu.PrefetchScalarGridSpec(
        num_scalar_prefetch=1,
        grid=(n_rows // rows,),
        in_specs=[pl.BlockSpec(memory_space=pl.ANY)],
        out_specs=pl.BlockSpec((rows, d), lambda i, idx: (i, 0)),
        scratch_shapes=[pltpu.SemaphoreType.DMA(())],
    )
    return pl.pallas_call(
        functools.partial(_gather_rows_kernel, rows=rows),
        grid_spec=grid_spec,
        out_shape=jax.ShapeDtypeStruct((n_rows, d), src.dtype),
        compiler_params=_cparams(("arbitrary",)),
        name="moe_gather",
    )(row_idx, src)


def _moe_ffn_kernel(te_ref, na_ref, x_ref, wg_ref, wu_ref, wd_ref, o_ref, h_ref):
    i = pl.program_id(0)
    f = pl.program_id(1)
    active = i < na_ref[0]

    @pl.when(f == 0)
    def _():
        h_ref[...] = x_ref[...].astype(BF16)
        o_ref[...] = jnp.zeros(o_ref.shape, F32)

    @pl.when(active)
    def _():
        h = h_ref[...]
        gate = jnp.dot(h, wg_ref[...], preferred_element_type=F32)
        up = jnp.dot(h, wu_ref[...], preferred_element_type=F32)
        act = (_silu(gate) * up).astype(BF16)
        o_ref[...] += jnp.dot(act, wd_ref[...], preferred_element_type=F32)


def _moe_ffn(xs, tile_expert, n_active, wg, wu, wd, tm):
    n_rows, d = xs.shape
    dff = wg.shape[2]
    tf = _tile(dff, FFN_TILE)
    nf = dff // tf

    def fcol(i, f, te, na):
        return jnp.where(i < na[0], f, nf - 1)

    grid_spec = pltpu.PrefetchScalarGridSpec(
        num_scalar_prefetch=2,
        grid=(n_rows // tm, nf),
        in_specs=[pl.BlockSpec((tm, d), lambda i, f, te, na: (i, 0)),
                  pl.BlockSpec((None, d, tf), lambda i, f, te, na: (te[i], 0, fcol(i, f, te, na))),
                  pl.BlockSpec((None, d, tf), lambda i, f, te, na: (te[i], 0, fcol(i, f, te, na))),
                  pl.BlockSpec((None, tf, d), lambda i, f, te, na: (te[i], fcol(i, f, te, na), 0))],
        out_specs=pl.BlockSpec((tm, d), lambda i, f, te, na: (i, 0)),
        scratch_shapes=[pltpu.VMEM((tm, d), BF16)],
    )
    return pl.pallas_call(
        _moe_ffn_kernel,
        grid_spec=grid_spec,
        out_shape=jax.ShapeDtypeStruct((n_rows, d), F32),
        compiler_params=_cparams(("parallel", "arbitrary")),
        name="moe_ffn",
    )(tile_expert, n_active, xs, wg, wu, wd)


def _combine_kernel(x_ref, gt_ref, gate_ref, g_ref, y0_ref, y1_ref, o_ref):
    gates = gate_ref[...]
    y = gates[:, 0:1] * y0_ref[...] + gates[:, 1:2] * y1_ref[...]
    x = x_ref[...] + gt_ref[...] * y
    o_ref[...] = _rms(x) * g_ref[...]


def _combine_final(x, gt, gates, y0, y1, g, row0, seq_len, tm):
    m, d = x.shape
    gtm = _Mod(gt, seq_len, tm)
    assert row0 % tm == 0
    blk0 = row0 // tm
    y_spec = pl.BlockSpec((tm, d), lambda i: (blk0 + i, 0))
    return pl.pallas_call(
        _combine_kernel,
        grid=(m // tm,),
        in_specs=[pl.BlockSpec((tm, d), lambda i: (i, 0)), gtm.spec1(),
                  pl.BlockSpec((tm, LANES), lambda i: (i, 0)),
                  pl.BlockSpec((1, d), lambda i: (0, 0)), y_spec, y_spec],
        out_specs=pl.BlockSpec((tm, d), lambda i: (i, 0)),
        out_shape=jax.ShapeDtypeStruct((m, d), F32),
        compiler_params=_cparams(("parallel",)),
        name="moe_combine_final",
    )(x, gtm.arr, gates, g.reshape(1, d), y0, y1)


def _split_mod(mod):
    return [mod[:, n * D_MODEL:(n + 1) * D_MODEL] for n in range(6)]


def _layer_even(x, mod, pos, conv_state, gdn_state, paged, p, n_batch, seq_len, tm):
    sh1, sc1, gt1, sh2, sc2, gt2 = mod
    m = x.shape[0]
    if seq_len % tm == 0:
        tabs = _rope_tables(pos)
    else:
        tabs = tuple(jnp.tile(t, (n_batch, 1)) for t in _rope_tables(pos))
    gq, gk, gv, gz, ba = _norm_proj(x, sh1, sc1, p["norm_mix0"], p["w_gdn"], (False,) * 4, p["w_ba"], None,
                                    seq_len, tm)
    dq, dk, dv = _norm_proj(x, sh1, sc1, p["norm_mix0"], p["w_da"], (True, True, False), None, tabs, seq_len, tm)
    o_gdn, new_gdn = _gated_deltanet(gq, gk, gv, gz, ba, conv_state, gdn_state, p["gdn_conv_w"], p["gdn_a_log"],
                                     p["gdn_dt_bias"], p["gdn_norm_w"], n_batch, seq_len)
    lam_args = (p["da_lq1"], p["da_lk1"], p["da_lq2"], p["da_lk2"], p["da_subln_w"])
    if paged is None:
        o_da = _diff_attention_prompt(dq, dk, dv, *lam_args, n_batch, seq_len)
    else:
        o_da = _diff_attention_sample(dq, dk, dv, paged[0], paged[1], paged[2], *lam_args, n_batch, seq_len)
    x = _proj_residual(x, gt1, [o_gdn, o_da], p["w_out"], seq_len, tm)
    x = _norm_ffn(x, sh2, sc2, gt2, p["norm_ffn0"], p["ffn_wg"], p["ffn_wu"], p["ffn_wd"], seq_len, tm)
    raw = jnp.concatenate([gq.reshape(n_batch, seq_len, -1), gk.reshape(n_batch, seq_len, -1),
                           gv.reshape(n_batch, seq_len, -1)], axis=-1)
    ext = jnp.concatenate([conv_state, raw[:, max(seq_len - (GDN_CONV - 1), 0):]], axis=1)
    new_conv = ext[:, ext.shape[1] - (GDN_CONV - 1):]
    k_out = dk.reshape(n_batch, seq_len, DA_HEADS, 2 * DA_DK)
    v_out = dv.reshape(n_batch, seq_len, DA_HEADS, DA_DV)
    return x, k_out, v_out, new_conv, new_gdn


def _layer_odd_mix(x, mod, shift_state, wkv_state, p, n_batch, seq_len, tm):
    sh1, sc1, gt1 = mod[:3]
    d = x.shape[1]
    tm_rw = _tile(tm, 512)
    r, k, v, lw, a, g = _rwkv_proj(x, sh1, sc1, shift_state, p["norm_mix1"], p["rw_mu"], p["rw_w1"], p["rw_a1"],
                                   p["rw_g1"], p["rw_wr"], p["rw_wk"], p["rw_wv"], p["rw_w2"], p["rw_a2"],
                                   p["rw_g2"], p["rw_w0"], p["rw_a0"], n_batch, seq_len, tm_rw)
    yg, s_pairs = _wkv7(r, k, v, lw, a, g, p["rw_k_k"], p["rw_k_a"], p["rw_r_k"], p["rw_ln_w"], p["rw_ln_b"],
                        _pair_states(wkv_state), n_batch, seq_len)
    last = x.reshape(n_batch, seq_len, d)[:, seq_len - 1]
    new_shift = _norm_mod(last, sh1, sc1, p["norm_mix1"])
    x = _proj_residual(x, gt1, [yg], [p["rw_wo"]], seq_len, tm)
    return x, new_shift, _unpair_states(s_pairs)


def kernel(x_prompt, x_sample, cache_k, cache_v, state_gdn_conv, state_gdn, state_rwkv_shift, state_rwkv, page_table, c_prompt, c_sample, ada_w0, ada_b0, norm_mix0, w_in0, gdn_conv_w, gdn_a_log, gdn_dt_bias, gdn_norm_w, da_lq1, da_lk1, da_lq2, da_lk2, da_subln_w, w_out0, norm_ffn0, ffn_w_gate, ffn_w_up, ffn_w_down, ada_w1, ada_b1, norm_mix1, rw_mu, rw_w0, rw_w1, rw_w2, rw_a0, rw_a1, rw_a2, rw_g1, rw_g2, rw_k_k, rw_k_a, rw_r_k, rw_wr, rw_wk, rw_wv, rw_wo, rw_ln_w, rw_ln_b, norm_ffn1, moe_router_w, moe_router_b, moe_w_gate, moe_w_up, moe_w_down, norm_final):
    bp, lp, d = x_prompt.shape
    bs, ls, _ = x_sample.shape
    n_pages = page_table.shape[1]
    past_len = n_pages * PAGE_SIZE
    mp, ms = bp * lp, bs * ls
    tm_p = _tile(lp, 1024)
    tm_s = _tile(ms, 256)

    qk_w = GDN_HEADS * GDN_DK
    c0 = 3 * qk_w
    c1 = c0 + qk_w
    c2 = c1 + 2 * GDN_HEADS
    da_w = DA_HEADS * 2 * DA_DK
    bf = lambda w: w.astype(BF16)
    w_ba = jnp.concatenate([w_in0[:, c1:c2], jnp.zeros((d, LANES - 2 * GDN_HEADS), F32)], axis=1)
    lora_pad = lambda w, axis: jnp.pad(w, [(0, (-w.shape[a]) % LANES if a == axis else 0) for a in range(2)])
    p = {
        "norm_mix0": norm_mix0,
        "w_gdn": [bf(w_in0[:, n * qk_w:(n + 1) * qk_w]) for n in range(4)],
        "w_ba": bf(w_ba),
        "w_da": [bf(w_in0[:, c2 + n * da_w:c2 + (n + 1) * da_w]) for n in range(3)],
        "gdn_conv_w": gdn_conv_w, "gdn_a_log": gdn_a_log, "gdn_dt_bias": gdn_dt_bias, "gdn_norm_w": gdn_norm_w,
        "da_lq1": da_lq1, "da_lk1": da_lk1, "da_lq2": da_lq2, "da_lk2": da_lk2, "da_subln_w": da_subln_w,
        "w_out": [bf(w_out0[:qk_w]), bf(w_out0[qk_w:])],
        "norm_ffn0": norm_ffn0, "ffn_wg": bf(ffn_w_gate), "ffn_wu": bf(ffn_w_up), "ffn_wd": bf(ffn_w_down),
        "norm_mix1": norm_mix1, "rw_mu": rw_mu, "rw_w0": rw_w0, "rw_a0": rw_a0,
        "rw_w1": bf(lora_pad(rw_w1, 1)), "rw_w2": bf(lora_pad(rw_w2, 0)),
        "rw_a1": bf(lora_pad(rw_a1, 1)), "rw_a2": bf(lora_pad(rw_a2, 0)),
        "rw_g1": bf(rw_g1), "rw_g2": bf(rw_g2),
        "rw_wr": bf(rw_wr), "rw_wk": bf(rw_wk), "rw_wv": bf(rw_wv), "rw_wo": bf(rw_wo),
        "rw_k_k": rw_k_k, "rw_k_a": rw_k_a, "rw_r_k": rw_r_k.reshape(-1), "rw_ln_w": rw_ln_w, "rw_ln_b": rw_ln_b,
    }

    c_all = jnp.concatenate([c_prompt, c_sample], axis=0)
    mod0 = _ada_mod(c_all, ada_w0, ada_b0)
    mod1 = _ada_mod(c_all, ada_w1, ada_b1)
    mod0_p, mod0_s = _split_mod(mod0[:bp]), _split_mod(mod0[bp:])
    mod1_p, mod1_s = _split_mod(mod1[:bp]), _split_mod(mod1[bp:])

    xp = x_prompt.reshape(mp, d)
    xs = x_sample.reshape(ms, d)
    pos_p = jnp.arange(lp, dtype=jnp.int32)
    pos_s = past_len + jnp.arange(ls, dtype=jnp.int32)

    xp, k_p, v_p, conv_p, gdn_p = _layer_even(
        xp, mod0_p, pos_p, jnp.zeros((bp, GDN_CONV - 1, 3 * qk_w), F32),
        jnp.zeros((bp, GDN_HEADS, GDN_DK, GDN_DK), F32), None, p, bp, lp, tm_p)
    xs, k_s, v_s, conv_s, gdn_s = _layer_even(
        xs, mod0_s, pos_s, state_gdn_conv, state_gdn, (cache_k, cache_v, page_table), p, bs, ls, tm_s)

    n_rw_heads = d // RW_HEAD
    xp, shift_p, rw_p = _layer_odd_mix(xp, mod1_p, jnp.zeros((bp, d), F32),
                                       jnp.zeros((bp, n_rw_heads, RW_HEAD, RW_HEAD), F32), p, bp, lp, tm_p)
    xs, shift_s, rw_s = _layer_odd_mix(xs, mod1_s, state_rwkv_shift, state_rwkv, p, bs, ls, tm_s)

    rw_pad = jnp.concatenate([moe_router_w, jnp.zeros((d, LANES - N_EXPERTS), F32)], axis=1)
    rb_pad = jnp.concatenate([moe_router_b, jnp.zeros((LANES - N_EXPERTS,), F32)]).reshape(1, LANES)
    tm_r = _tile(lp, 512)
    h_p, idx_p, gate_p = _router(xp, mod1_p[3], mod1_p[4], norm_ffn1, rw_pad, rb_pad, lp, tm_r)
    h_s, idx_s, gate_s = _router(xs, mod1_s[3], mod1_s[4], norm_ffn1, rw_pad, rb_pad, ls, tm_s)
    h_all = jnp.concatenate([h_p, h_s], axis=0)
    flat_e = jnp.concatenate([idx_p[:, :TOP_K], idx_s[:, :TOP_K]], axis=0).reshape(-1)
    n_assign = flat_e.shape[0]
    tmoe = MOE_TILE
    onehot = (flat_e[:, None] == jnp.arange(N_EXPERTS, dtype=jnp.int32)[None, :]).astype(jnp.int32)
    running = jnp.cumsum(onehot, axis=0)
    pos_in_e = jnp.sum((running - onehot) * onehot, axis=1)
    counts = running[-1]
    padded = (counts + tmoe - 1) // tmoe * tmoe
    pad_end = jnp.cumsum(padded)
    pad_start = pad_end - padded
    dest = (pad_start[flat_e] + pos_in_e).astype(jnp.int32)
    n_tiles = -(-n_assign // tmoe) + N_EXPERTS
    n_rows = n_tiles * tmoe
    row_tok = jnp.zeros((n_rows,), jnp.int32).at[dest].set(jnp.arange(n_assign, dtype=jnp.int32) // TOP_K)
    n_active = (pad_end[-1] // tmoe).astype(jnp.int32)
    tile_ids = jnp.minimum(jnp.arange(n_tiles, dtype=jnp.int32), n_active - 1)
    tile_e = jnp.minimum(jnp.searchsorted(pad_end, tile_ids * tmoe, side="right"), N_EXPERTS - 1).astype(jnp.int32)
    x_sorted = _gather_rows(h_all, row_tok, tmoe)
    yb = _moe_ffn(x_sorted, tile_e, n_active.reshape(1), bf(moe_w_gate), bf(moe_w_up), bf(moe_w_down), tmoe)
    dest2 = dest.reshape(-1, TOP_K)
    rows_c = _tile(mp + ms, GATHER_ROWS)
    y0 = _gather_rows(yb, dest2[:, 0], rows_c)
    y1 = _gather_rows(yb, dest2[:, 1], rows_c)
    tm_c = _tile(lp, 256)
    y_p = _combine_final(xp, mod1_p[5], gate_p, y0, y1, norm_final, 0, lp, tm_c)
    y_s = _combine_final(xs, mod1_s[5], gate_s, y0, y1, norm_final, mp, ls, tm_s)

    return (y_p.reshape(bp, lp, d), y_s.reshape(bs, ls, d), k_p, v_p, k_s, v_s, conv_p, conv_s,
            gdn_p, gdn_s, shift_p, shift_s, rw_p, rw_s)
```

```python
import functools
import math

import jax
import jax.numpy as jnp
from jax import lax
from jax.experimental import pallas as pl
from jax.experimental.pallas import tpu as pltpu

F32 = jnp.float32
BF16 = jnp.bfloat16

D_MODEL = 2048
NORM_EPS = 1e-6
GDN_HEADS = 8
GDN_DK = 128
GDN_CONV = 4
GDN_CHUNK = 64
DA_HEADS = 8
DA_DK = 64
DA_DV = 128
ROPE_DIM = 16
ROPE_THETA = 500000.0
DA_LAMBDA_INIT = 0.8 - 0.6 * math.exp(-0.3 * 0)
RW_HEAD = 64
RW_CHUNK = 64
RW_GN_EPS = 1e-5 * RW_HEAD
N_EXPERTS = 8
TOP_K = 2
PAGE_SIZE = 128
LANES = 128
SUBLANES = 8
VMEM_LIMIT = 56 * 1024 * 1024
MOE_TILE = 512
FFN_TILE = 1024
MOE_FFN_TILE = 512
PAGES_PER_STEP = 8
GATHER_ROWS = 512
GATHER_UNROLL = 8
WKV_ITEMS_PER_STEP = 16
GDN_ITEMS_PER_STEP = 16


def _cparams(sem):
    return pltpu.CompilerParams(dimension_semantics=sem, vmem_limit_bytes=VMEM_LIMIT)


def _tile(n, pref):
    if n <= pref:
        return n
    t = pref
    while t >= SUBLANES:
        if n % t == 0 and t % SUBLANES == 0:
            return t
        t -= SUBLANES
    return n


def _dot(a, b):
    return jnp.dot(a.astype(BF16), b.astype(BF16), preferred_element_type=F32)


def _dot_nt(a, b):
    return lax.dot_general(a.astype(BF16), b.astype(BF16), (((1,), (1,)), ((), ())),
                           preferred_element_type=F32)


def _dot_tn(a, b):
    return lax.dot_general(a.astype(BF16), b.astype(BF16), (((0,), (0,)), ((), ())),
                           preferred_element_type=F32)


def _sigmoid(x):
    return 1.0 / (1.0 + jnp.exp(-x))


def _silu(x):
    return x * _sigmoid(x)


def _softplus(x):
    return jnp.maximum(x, 0.0) + jnp.log(1.0 + jnp.exp(-jnp.abs(x)))


def _rms(x, eps=NORM_EPS):
    return x * lax.rsqrt(jnp.mean(x * x, axis=-1, keepdims=True) + eps)


def _unit_lower_inverse(a, block):
    n = a.shape[0]
    row = lax.broadcasted_iota(jnp.int32, (n, n), 0)
    col = lax.broadcasted_iota(jnp.int32, (n, n), 1)
    inv = (row == col).astype(F32) - jnp.where((row // 2) == (col // 2), a, 0.0)
    s = 4
    while s <= block:
        a_s = jnp.where(((row // s) == (col // s)) & ((row // (s // 2)) != (col // (s // 2))), a, 0.0)
        inv = inv - _dot(_dot(inv, a_s), inv)
        s *= 2
    return inv


def _unit_lower_inverses(mats, block):
    n = mats[0].shape[0]
    row = lax.broadcasted_iota(jnp.int32, (n, n), 0)
    col = lax.broadcasted_iota(jnp.int32, (n, n), 1)
    eye = (row == col).astype(F32)
    pair = (row // 2) == (col // 2)
    invs = [eye - jnp.where(pair, a, 0.0) for a in mats]
    s = 4
    while s <= block:
        level = ((row // s) == (col // s)) & ((row // (s // 2)) != (col // (s // 2)))
        left = [_dot(inv, jnp.where(level, a, 0.0)) for inv, a in zip(invs, mats)]
        invs = [inv - _dot(lf, inv) for inv, lf in zip(invs, left)]
        s *= 2
    return invs


class _Mod:
    def __init__(self, m, seq_len, tm):
        self.width = m.shape[1]
        if seq_len % tm == 0:
            self.per_batch = True
            self.tiles_per_batch = seq_len // tm
            self.arr = m.reshape(m.shape[0], 1, self.width)
        else:
            assert tm % seq_len == 0
            self.per_batch = False
            self.arr = jnp.repeat(m, seq_len, axis=0)
        self.tm = tm

    def spec(self, tn=None, col=None):
        tn = self.width if tn is None else tn
        col = (lambda i, j: 0) if col is None else col
        if self.per_batch:
            tpb = self.tiles_per_batch
            return pl.BlockSpec((None, 1, tn), lambda i, j: (i // tpb, 0, col(i, j)))
        return pl.BlockSpec((self.tm, tn), lambda i, j: (i, col(i, j)))

    def spec1(self):
        if self.per_batch:
            tpb = self.tiles_per_batch
            return pl.BlockSpec((None, 1, self.width), lambda i: (i // tpb, 0, 0))
        return pl.BlockSpec((self.tm, self.width), lambda i: (i, 0))


def _ada_kernel(c_ref, w_ref, b_ref, o_ref):
    a = _silu(c_ref[...])
    o_ref[...] = _dot(a, w_ref[...]) + b_ref[...]


def _ada_mod(c, w, b):
    r, d = c.shape
    n = w.shape[1]
    tn = _tile(n, 1024)
    return pl.pallas_call(
        _ada_kernel,
        grid=(n // tn,),
        in_specs=[pl.BlockSpec((r, d), lambda j: (0, 0)),
                  pl.BlockSpec((d, tn), lambda j: (0, j)),
                  pl.BlockSpec((1, tn), lambda j: (0, j))],
        out_specs=pl.BlockSpec((r, tn), lambda j: (0, j)),
        out_shape=jax.ShapeDtypeStruct((r, n), F32),
        compiler_params=_cparams(("parallel",)),
        name="ada_mod",
    )(c, w, b.reshape(1, n))


def _rope_tile(acc, cos, sin_lo, sin_hi):
    fwd = pltpu.roll(acc, LANES - ROPE_DIM // 2, 1)
    bwd = pltpu.roll(acc, ROPE_DIM // 2, 1)
    return acc * cos + fwd * sin_lo + bwd * sin_hi


def _norm_proj_kernel(*refs, n_w, rope, has_extra, tn):
    x_ref, sh_ref, sc_ref, g_ref = refs[:4]
    pos = 4
    w_refs = refs[pos:pos + n_w]
    pos += n_w
    if has_extra:
        we_ref = refs[pos]
        pos += 1
    if any(rope):
        cos_ref, slo_ref, shi_ref = refs[pos:pos + 3]
        pos += 3
    o_refs = refs[pos:pos + n_w]
    pos += n_w
    if has_extra:
        oe_ref = refs[pos]
        pos += 1
    h_ref = refs[pos]
    j = pl.program_id(1)

    @pl.when(j == 0)
    def _():
        h = _rms(x_ref[...]) * g_ref[...] * (1.0 + sc_ref[...]) + sh_ref[...]
        h_ref[...] = h.astype(BF16)
        if has_extra:
            oe_ref[...] = jnp.dot(h_ref[...], we_ref[...], preferred_element_type=F32)

    h = h_ref[...]
    for k in range(n_w):
        acc = jnp.dot(h, w_refs[k][...], preferred_element_type=F32)
        if rope[k]:
            cos, slo, shi = cos_ref[...], slo_ref[...], shi_ref[...]
            for c in range(tn // LANES):
                sl = slice(c * LANES, (c + 1) * LANES)
                o_refs[k][:, sl] = _rope_tile(acc[:, sl], cos, slo, shi)
        else:
            o_refs[k][...] = acc


def _norm_proj(x, sh, sc, g, ws, rope, extra_w, rope_tabs, seq_len, tm):
    m, d = x.shape
    n = ws[0].shape[1]
    tn = _tile(n, 256)
    n_w = len(ws)
    shm, scm = _Mod(sh, seq_len, tm), _Mod(sc, seq_len, tm)
    in_specs = [pl.BlockSpec((tm, d), lambda i, j: (i, 0)), shm.spec(), scm.spec(),
                pl.BlockSpec((1, d), lambda i, j: (0, 0))]
    args = [x, shm.arr, scm.arr, g.reshape(1, d)]
    for w in ws:
        in_specs.append(pl.BlockSpec((d, tn), lambda i, j: (0, j)))
        args.append(w)
    has_extra = extra_w is not None
    if has_extra:
        in_specs.append(pl.BlockSpec((d, LANES), lambda i, j: (0, 0)))
        args.append(extra_w)
    if any(rope):
        if seq_len % tm == 0:
            tpb = seq_len // tm
            tab_spec = pl.BlockSpec((tm, LANES), lambda i, j: (i % tpb, 0))
        else:
            tab_spec = pl.BlockSpec((tm, LANES), lambda i, j: (i, 0))
        for t in rope_tabs:
            in_specs.append(tab_spec)
            args.append(t)
    out_specs = [pl.BlockSpec((tm, tn), lambda i, j: (i, j)) for _ in ws]
    out_shape = [jax.ShapeDtypeStruct((m, n), F32) for _ in ws]
    if has_extra:
        out_specs.append(pl.BlockSpec((tm, LANES), lambda i, j: (i, 0)))
        out_shape.append(jax.ShapeDtypeStruct((m, LANES), F32))
    return pl.pallas_call(
        functools.partial(_norm_proj_kernel, n_w=n_w, rope=tuple(rope), has_extra=has_extra, tn=tn),
        grid=(m // tm, n // tn),
        in_specs=in_specs,
        out_specs=out_specs,
        out_shape=out_shape,
        scratch_shapes=[pltpu.VMEM((tm, d), BF16)],
        compiler_params=_cparams(("parallel", "arbitrary")),
        name="norm_proj",
    )(*args)


def _rope_tables(pos):
    half = ROPE_DIM // 2
    inv_freq = ROPE_THETA ** (-jnp.arange(half, dtype=F32) * (2.0 / ROPE_DIM))
    ang = pos.astype(F32)[:, None] * inv_freq
    cos, sin = jnp.cos(ang), jnp.sin(ang)
    n = pos.shape[0]
    ones = jnp.ones((n, DA_DK - ROPE_DIM), F32)
    zeros = jnp.zeros((n, DA_DK - ROPE_DIM), F32)
    zh = jnp.zeros((n, half), F32)
    cos_m = jnp.concatenate([cos, cos, ones], axis=1)
    slo_m = jnp.concatenate([-sin, zh, zeros], axis=1)
    shi_m = jnp.concatenate([zh, sin, zeros], axis=1)
    return tuple(jnp.concatenate([t, t], axis=1) for t in (cos_m, slo_m, shi_m))


def _gdn_kernel(q_ref, k_ref, v_ref, z_ref, ba_ref, csq_ref, csk_ref, csv_ref, cwq_ref, cwk_ref, cwv_ref,
                s0_ref, alog_ref, dtb_ref, nw_ref, o_ref, sout_ref, s_scr, hist_scr, ext_scr, *, tt, chunk):
    hp = pl.program_id(1)
    t = pl.program_id(2)
    nt = pl.num_programs(2)
    n_hist = SUBLANES

    @pl.when(t == 0)
    def _():
        s_scr[...] = s0_ref[...]
        hist_scr[0] = csq_ref[...]
        hist_scr[1] = csk_ref[...]
        hist_scr[2] = csv_ref[...]

    conv = []
    for s, (raw_ref, cw_ref) in enumerate(((q_ref, cwq_ref), (k_ref, cwk_ref), (v_ref, cwv_ref))):
        raw = raw_ref[...]
        ext_scr[s, 0:n_hist, :] = hist_scr[s]
        ext_scr[s, n_hist:n_hist + tt, :] = raw
        cw = cw_ref[...]
        y = raw * cw[GDN_CONV - 1:GDN_CONV, :]
        for dly in range(1, GDN_CONV):
            y = y + ext_scr[s, n_hist - dly:n_hist - dly + tt, :] * cw[GDN_CONV - 1 - dly:GDN_CONV - dly, :]
        hist_scr[s] = raw[tt - n_hist:tt, :]
        conv.append(_silu(y))
    q_all, k_all, v_all = conv
    z_all = z_ref[...]

    ba = ba_ref[...]
    lane = lax.broadcasted_iota(jnp.int32, ba.shape, 1)
    gates = jnp.where(lane < GDN_HEADS, _sigmoid(ba), -jnp.exp(alog_ref[...]) * _softplus(ba + dtb_ref[...]))
    n_heads = q_ref.shape[1] // LANES
    n_pairs = n_heads // 2
    beta_cols, g_cols = [], []
    for hq in range(n_heads):
        hh = n_heads * hp + hq
        beta_cols.append(jnp.sum(jnp.where(lane == hh, gates, 0.0), axis=1, keepdims=True))
        g_cols.append(jnp.sum(jnp.where(lane == GDN_HEADS + hh, gates, 0.0), axis=1, keepdims=True))

    c = chunk
    n2 = 2 * c
    row = lax.broadcasted_iota(jnp.int32, (n2, n2), 0)
    col = lax.broadcasted_iota(jnp.int32, (n2, n2), 1)
    same = (row // c) == (col // c)
    low_incl = same & (col <= row)
    low_strict = same & (col < row)
    up_incl = same & (row <= col)
    eye = row == col
    nw = nw_ref[...]

    def stack(x_all, r0, pq):
        return jnp.concatenate([x_all[r0:r0 + c, (2 * pq + hl) * LANES:(2 * pq + hl + 1) * LANES]
                                for hl in range(2)], axis=0)

    def stack_col(cols, r0, pq):
        return jnp.concatenate([cols[2 * pq + hl][r0:r0 + c, :] for hl in range(2)], axis=0)

    n_chunks = tt // c
    items = [(pq, ci) for pq in range(n_pairs) for ci in range(n_chunks)]
    rng = range(len(items))
    a_mats, qks, kes, bvs, qes, kds, s_decay = ([] for _ in range(7))
    for pq, ci in items:
        r0 = ci * c
        q = stack(q_all, r0, pq)
        k = stack(k_all, r0, pq)
        beta = stack_col(beta_cols, r0, pq)
        g = stack_col(g_cols, r0, pq)
        q = q * lax.rsqrt(jnp.sum(q * q, axis=-1, keepdims=True) + 1e-12) * (GDN_DK ** -0.5)
        k = k * lax.rsqrt(jnp.sum(k * k, axis=-1, keepdims=True) + 1e-12)
        g_row = jnp.sum(jnp.where(eye, g, 0.0), axis=0, keepdims=True)
        gc_col = jnp.sum(jnp.where(low_incl, g_row, 0.0), axis=1, keepdims=True)
        gc_row = jnp.sum(jnp.where(up_incl, g, 0.0), axis=0, keepdims=True)
        decay = jnp.where(low_incl, jnp.exp(jnp.where(low_incl, gc_col - gc_row, 0.0)), 0.0)
        gram = _dot_nt(jnp.concatenate([k * beta, q], axis=0), k)
        a_mats.append(jnp.where(low_strict, gram[0:n2] * decay, 0.0))
        qks.append(jnp.where(low_incl, gram[n2:2 * n2] * decay, 0.0))
        egc = jnp.exp(gc_col)
        kes.append(k * (beta * egc))
        bvs.append(stack(v_all, r0, pq) * beta)
        qes.append(q * egc)
        g_last = [gc_col[(hl + 1) * c - 1:(hl + 1) * c, :] for hl in range(2)]
        kds.append([k[hl * c:(hl + 1) * c] * jnp.exp(g_last[hl] - gc_col[hl * c:(hl + 1) * c]) for hl in range(2)])
        s_decay.append([jnp.exp(g_last[hl]) for hl in range(2)])
    t_inv = _unit_lower_inverses(a_mats, c)
    mb1 = [_dot(t_inv[i], jnp.concatenate([kes[i], bvs[i]], axis=1)) for i in rng]
    mb2 = [_dot(qks[i], mb1[i]) for i in rng]
    m2 = [qes[i] - mb2[i][:, 0:LANES] for i in rng]
    b2 = [mb2[i][:, LANES:2 * LANES] for i in rng]
    mb3 = [[_dot_tn(kds[i][hl], mb1[i][hl * c:(hl + 1) * c]) for hl in range(2)] for i in rng]

    outs = []
    for pq in range(n_pairs):
        s = [s_scr[2 * pq], s_scr[2 * pq + 1]]
        for ci in range(n_chunks):
            i = pq * n_chunks + ci
            s_cat = jnp.concatenate(s, axis=1).astype(BF16)
            o_wide = jnp.dot(m2[i].astype(BF16), s_cat, preferred_element_type=F32)
            outs.append(jnp.concatenate([o_wide[0:c, 0:LANES], o_wide[c:n2, LANES:2 * LANES]], axis=0) + b2[i])
            s = [s[hl] * s_decay[i][hl] - _dot(mb3[i][hl][:, 0:LANES], s[hl]) + mb3[i][hl][:, LANES:2 * LANES]
                 for hl in range(2)]
        s_scr[2 * pq] = s[0]
        s_scr[2 * pq + 1] = s[1]

    for i, (pq, ci) in enumerate(items):
        r0 = ci * c
        o = _rms(outs[i]) * nw * _silu(stack(z_all, r0, pq))
        for hl in range(2):
            lanes = slice((2 * pq + hl) * LANES, (2 * pq + hl + 1) * LANES)
            o_ref[r0:r0 + c, lanes] = o[hl * c:(hl + 1) * c].astype(o_ref.dtype)

    @pl.when(t == nt - 1)
    def _():
        sout_ref[...] = s_scr[...]


def _gated_deltanet(qraw, kraw, vraw, z, ba, conv_state, s0, conv_w, a_log, dt_bias, norm_w, n_batch, seq_len):
    m = qraw.shape[0]
    tt = _tile(seq_len, 256)
    nt = seq_len // tt
    chunk = min(GDN_CHUNK, tt)
    assert tt % chunk == 0
    n_hist = SUBLANES
    pairs = min(GDN_HEADS // 2, max(1, GDN_ITEMS_PER_STEP // (tt // chunk)))
    w2 = 2 * pairs * LANES
    qkv_w = 3 * GDN_HEADS * GDN_DK
    cs = jnp.concatenate([jnp.zeros((n_batch, n_hist - (GDN_CONV - 1), qkv_w), F32), conv_state], axis=1)
    cw = jnp.concatenate([conv_w, jnp.zeros((n_hist - GDN_CONV, qkv_w), F32)], axis=0)
    alog = jnp.zeros((1, LANES), F32).at[0, GDN_HEADS:2 * GDN_HEADS].set(a_log)
    dtb = jnp.zeros((1, LANES), F32).at[0, GDN_HEADS:2 * GDN_HEADS].set(dt_bias)
    nblk = GDN_HEADS // (2 * pairs)
    row_spec = pl.BlockSpec((tt, w2), lambda b, hp, t: (b * nt + t, hp))
    cs_specs = [pl.BlockSpec((None, n_hist, w2), functools.partial(lambda b, hp, t, s: (b, 0, s * nblk + hp), s=s))
                for s in range(3)]
    cw_specs = [pl.BlockSpec((n_hist, w2), functools.partial(lambda b, hp, t, s: (0, s * nblk + hp), s=s))
                for s in range(3)]
    st_spec = pl.BlockSpec((None, 2 * pairs, GDN_DK, GDN_DK), lambda b, hp, t: (b, hp, 0, 0))
    vec_spec = pl.BlockSpec((1, LANES), lambda b, hp, t: (0, 0))
    return pl.pallas_call(
        functools.partial(_gdn_kernel, tt=tt, chunk=chunk),
        grid=(n_batch, nblk, nt),
        in_specs=[row_spec, row_spec, row_spec, row_spec,
                  pl.BlockSpec((tt, LANES), lambda b, hp, t: (b * nt + t, 0)),
                  *cs_specs, *cw_specs, st_spec, vec_spec, vec_spec, vec_spec],
        out_specs=[row_spec, st_spec],
        out_shape=[jax.ShapeDtypeStruct((m, GDN_HEADS * GDN_DK), BF16),
                   jax.ShapeDtypeStruct(s0.shape, F32)],
        scratch_shapes=[pltpu.VMEM((2 * pairs, GDN_DK, GDN_DK), F32),
                        pltpu.VMEM((3, n_hist, w2), F32),
                        pltpu.VMEM((3, n_hist + tt, w2), F32)],
        compiler_params=_cparams(("parallel", "parallel", "arbitrary")),
        name="gated_deltanet",
    )(qraw, kraw, vraw, z, ba, cs, cs, cs, cw, cw, cw, s0, alog, dtb, norm_w.reshape(1, LANES))


def _lambda(lq1_ref, lk1_ref, lq2_ref, lk2_ref):
    s1 = jnp.sum(lq1_ref[...] * lk1_ref[...], axis=-1, keepdims=True)
    s2 = jnp.sum(lq2_ref[...] * lk2_ref[...], axis=-1, keepdims=True)
    return jnp.exp(s1) - jnp.exp(s2) + DA_LAMBDA_INIT


def _flash_kernel(q_ref, k_ref, v_ref, lq1_ref, lk1_ref, lq2_ref, lk2_ref, sw_ref, o_ref,
                  *, tq, tk):
    qi = pl.program_id(2)
    q = q_ref[...] * (DA_DK ** -0.5)
    lane = lax.broadcasted_iota(jnp.int32, q.shape, 1)
    q_maps = (jnp.where(lane < DA_DK, q, 0.0).astype(BF16), jnp.where(lane >= DA_DK, q, 0.0).astype(BF16))

    def block(ki, carry, diagonal):
        start = pl.multiple_of(ki * tk, tk)
        k = k_ref[pl.ds(start, tk), :].astype(BF16)
        v = v_ref[pl.ds(start, tk), :].astype(BF16)
        if diagonal:
            visible = (lax.broadcasted_iota(jnp.int32, (tq, tk), 1) <= lax.broadcasted_iota(jnp.int32, (tq, tk), 0))
        new = []
        for mp in range(2):
            m_old, l_old, acc_old = carry[mp]
            s = lax.dot_general(q_maps[mp], k, (((1,), (1,)), ((), ())), preferred_element_type=F32)
            if diagonal:
                s = jnp.where(visible, s, -jnp.inf)
            m_new = jnp.maximum(m_old, jnp.max(s, axis=-1, keepdims=True))
            alpha = jnp.exp(m_old - m_new)
            p = jnp.exp(s - m_new)
            l_new = alpha * l_old + jnp.sum(p, axis=-1, keepdims=True)
            acc_new = alpha * acc_old + jnp.dot(p.astype(BF16), v, preferred_element_type=F32)
            new.append((m_new, l_new, acc_new))
        return tuple(new)

    init = tuple((jnp.full((tq, 1), -jnp.inf, F32), jnp.zeros((tq, 1), F32), jnp.zeros((tq, LANES), F32))
                 for _ in range(2))
    carry = lax.fori_loop(0, qi, lambda ki, c: block(ki, c, False), init)
    (_, l0, acc0), (_, l1, acc1) = block(qi, carry, True)
    lam = _lambda(lq1_ref, lk1_ref, lq2_ref, lk2_ref)
    o = acc0 / l0 - lam * (acc1 / l1)
    o = _rms(o) * sw_ref[...] * (1.0 - DA_LAMBDA_INIT)
    o_ref[...] = o.astype(o_ref.dtype)


def _diff_attention_prompt(q, k, v, lq1, lk1, lq2, lk2, subln_w, n_batch, seq_len):
    m = q.shape[0]
    tq = _tile(seq_len, 1024)
    tk = tq
    nq = seq_len // tq
    vec = lambda a: a.reshape(1, -1)
    vspec = lambda n: pl.BlockSpec((1, n), lambda b, h, qi: (0, 0))
    kv_spec = pl.BlockSpec((seq_len, LANES), lambda b, h, qi: (b, h))
    return pl.pallas_call(
        functools.partial(_flash_kernel, tq=tq, tk=tk),
        grid=(n_batch, DA_HEADS, nq),
        in_specs=[pl.BlockSpec((tq, LANES), lambda b, h, qi: (b * nq + qi, h)), kv_spec, kv_spec,
                  vspec(DA_DK), vspec(DA_DK), vspec(DA_DK), vspec(DA_DK), vspec(DA_DV)],
        out_specs=pl.BlockSpec((tq, LANES), lambda b, h, qi: (b * nq + qi, h)),
        out_shape=jax.ShapeDtypeStruct((m, DA_HEADS * DA_DV), BF16),
        compiler_params=_cparams(("parallel", "parallel", "arbitrary")),
        name="diff_attn_prompt",
    )(q, k, v, vec(lq1), vec(lk1), vec(lq2), vec(lk2), vec(subln_w))


def _paged_attn_kernel(pt_ref, q_ref, kn_ref, vn_ref, *refs, n_group, ls):
    k_refs = refs[:n_group]
    v_refs = refs[n_group:2 * n_group]
    lq1_ref, lk1_ref, lq2_ref, lk2_ref, sw_ref, o_ref, qs_scr, m_scr, l_scr, acc_scr = refs[2 * n_group:]
    p_idx = pl.program_id(1)
    n_steps = pl.num_programs(1)
    n_rows = 2 * ls

    @pl.when(p_idx == 0)
    def _():
        q = q_ref[...] * (DA_DK ** -0.5)
        q2 = jnp.concatenate([q, q], axis=0)
        r = lax.broadcasted_iota(jnp.int32, q2.shape, 0)
        lane = lax.broadcasted_iota(jnp.int32, q2.shape, 1)
        keep = (r // ls) == ((lane % LANES) // DA_DK)
        qs_scr[...] = jnp.where(keep, q2, 0.0).astype(BF16)
        m_scr[...] = jnp.full(m_scr.shape, -jnp.inf, F32)
        l_scr[...] = jnp.zeros(l_scr.shape, F32)
        acc_scr[...] = jnp.zeros(acc_scr.shape, F32)

    def update(k_heads, v_heads, visible):
        s = jnp.concatenate(
            [lax.dot_general(qs_scr[:, h * LANES:(h + 1) * LANES], k_heads[h], (((1,), (1,)), ((), ())),
                             preferred_element_type=F32) for h in range(DA_HEADS)], axis=0)
        if visible is not None:
            s = jnp.where(visible, s, -jnp.inf)
        m_old = m_scr[...]
        m_new = jnp.maximum(m_old, jnp.max(s, axis=-1, keepdims=True))
        alpha = jnp.exp(m_old - m_new)
        p = jnp.exp(s - m_new)
        l_scr[...] = alpha * l_scr[...] + jnp.sum(p, axis=-1, keepdims=True)
        p = p.astype(BF16)
        pv = jnp.concatenate([jnp.dot(p[h * n_rows:(h + 1) * n_rows], v_heads[h], preferred_element_type=F32)
                              for h in range(DA_HEADS)], axis=0)
        acc_scr[...] = alpha * acc_scr[...] + pv
        m_scr[...] = m_new

    update([jnp.concatenate([kr[pl.ds(h, PAGE_SIZE, stride=DA_HEADS), :].astype(BF16) for kr in k_refs], axis=0)
            for h in range(DA_HEADS)],
           [jnp.concatenate([vr[pl.ds(h, PAGE_SIZE, stride=DA_HEADS), :].astype(BF16) for vr in v_refs], axis=0)
            for h in range(DA_HEADS)], None)

    @pl.when(p_idx == n_steps - 1)
    def _():
        lam = _lambda(lq1_ref, lk1_ref, lq2_ref, lk2_ref)
        pad = jnp.zeros((LANES - ls, DA_HEADS * LANES), F32)
        kn = jnp.concatenate([kn_ref[...], pad], axis=0).astype(BF16)
        vn = jnp.concatenate([vn_ref[...], pad], axis=0).astype(BF16)
        r = lax.broadcasted_iota(jnp.int32, (DA_HEADS * n_rows, LANES), 0)
        c = lax.broadcasted_iota(jnp.int32, (DA_HEADS * n_rows, LANES), 1)
        update([kn[:, h * LANES:(h + 1) * LANES] for h in range(DA_HEADS)],
               [vn[:, h * LANES:(h + 1) * LANES] for h in range(DA_HEADS)], c <= (r % ls))
        acc = acc_scr[...] / l_scr[...]
        for h in range(DA_HEADS):
            o = acc[h * n_rows:h * n_rows + ls] - lam * acc[h * n_rows + ls:(h + 1) * n_rows]
            o = _rms(o) * sw_ref[...] * (1.0 - DA_LAMBDA_INIT)
            o_ref[:, h * LANES:(h + 1) * LANES] = o.astype(o_ref.dtype)


def _diff_attention_sample(q, k_new, v_new, cache_k, cache_v, page_table, lq1, lk1, lq2, lk2, subln_w, n_batch, ls):
    n_pool = cache_k.shape[0]
    width = DA_HEADS * LANES
    ck = cache_k.reshape(n_pool, PAGE_SIZE * DA_HEADS, LANES)
    cv = cache_v.reshape(n_pool, PAGE_SIZE * DA_HEADS, LANES)
    n_pages = page_table.shape[1]
    n_group = next(g for g in (PAGES_PER_STEP, 4, 2, 1) if n_pages % g == 0)
    n_steps = n_pages // n_group
    row_spec = pl.BlockSpec((ls, width), lambda b, p, pt: (b, 0))
    page_specs = [pl.BlockSpec((None, PAGE_SIZE * DA_HEADS, LANES),
                               functools.partial(lambda b, p, pt, g: (pt[b, p * n_group + g], 0, 0), g=g))
                  for g in range(n_group)]
    vec = lambda a: a.reshape(1, -1)
    vspec = lambda n: pl.BlockSpec((1, n), lambda b, p, pt: (0, 0))
    grid_spec = pltpu.PrefetchScalarGridSpec(
        num_scalar_prefetch=1,
        grid=(n_batch, n_steps),
        in_specs=[row_spec, row_spec, row_spec, *page_specs, *page_specs,
                  vspec(DA_DK), vspec(DA_DK), vspec(DA_DK), vspec(DA_DK), vspec(DA_DV)],
        out_specs=row_spec,
        scratch_shapes=[pltpu.VMEM((2 * ls, width), BF16), pltpu.VMEM((DA_HEADS * 2 * ls, 1), F32),
                        pltpu.VMEM((DA_HEADS * 2 * ls, 1), F32), pltpu.VMEM((DA_HEADS * 2 * ls, LANES), F32)],
    )
    return pl.pallas_call(
        functools.partial(_paged_attn_kernel, n_group=n_group, ls=ls),
        grid_spec=grid_spec,
        out_shape=jax.ShapeDtypeStruct((n_batch * ls, width), BF16),
        compiler_params=_cparams(("parallel", "arbitrary")),
        name="diff_attn_sample",
    )(page_table, q, k_new, v_new, *([ck] * n_group), *([cv] * n_group),
      vec(lq1), vec(lk1), vec(lq2), vec(lk2), vec(subln_w))


def _proj_res_kernel(*refs, n_parts):
    x_ref, gt_ref = refs[:2]
    a_refs = refs[2:2 + n_parts]
    w_refs = refs[2 + n_parts:2 + 2 * n_parts]
    o_ref = refs[2 + 2 * n_parts]
    acc = jnp.dot(a_refs[0][...], w_refs[0][...], preferred_element_type=F32)
    for p in range(1, n_parts):
        acc = acc + jnp.dot(a_refs[p][...], w_refs[p][...], preferred_element_type=F32)
    o_ref[...] = x_ref[...] + gt_ref[...] * acc


def _proj_residual(x, gt, parts, ws, seq_len, tm):
    m, n = x.shape
    tn = _tile(n, 512)
    gtm = _Mod(gt, seq_len, tm)
    in_specs = [pl.BlockSpec((tm, tn), lambda i, j: (i, j)), gtm.spec(tn, lambda i, j: j)]
    for a in parts:
        in_specs.append(pl.BlockSpec((tm, a.shape[1]), lambda i, j: (i, 0)))
    for w in ws:
        in_specs.append(pl.BlockSpec((w.shape[0], tn), lambda i, j: (0, j)))
    return pl.pallas_call(
        functools.partial(_proj_res_kernel, n_parts=len(parts)),
        grid=(m // tm, n // tn),
        in_specs=in_specs,
        out_specs=pl.BlockSpec((tm, tn), lambda i, j: (i, j)),
        out_shape=jax.ShapeDtypeStruct((m, n), F32),
        compiler_params=_cparams(("parallel", "arbitrary")),
        name="proj_residual",
    )(x, gtm.arr, *parts, *ws)


def _norm_ffn_kernel(x_ref, sh_ref, sc_ref, gt_ref, g_ref, wg_ref, wu_ref, wd_ref, o_ref, h_ref):
    f = pl.program_id(1)
    nf = pl.num_programs(1)

    @pl.when(f == 0)
    def _():
        h = _rms(x_ref[...]) * g_ref[...] * (1.0 + sc_ref[...]) + sh_ref[...]
        h_ref[...] = h.astype(BF16)
        o_ref[...] = jnp.zeros(o_ref.shape, F32)

    h = h_ref[...]
    gate = jnp.dot(h, wg_ref[...], preferred_element_type=F32)
    up = jnp.dot(h, wu_ref[...], preferred_element_type=F32)
    act = (_silu(gate) * up).astype(BF16)
    o_ref[...] += jnp.dot(act, wd_ref[...], preferred_element_type=F32)

    @pl.when(f == nf - 1)
    def _():
        o_ref[...] = x_ref[...] + gt_ref[...] * o_ref[...]


def _norm_ffn(x, sh, sc, gt, g, wg, wu, wd, seq_len, tm):
    m, d = x.shape
    dff = wg.shape[1]
    tf = _tile(dff, FFN_TILE)
    tm = _tile(tm, 512)
    shm, scm, gtm = _Mod(sh, seq_len, tm), _Mod(sc, seq_len, tm), _Mod(gt, seq_len, tm)
    return pl.pallas_call(
        _norm_ffn_kernel,
        grid=(m // tm, dff // tf),
        in_specs=[pl.BlockSpec((tm, d), lambda i, f: (i, 0)), shm.spec(), scm.spec(), gtm.spec(),
                  pl.BlockSpec((1, d), lambda i, f: (0, 0)),
                  pl.BlockSpec((d, tf), lambda i, f: (0, f)),
                  pl.BlockSpec((d, tf), lambda i, f: (0, f)),
                  pl.BlockSpec((tf, d), lambda i, f: (f, 0))],
        out_specs=pl.BlockSpec((tm, d), lambda i, f: (i, 0)),
        out_shape=jax.ShapeDtypeStruct((m, d), F32),
        scratch_shapes=[pltpu.VMEM((tm, d), BF16)],
        compiler_params=_cparams(("parallel", "arbitrary")),
        name="norm_ffn",
    )(x, shm.arr, scm.arr, gtm.arr, g.reshape(1, d), wg, wu, wd)


def _rwkv_proj_kernel(x_ref, xp_ref, sh_ref, sc_ref, shift_ref, g_ref, mu_ref, w1_ref, a1_ref, g1_ref,
                      wr_ref, wk_ref, wv_ref, w2_ref, a2_ref, g2_ref, w0_ref, a0_ref,
                      r_ref, k_ref, v_ref, lw_ref, a_ref, gg_ref,
                      mix_scr, sw_scr, sa_scr, sg_scr, *, tm, seq_len):
    i = pl.program_id(0)
    j = pl.program_id(1)

    @pl.when(j == 0)
    def _():
        g = g_ref[...]
        scale = 1.0 + sc_ref[...]
        shift = sh_ref[...]
        h = _rms(x_ref[...]) * g * scale + shift
        row = lax.broadcasted_iota(jnp.int32, h.shape, 0)
        rolled = pltpu.roll(h, 1, 0)
        if seq_len % tm == 0:
            sc_row = scale[0:1] if scale.shape[0] > 1 else scale
            sh_row = shift[0:1] if shift.shape[0] > 1 else shift
            hp = _rms(xp_ref[...]) * g * sc_row + sh_row
            first = jnp.where((i % (seq_len // tm)) == 0, shift_ref[...], hp[SUBLANES - 1:SUBLANES])
            prev = jnp.where(row == 0, first, rolled)
        else:
            prev = jnp.where((row % seq_len) == 0, shift_ref[...], rolled)
        xx = prev - h
        mu = mu_ref[...]
        for n in range(6):
            mix_scr[n] = (h + xx * mu[n:n + 1]).astype(BF16)
        sw_scr[...] = jnp.tanh(jnp.dot(mix_scr[1], w1_ref[...], preferred_element_type=F32)).astype(BF16)
        sa_scr[...] = jnp.dot(mix_scr[4], a1_ref[...], preferred_element_type=F32).astype(BF16)
        sg_scr[...] = _sigmoid(jnp.dot(mix_scr[5], g1_ref[...], preferred_element_type=F32)).astype(BF16)

    r_ref[...] = jnp.dot(mix_scr[0], wr_ref[...], preferred_element_type=F32)
    k_ref[...] = jnp.dot(mix_scr[2], wk_ref[...], preferred_element_type=F32)
    v_ref[...] = jnp.dot(mix_scr[3], wv_ref[...], preferred_element_type=F32)
    wl = w0_ref[...] + jnp.dot(sw_scr[...], w2_ref[...], preferred_element_type=F32)
    lw_ref[...] = -jnp.exp(-_softplus(-wl) - 0.5)
    a_ref[...] = _sigmoid(a0_ref[...] + jnp.dot(sa_scr[...], a2_ref[...], preferred_element_type=F32))
    gg_ref[...] = jnp.dot(sg_scr[...], g2_ref[...], preferred_element_type=F32)


def _rwkv_proj(x, sh, sc, shift_state, g, mu, w1, a1, g1, wr, wk, wv, w2, a2, g2, w0, a0, n_batch, seq_len, tm):
    m, d = x.shape
    tn = _tile(d, 256)
    shm, scm = _Mod(sh, seq_len, tm), _Mod(sc, seq_len, tm)
    stm = _Mod(shift_state, seq_len, tm)
    rows8 = tm // SUBLANES
    full = lambda a: pl.BlockSpec(a.shape, lambda i, j: (0,) * a.ndim)
    colw = lambda a: pl.BlockSpec((a.shape[0], tn), lambda i, j: (0, j))
    out_spec = pl.BlockSpec((tm, tn), lambda i, j: (i, j))
    g2d, w0r, a0r = g.reshape(1, d), w0.reshape(1, d), a0.reshape(1, d)
    return pl.pallas_call(
        functools.partial(_rwkv_proj_kernel, tm=tm, seq_len=seq_len),
        grid=(m // tm, d // tn),
        in_specs=[pl.BlockSpec((tm, d), lambda i, j: (i, 0)),
                  pl.BlockSpec((SUBLANES, d), lambda i, j: (jnp.maximum(i * rows8 - 1, 0), 0)),
                  shm.spec(), scm.spec(), stm.spec(), full(g2d), full(mu), full(w1), full(a1), full(g1),
                  colw(wr), colw(wk), colw(wv), colw(w2), colw(a2), colw(g2), colw(w0r), colw(a0r)],
        out_specs=[out_spec] * 6,
        out_shape=[jax.ShapeDtypeStruct((m, d), F32)] * 6,
        scratch_shapes=[pltpu.VMEM((6, tm, d), BF16), pltpu.VMEM((tm, w1.shape[1]), BF16),
                        pltpu.VMEM((tm, a1.shape[1]), BF16), pltpu.VMEM((tm, g1.shape[1]), BF16)],
        compiler_params=_cparams(("parallel", "arbitrary")),
        name="rwkv_proj",
    )(x, x, shm.arr, scm.arr, stm.arr, g2d, mu, w1, a1, g1, wr, wk, wv, w2, a2, g2, w0r, a0r)


def _norm_mod_kernel(x_ref, sh_ref, sc_ref, g_ref, o_ref):
    o_ref[...] = _rms(x_ref[...]) * g_ref[...] * (1.0 + sc_ref[...]) + sh_ref[...]


def _norm_mod(x, sh, sc, g):
    r, d = x.shape
    spec = pl.BlockSpec((r, d), lambda i: (0, 0))
    return pl.pallas_call(
        _norm_mod_kernel, grid=(1,),
        in_specs=[spec, spec, spec, pl.BlockSpec((1, d), lambda i: (0, 0))],
        out_specs=spec, out_shape=jax.ShapeDtypeStruct((r, d), F32),
        compiler_params=_cparams(("arbitrary",)), name="norm_mod",
    )(x, sh, sc, g.reshape(1, d))


def _wkv_kernel(r_ref, k_ref, v_ref, lw_ref, a_ref, g_ref, kk_ref, ka_ref, rk_ref, lnw_ref, lnb_ref, s0_ref,
                o_ref, sout_ref, z_scr, *, tt, chunk):
    t = pl.program_id(2)
    nt = pl.num_programs(2)

    @pl.when(t == 0)
    def _():
        z_scr[...] = s0_ref[...]

    c = chunk
    n2 = 2 * c
    n_chunks = max(tt // c, 1)
    lane1 = lax.broadcasted_iota(jnp.int32, (1, LANES), 1)
    head_a = lane1 < RW_HEAD
    rr = lax.broadcasted_iota(jnp.int32, (LANES, LANES), 0)
    cc = lax.broadcasted_iota(jnp.int32, (LANES, LANES), 1)
    seg_ones = ((rr // RW_HEAD) == (cc // RW_HEAD)).astype(BF16)
    row = lax.broadcasted_iota(jnp.int32, (n2, n2), 0)
    col = lax.broadcasted_iota(jnp.int32, (n2, n2), 1)
    same = (row // c) == (col // c)
    low_incl = same & (col <= row)
    low_strict = same & (col < row)
    tri = (lax.broadcasted_iota(jnp.int32, (c, c), 1) <= lax.broadcasted_iota(jnp.int32, (c, c), 0)).astype(BF16)
    n_pairs = r_ref.shape[1] // LANES

    def seg_sum(x):
        return jnp.dot(x.astype(BF16), seg_ones, preferred_element_type=F32)

    def stack(x):
        return jnp.concatenate([jnp.where(head_a, x, 0.0), jnp.where(head_a, 0.0, x)], axis=0)

    pt_s, rt_s, ch_s, kh_s, v_s, w_end, l_pc, l_pk, a_rc, a_rk = ([] for _ in range(10))
    bonus, gate = [], []
    for p in range(n_pairs):
        ls = slice(p * LANES, (p + 1) * LANES)

        def load(ref):
            x = ref[:, ls]
            if tt < c:
                x = jnp.concatenate([x, jnp.zeros((c - tt, LANES), F32)], axis=0)
            return x

        r_all, k_all, v_all, lw_all, a_all = (load(ref) for ref in (r_ref, k_ref, v_ref, lw_ref, a_ref))
        gate.append(load(g_ref))
        kk_raw = k_all * kk_ref[:, ls]
        kk_all = kk_raw * lax.rsqrt(seg_sum(kk_raw * kk_raw) + 1e-12)
        kmod_all = k_all * (1.0 + (a_all - 1.0) * ka_ref[:, ls])
        bonus.append(seg_sum(r_all * kmod_all * rk_ref[:, ls]) * v_all)
        lw_hi = lw_all.astype(BF16)
        lw_lo = (lw_all - lw_hi.astype(F32)).astype(BF16)
        for ci in range(n_chunks):
            sl = slice(ci * c, (ci + 1) * c)
            cum = (jnp.dot(tri, lw_hi[sl], preferred_element_type=F32)
                   + jnp.dot(tri, lw_lo[sl], preferred_element_type=F32))
            cum_end = cum[c - 1:c, :]
            e_neg = jnp.exp(-cum)
            e_end = jnp.exp(cum_end - cum)
            pvec = -kk_all[sl]
            cvec = kk_all[sl] * a_all[sl]
            p_s = stack(pvec * jnp.exp(cum - lw_all[sl]))
            r_s = stack(r_all[sl] * jnp.exp(cum))
            gram = _dot_nt(jnp.concatenate([p_s, r_s], axis=0),
                           jnp.concatenate([stack(cvec * e_neg), stack(kmod_all[sl] * e_neg)], axis=0))
            l_pc.append(jnp.where(low_strict, gram[0:n2, 0:n2], 0.0))
            l_pk.append(jnp.where(low_strict, gram[0:n2, n2:2 * n2], 0.0))
            a_rc.append(jnp.where(low_incl, gram[n2:2 * n2, 0:n2], 0.0))
            a_rk.append(jnp.where(low_incl, gram[n2:2 * n2, n2:2 * n2], 0.0))
            pt_s.append(p_s)
            rt_s.append(r_s)
            ch_s.append(stack(cvec * e_end))
            kh_s.append(stack(kmod_all[sl] * e_end))
            v_s.append(stack(v_all[sl]))
            w_end.append(jnp.exp(cum_end))
    t_inv = _unit_lower_inverses([-x for x in l_pc], c)
    rng = range(n_pairs * n_chunks)
    lv = [_dot(l_pk[i], v_s[i]) for i in rng]
    m1 = [_dot(t_inv[i], pt_s[i]) for i in rng]
    b1 = [_dot(t_inv[i], lv[i]) for i in rng]
    m2 = [rt_s[i] + _dot(a_rc[i], m1[i]) for i in rng]
    bv = [jnp.concatenate([b1[i], v_s[i]], axis=0) for i in rng]
    b2 = [_dot(jnp.concatenate([a_rc[i], a_rk[i]], axis=1), bv[i]) for i in rng]
    m3 = [_dot_tn(ch_s[i], m1[i]) for i in rng]
    b3 = [_dot_tn(bv[i], jnp.concatenate([ch_s[i], kh_s[i]], axis=0)) for i in rng]

    for p in range(n_pairs):
        ls = slice(p * LANES, (p + 1) * LANES)
        z = z_scr[p]
        ys = []
        for ci in range(n_chunks):
            i = p * n_chunks + ci
            y_s = _dot_nt(m2[i], z) + b2[i]
            z = z * w_end[i] + _dot_nt(z, m3[i]) + b3[i]
            ys.append(y_s[0:c] + y_s[c:n2])
        z_scr[p] = z
        y = jnp.concatenate(ys, axis=0) if n_chunks > 1 else ys[0]
        mean = seg_sum(y) * (1.0 / RW_HEAD)
        dev = y - mean
        var = seg_sum(dev * dev) * (1.0 / RW_HEAD)
        yn = dev * lax.rsqrt(var + RW_GN_EPS) * lnw_ref[:, ls] + lnb_ref[:, ls]
        out = (yn + bonus[p]) * gate[p]
        o_ref[:, ls] = out[0:tt].astype(o_ref.dtype)

    @pl.when(t == nt - 1)
    def _():
        sout_ref[...] = z_scr[...]


def _wkv7(r, k, v, lw, a, g, k_k, k_a, r_k, ln_w, ln_b, s0_pairs, n_batch, seq_len):
    m, d = r.shape
    n_pairs = d // LANES
    tt = _tile(seq_len, 512)
    nt = seq_len // tt
    chunk = min(RW_CHUNK, max(seq_len, SUBLANES))
    group = min(n_pairs, max(1, WKV_ITEMS_PER_STEP // max(tt // chunk, 1)))
    row_spec = pl.BlockSpec((tt, group * LANES), lambda b, hp, t: (b * nt + t, hp))
    vec_spec = pl.BlockSpec((1, group * LANES), lambda b, hp, t: (0, hp))
    st_spec = pl.BlockSpec((None, group, LANES, LANES), lambda b, hp, t: (b, hp, 0, 0))
    vec = lambda x: x.reshape(1, d)
    return pl.pallas_call(
        functools.partial(_wkv_kernel, tt=tt, chunk=chunk),
        grid=(n_batch, n_pairs // group, nt),
        in_specs=[row_spec] * 6 + [vec_spec] * 5 + [st_spec],
        out_specs=[row_spec, st_spec],
        out_shape=[jax.ShapeDtypeStruct((m, d), BF16), jax.ShapeDtypeStruct(s0_pairs.shape, F32)],
        scratch_shapes=[pltpu.VMEM((group, LANES, LANES), F32)],
        compiler_params=_cparams(("parallel", "parallel", "arbitrary")),
        name="wkv7",
    )(r, k, v, lw, a, g, vec(k_k), vec(k_a), vec(r_k), vec(ln_w), vec(ln_b), s0_pairs)


def _pair_states(s):
    b, h, n, _ = s.shape
    s2 = s.reshape(b, h // 2, 2, n, n)
    zero = jnp.zeros((b, h // 2, n, n), s.dtype)
    top = jnp.concatenate([s2[:, :, 0], zero], axis=-1)
    bot = jnp.concatenate([zero, s2[:, :, 1]], axis=-1)
    return jnp.concatenate([top, bot], axis=-2)


def _unpair_states(sp):
    n = RW_HEAD
    b, hp = sp.shape[:2]
    return jnp.stack([sp[:, :, :n, :n], sp[:, :, n:, n:]], axis=2).reshape(b, 2 * hp, n, n)


def _router_kernel(x_ref, sh_ref, sc_ref, g_ref, rw_ref, rb_ref, h_ref, idx_ref, gate_ref):
    h = _rms(x_ref[...]) * g_ref[...] * (1.0 + sc_ref[...]) + sh_ref[...]
    h_ref[...] = h
    logits = jnp.dot(h, rw_ref[...], preferred_element_type=F32, precision=lax.Precision.HIGHEST) + rb_ref[...]
    lane = lax.broadcasted_iota(jnp.int32, logits.shape, 1)
    lane_f = lane.astype(F32)
    neg = -jnp.inf
    l1 = jnp.where(lane < N_EXPERTS, logits, neg)
    m1 = jnp.max(l1, axis=-1, keepdims=True)
    i1 = jnp.min(jnp.where(l1 == m1, lane_f, float(LANES)), axis=-1, keepdims=True)
    l2 = jnp.where(lane_f == i1, neg, l1)
    m2 = jnp.max(l2, axis=-1, keepdims=True)
    i2 = jnp.min(jnp.where(l2 == m2, lane_f, float(LANES)), axis=-1, keepdims=True)
    e = jnp.exp(m2 - m1)
    g0 = 1.0 / (1.0 + e)
    g1 = e / (1.0 + e)
    idx_ref[...] = jnp.where(lane == 0, i1, jnp.where(lane == 1, i2, 0.0)).astype(jnp.int32)
    gate_ref[...] = jnp.where(lane == 0, g0, jnp.where(lane == 1, g1, 0.0))


def _router(x, sh, sc, g, rw_pad, rb_pad, seq_len, tm):
    m, d = x.shape
    shm, scm = _Mod(sh, seq_len, tm), _Mod(sc, seq_len, tm)
    return pl.pallas_call(
        _router_kernel,
        grid=(m // tm,),
        in_specs=[pl.BlockSpec((tm, d), lambda i: (i, 0)), shm.spec1(), scm.spec1(),
                  pl.BlockSpec((1, d), lambda i: (0, 0)),
                  pl.BlockSpec((d, LANES), lambda i: (0, 0)), pl.BlockSpec((1, LANES), lambda i: (0, 0))],
        out_specs=[pl.BlockSpec((tm, d), lambda i: (i, 0)), pl.BlockSpec((tm, LANES), lambda i: (i, 0)),
                   pl.BlockSpec((tm, LANES), lambda i: (i, 0))],
        out_shape=[jax.ShapeDtypeStruct((m, d), F32), jax.ShapeDtypeStruct((m, LANES), jnp.int32),
                   jax.ShapeDtypeStruct((m, LANES), F32)],
        compiler_params=_cparams(("parallel",)),
        name="moe_router",
    )(x, shm.arr, scm.arr, g.reshape(1, d), rw_pad, rb_pad)


def _gather_rows_kernel(idx_ref, src_ref, o_ref, sem, *, rows):
    i = pl.program_id(0)

    def row_copy(r):
        return pltpu.make_async_copy(src_ref.at[pl.ds(idx_ref[i * rows + r], 1), :], o_ref.at[pl.ds(r, 1), :], sem)

    def start_group(grp, carry):
        for j in range(GATHER_UNROLL):
            row_copy(grp * GATHER_UNROLL + j).start(priority=j % 2)
        return carry

    def wait(r, carry):
        row_copy(r).wait()
        return carry

    lax.fori_loop(0, rows // GATHER_UNROLL, start_group, 0)
    lax.fori_loop(0, rows, wait, 0, unroll=GATHER_UNROLL)


def _gather_rows(src, row_idx, rows):
    n_rows = row_idx.shape[0]
    d = src.shape[1]
    assert n_rows % rows == 0 and rows % GATHER_UNROLL == 0
    grid_spec = pltpu.PrefetchScalarGridSpec(
        num_scalar_prefetch=1,
        grid=(n_rows // rows,),
        in_specs=[pl.BlockSpec(memory_space=pl.ANY)],
        out_specs=pl.BlockSpec((rows, d), lambda i, idx: (i, 0)),
        scratch_shapes=[pltpu.SemaphoreType.DMA(())],
    )
    return pl.pallas_call(
        functools.partial(_gather_rows_kernel, rows=rows),
        grid_spec=grid_spec,
        out_shape=jax.ShapeDtypeStruct((n_rows, d), src.dtype),
        compiler_params=_cparams(("arbitrary",)),
        name="moe_gather",
    )(row_idx, src)


def _moe_ffn_kernel(te_ref, na_ref, x_ref, wg_ref, wu_ref, wd_ref, o_ref, h_ref):
    i = pl.program_id(0)
    f = pl.program_id(1)
    active = i < na_ref[0]

    @pl.when(f == 0)
    def _():
        h_ref[...] = x_ref[...].astype(BF16)
        o_ref[...] = jnp.zeros(o_ref.shape, F32)

    @pl.when(active)
    def _():
        h = h_ref[...]
        gate = jnp.dot(h, wg_ref[...].astype(BF16), preferred_element_type=F32)
        up = jnp.dot(h, wu_ref[...].astype(BF16), preferred_element_type=F32)
        act = (_silu(gate) * up).astype(BF16)
        o_ref[...] += jnp.dot(act, wd_ref[...].astype(BF16), preferred_element_type=F32)


def _moe_ffn(xs, tile_expert, n_active, wg, wu, wd, tm):
    n_rows, d = xs.shape
    dff = wg.shape[2]
    tf = _tile(dff, MOE_FFN_TILE)
    nf = dff // tf

    def fcol(i, f, te, na):
        return jnp.where(i < na[0], f, nf - 1)

    grid_spec = pltpu.PrefetchScalarGridSpec(
        num_scalar_prefetch=2,
        grid=(n_rows // tm, nf),
        in_specs=[pl.BlockSpec((tm, d), lambda i, f, te, na: (i, 0)),
                  pl.BlockSpec((None, d, tf), lambda i, f, te, na: (te[i], 0, fcol(i, f, te, na))),
                  pl.BlockSpec((None, d, tf), lambda i, f, te, na: (te[i], 0, fcol(i, f, te, na))),
                  pl.BlockSpec((None, tf, d), lambda i, f, te, na: (te[i], fcol(i, f, te, na), 0))],
        out_specs=pl.BlockSpec((tm, d), lambda i, f, te, na: (i, 0)),
        scratch_shapes=[pltpu.VMEM((tm, d), BF16)],
    )
    return pl.pallas_call(
        _moe_ffn_kernel,
        grid_spec=grid_spec,
        out_shape=jax.ShapeDtypeStruct((n_rows, d), F32),
        compiler_params=_cparams(("parallel", "arbitrary")),
        name="moe_ffn",
    )(tile_expert, n_active, xs, wg, wu, wd)


def _combine_kernel(x_ref, gt_ref, gate_ref, g_ref, y0_ref, y1_ref, o_ref):
    gates = gate_ref[...]
    y = gates[:, 0:1] * y0_ref[...] + gates[:, 1:2] * y1_ref[...]
    x = x_ref[...] + gt_ref[...] * y
    o_ref[...] = _rms(x) * g_ref[...]


def _combine_final(x, gt, gates, y0, y1, g, row0, seq_len, tm):
    m, d = x.shape
    gtm = _Mod(gt, seq_len, tm)
    assert row0 % tm == 0
    blk0 = row0 // tm
    y_spec = pl.BlockSpec((tm, d), lambda i: (blk0 + i, 0))
    return pl.pallas_call(
        _combine_kernel,
        grid=(m // tm,),
        in_specs=[pl.BlockSpec((tm, d), lambda i: (i, 0)), gtm.spec1(),
                  pl.BlockSpec((tm, LANES), lambda i: (i, 0)),
                  pl.BlockSpec((1, d), lambda i: (0, 0)), y_spec, y_spec],
        out_specs=pl.BlockSpec((tm, d), lambda i: (i, 0)),
        out_shape=jax.ShapeDtypeStruct((m, d), F32),
        compiler_params=_cparams(("parallel",)),
        name="moe_combine_final",
    )(x, gtm.arr, gates, g.reshape(1, d), y0, y1)


def _split_mod(mod):
    return [mod[:, n * D_MODEL:(n + 1) * D_MODEL] for n in range(6)]


def _layer_even(x, mod, pos, conv_state, gdn_state, paged, p, n_batch, seq_len, tm):
    sh1, sc1, gt1, sh2, sc2, gt2 = mod
    m = x.shape[0]
    if seq_len % tm == 0:
        tabs = _rope_tables(pos)
    else:
        tabs = tuple(jnp.tile(t, (n_batch, 1)) for t in _rope_tables(pos))
    gq, gk, gv, gz, ba = _norm_proj(x, sh1, sc1, p["norm_mix0"], p["w_gdn"], (False,) * 4, p["w_ba"], None,
                                    seq_len, tm)
    dq, dk, dv = _norm_proj(x, sh1, sc1, p["norm_mix0"], p["w_da"], (True, True, False), None, tabs, seq_len, tm)
    o_gdn, new_gdn = _gated_deltanet(gq, gk, gv, gz, ba, conv_state, gdn_state, p["gdn_conv_w"], p["gdn_a_log"],
                                     p["gdn_dt_bias"], p["gdn_norm_w"], n_batch, seq_len)
    lam_args = (p["da_lq1"], p["da_lk1"], p["da_lq2"], p["da_lk2"], p["da_subln_w"])
    if paged is None:
        o_da = _diff_attention_prompt(dq, dk, dv, *lam_args, n_batch, seq_len)
    else:
        o_da = _diff_attention_sample(dq, dk, dv, paged[0], paged[1], paged[2], *lam_args, n_batch, seq_len)
    x = _proj_residual(x, gt1, [o_gdn, o_da], p["w_out"], seq_len, tm)
    x = _norm_ffn(x, sh2, sc2, gt2, p["norm_ffn0"], p["ffn_wg"], p["ffn_wu"], p["ffn_wd"], seq_len, tm)
    raw = jnp.concatenate([gq.reshape(n_batch, seq_len, -1), gk.reshape(n_batch, seq_len, -1),
                           gv.reshape(n_batch, seq_len, -1)], axis=-1)
    ext = jnp.concatenate([conv_state, raw[:, max(seq_len - (GDN_CONV - 1), 0):]], axis=1)
    new_conv = ext[:, ext.shape[1] - (GDN_CONV - 1):]
    k_out = dk.reshape(n_batch, seq_len, DA_HEADS, 2 * DA_DK)
    v_out = dv.reshape(n_batch, seq_len, DA_HEADS, DA_DV)
    return x, k_out, v_out, new_conv, new_gdn


def _layer_odd_mix(x, mod, shift_state, wkv_state, p, n_batch, seq_len, tm):
    sh1, sc1, gt1 = mod[:3]
    d = x.shape[1]
    tm_rw = _tile(tm, 512)
    r, k, v, lw, a, g = _rwkv_proj(x, sh1, sc1, shift_state, p["norm_mix1"], p["rw_mu"], p["rw_w1"], p["rw_a1"],
                                   p["rw_g1"], p["rw_wr"], p["rw_wk"], p["rw_wv"], p["rw_w2"], p["rw_a2"],
                                   p["rw_g2"], p["rw_w0"], p["rw_a0"], n_batch, seq_len, tm_rw)
    yg, s_pairs = _wkv7(r, k, v, lw, a, g, p["rw_k_k"], p["rw_k_a"], p["rw_r_k"], p["rw_ln_w"], p["rw_ln_b"],
                        _pair_states(wkv_state), n_batch, seq_len)
    last = x.reshape(n_batch, seq_len, d)[:, seq_len - 1]
    new_shift = _norm_mod(last, sh1, sc1, p["norm_mix1"])
    x = _proj_residual(x, gt1, [yg], [p["rw_wo"]], seq_len, tm)
    return x, new_shift, _unpair_states(s_pairs)


def kernel(x_prompt, x_sample, cache_k, cache_v, state_gdn_conv, state_gdn, state_rwkv_shift, state_rwkv, page_table, c_prompt, c_sample, ada_w0, ada_b0, norm_mix0, w_in0, gdn_conv_w, gdn_a_log, gdn_dt_bias, gdn_norm_w, da_lq1, da_lk1, da_lq2, da_lk2, da_subln_w, w_out0, norm_ffn0, ffn_w_gate, ffn_w_up, ffn_w_down, ada_w1, ada_b1, norm_mix1, rw_mu, rw_w0, rw_w1, rw_w2, rw_a0, rw_a1, rw_a2, rw_g1, rw_g2, rw_k_k, rw_k_a, rw_r_k, rw_wr, rw_wk, rw_wv, rw_wo, rw_ln_w, rw_ln_b, norm_ffn1, moe_router_w, moe_router_b, moe_w_gate, moe_w_up, moe_w_down, norm_final):
    bp, lp, d = x_prompt.shape
    bs, ls, _ = x_sample.shape
    n_pages = page_table.shape[1]
    past_len = n_pages * PAGE_SIZE
    mp, ms = bp * lp, bs * ls
    tm_p = _tile(lp, 1024)
    tm_s = _tile(ms, 256)

    qk_w = GDN_HEADS * GDN_DK
    c0 = 3 * qk_w
    c1 = c0 + qk_w
    c2 = c1 + 2 * GDN_HEADS
    da_w = DA_HEADS * 2 * DA_DK
    bf = lambda w: w.astype(BF16)
    w_ba = jnp.concatenate([w_in0[:, c1:c2], jnp.zeros((d, LANES - 2 * GDN_HEADS), F32)], axis=1)
    lora_pad = lambda w, axis: jnp.pad(w, [(0, (-w.shape[a]) % LANES if a == axis else 0) for a in range(2)])
    p = {
        "norm_mix0": norm_mix0,
        "w_gdn": [bf(w_in0[:, n * qk_w:(n + 1) * qk_w]) for n in range(4)],
        "w_ba": bf(w_ba),
        "w_da": [bf(w_in0[:, c2 + n * da_w:c2 + (n + 1) * da_w]) for n in range(3)],
        "gdn_conv_w": gdn_conv_w, "gdn_a_log": gdn_a_log, "gdn_dt_bias": gdn_dt_bias, "gdn_norm_w": gdn_norm_w,
        "da_lq1": da_lq1, "da_lk1": da_lk1, "da_lq2": da_lq2, "da_lk2": da_lk2, "da_subln_w": da_subln_w,
        "w_out": [bf(w_out0[:qk_w]), bf(w_out0[qk_w:])],
        "norm_ffn0": norm_ffn0, "ffn_wg": bf(ffn_w_gate), "ffn_wu": bf(ffn_w_up), "ffn_wd": bf(ffn_w_down),
        "norm_mix1": norm_mix1, "rw_mu": rw_mu, "rw_w0": rw_w0, "rw_a0": rw_a0,
        "rw_w1": bf(lora_pad(rw_w1, 1)), "rw_w2": bf(lora_pad(rw_w2, 0)),
        "rw_a1": bf(lora_pad(rw_a1, 1)), "rw_a2": bf(lora_pad(rw_a2, 0)),
        "rw_g1": bf(rw_g1), "rw_g2": bf(rw_g2),
        "rw_wr": bf(rw_wr), "rw_wk": bf(rw_wk), "rw_wv": bf(rw_wv), "rw_wo": bf(rw_wo),
        "rw_k_k": rw_k_k, "rw_k_a": rw_k_a, "rw_r_k": rw_r_k.reshape(-1), "rw_ln_w": rw_ln_w, "rw_ln_b": rw_ln_b,
    }

    c_all = jnp.concatenate([c_prompt, c_sample], axis=0)
    mod0 = _ada_mod(c_all, ada_w0, ada_b0)
    mod1 = _ada_mod(c_all, ada_w1, ada_b1)
    mod0_p, mod0_s = _split_mod(mod0[:bp]), _split_mod(mod0[bp:])
    mod1_p, mod1_s = _split_mod(mod1[:bp]), _split_mod(mod1[bp:])

    xp = x_prompt.reshape(mp, d)
    xs = x_sample.reshape(ms, d)
    pos_p = jnp.arange(lp, dtype=jnp.int32)
    pos_s = past_len + jnp.arange(ls, dtype=jnp.int32)

    xp, k_p, v_p, conv_p, gdn_p = _layer_even(
        xp, mod0_p, pos_p, jnp.zeros((bp, GDN_CONV - 1, 3 * qk_w), F32),
        jnp.zeros((bp, GDN_HEADS, GDN_DK, GDN_DK), F32), None, p, bp, lp, tm_p)
    xs, k_s, v_s, conv_s, gdn_s = _layer_even(
        xs, mod0_s, pos_s, state_gdn_conv, state_gdn, (cache_k, cache_v, page_table), p, bs, ls, tm_s)

    n_rw_heads = d // RW_HEAD
    xp, shift_p, rw_p = _layer_odd_mix(xp, mod1_p, jnp.zeros((bp, d), F32),
                                       jnp.zeros((bp, n_rw_heads, RW_HEAD, RW_HEAD), F32), p, bp, lp, tm_p)
    xs, shift_s, rw_s = _layer_odd_mix(xs, mod1_s, state_rwkv_shift, state_rwkv, p, bs, ls, tm_s)

    rw_pad = jnp.concatenate([moe_router_w, jnp.zeros((d, LANES - N_EXPERTS), F32)], axis=1)
    rb_pad = jnp.concatenate([moe_router_b, jnp.zeros((LANES - N_EXPERTS,), F32)]).reshape(1, LANES)
    tm_r = _tile(lp, 512)
    h_p, idx_p, gate_p = _router(xp, mod1_p[3], mod1_p[4], norm_ffn1, rw_pad, rb_pad, lp, tm_r)
    h_s, idx_s, gate_s = _router(xs, mod1_s[3], mod1_s[4], norm_ffn1, rw_pad, rb_pad, ls, tm_s)
    h_all = jnp.concatenate([h_p, h_s], axis=0)
    flat_e = jnp.concatenate([idx_p[:, :TOP_K], idx_s[:, :TOP_K]], axis=0).reshape(-1)
    n_assign = flat_e.shape[0]
    tmoe = MOE_TILE
    onehot = (flat_e[:, None] == jnp.arange(N_EXPERTS, dtype=jnp.int32)[None, :]).astype(jnp.int32)
    running = jnp.cumsum(onehot, axis=0)
    pos_in_e = jnp.sum((running - onehot) * onehot, axis=1)
    counts = running[-1]
    padded = (counts + tmoe - 1) // tmoe * tmoe
    pad_end = jnp.cumsum(padded)
    pad_start = pad_end - padded
    dest = (pad_start[flat_e] + pos_in_e).astype(jnp.int32)
    n_tiles = -(-n_assign // tmoe) + N_EXPERTS
    n_rows = n_tiles * tmoe
    row_tok = jnp.zeros((n_rows,), jnp.int32).at[dest].set(jnp.arange(n_assign, dtype=jnp.int32) // TOP_K)
    n_active = (pad_end[-1] // tmoe).astype(jnp.int32)
    tile_ids = jnp.minimum(jnp.arange(n_tiles, dtype=jnp.int32), n_active - 1)
    tile_e = jnp.minimum(jnp.searchsorted(pad_end, tile_ids * tmoe, side="right"), N_EXPERTS - 1).astype(jnp.int32)
    x_sorted = _gather_rows(h_all, row_tok, tmoe)
    yb = _moe_ffn(x_sorted, tile_e, n_active.reshape(1), moe_w_gate, moe_w_up, moe_w_down, tmoe)
    dest2 = dest.reshape(-1, TOP_K)
    rows_c = _tile(mp + ms, GATHER_ROWS)
    y0 = _gather_rows(yb, dest2[:, 0], rows_c)
    y1 = _gather_rows(yb, dest2[:, 1], rows_c)
    tm_c = _tile(lp, 256)
    y_p = _combine_final(xp, mod1_p[5], gate_p, y0, y1, norm_final, 0, lp, tm_c)
    y_s = _combine_final(xs, mod1_s[5], gate_s, y0, y1, norm_final, mp, ls, tm_s)

    return (y_p.reshape(bp, lp, d), y_s.reshape(bs, ls, d), k_p, v_p, k_s, v_s, conv_p, conv_s,
            gdn_p, gdn_s, shift_p, shift_s, rw_p, rw_s)
```
